```python
import jax
import jax.numpy as jnp
from jax import lax
import numpy as np

D_MODEL = 4096
BATCH = 1
SEQ = 16384
DEPTH = 4

GRID_W = 64
CTX_LEN = 256
FOURIER_WIDTH = D_MODEL // 4
FOURIER_HEADS = 4
FOURIER_DIM = FOURIER_WIDTH // FOURIER_HEADS
NA_WIDTH = 3 * D_MODEL // 8
NA_HEAD_DIM = 128
NA_HEADS = NA_WIDTH // NA_HEAD_DIM
NA_KH = 8
NA_KW = 16
RWKV_WIDTH = D_MODEL - FOURIER_WIDTH - NA_WIDTH
RWKV_HEAD_DIM = 64
RWKV_HEADS = RWKV_WIDTH // RWKV_HEAD_DIM
W_LORA = 64
A_LORA = 64
G_LORA = 224
D_FF = 5120
ROPE_THETA = 10000.0
NORM_EPS = 1e-6
GN_EPS = 64e-5
OFF_NA = FOURIER_WIDTH
OFF_RWKV = OFF_NA + 3 * NA_WIDTH
OFF_LORA = OFF_RWKV + 3 * RWKV_WIDTH
IN_COLS = OFF_LORA + 2 * W_LORA + 2 * A_LORA + G_LORA

kernel_name = 'hybrid_fourier_na_rwkv7_dit'


def _rmsnorm(x, w):
    x32 = x.astype(jnp.float32)
    y = x32 * lax.rsqrt(jnp.mean(x32 * x32, axis=-1, keepdims=True) + NORM_EPS)
    return (y * w.astype(jnp.float32)).astype(x.dtype)


def _dwconv3(x, w):
    xp = jnp.pad(x, ((0, 0), (1, 1), (0, 0)))
    return xp[:, :-2] * w[0] + xp[:, 1:-1] * w[1] + xp[:, 2:] * w[2]


def _fourier_mix(f, w, b):
    B, T, _ = f.shape
    fh = f.reshape(B, T, FOURIER_HEADS, FOURIER_DIM).astype(jnp.float32)
    spec = jnp.fft.fftn(fh, axes=(1, 3), norm='ortho').real.astype(f.dtype)
    out = jnp.einsum('bthc,hcd->bthd', spec, w) + b
    return out.reshape(B, T, FOURIER_WIDTH)


def _rope(t, pos):
    nf = t.shape[-1] // 2
    inv = 1.0 / (ROPE_THETA ** (jnp.arange(nf, dtype=jnp.float32) / nf))
    ang = pos.astype(jnp.float32)[:, None] * inv[None, :]
    cos = jnp.cos(ang)[None, :, None, :]
    sin = jnp.sin(ang)[None, :, None, :]
    t32 = t.astype(jnp.float32)
    t1, t2 = t32[..., :nf], t32[..., nf:]
    return jnp.concatenate([t1 * cos - t2 * sin, t2 * cos + t1 * sin], axis=-1)


def _axial_rope(x, row_pos, col_pos):
    half = x.shape[-1] // 2
    return jnp.concatenate([_rope(x[..., :half], row_pos), _rope(x[..., half:], col_pos)], axis=-1).astype(x.dtype)


def _heads_norm(t, w, n_heads):
    B, T, _ = t.shape
    return _rmsnorm(t.reshape(B, T, n_heads, -1), w)


def _neighbourhood_attention(q, k, v, k_ctx, v_ctx, rpb):
    B, S, H, D = q.shape
    rows = S // GRID_W
    kh = min(NA_KH, rows)
    kw = NA_KW
    qg = q.reshape(B, rows, GRID_W, H, D)
    kg = k.reshape(B, rows, GRID_W, H, D)
    vg = v.reshape(B, rows, GRID_W, H, D)
    cols = jnp.arange(GRID_W)
    col_start = jnp.clip(cols - kw // 2, 0, GRID_W - kw)
    col_idx = col_start[:, None] + jnp.arange(kw)[None, :]
    dc = col_idx - cols[:, None] + (NA_KW - 1)
    scale = NA_HEAD_DIM ** -0.5

    def row_block(r):
        rs = jnp.clip(r - kh // 2, 0, rows - kh)
        q_r = lax.dynamic_index_in_dim(qg, r, axis=1, keepdims=False)
        k_band = lax.dynamic_slice_in_dim(kg, rs, kh, axis=1)
        v_band = lax.dynamic_slice_in_dim(vg, rs, kh, axis=1)
        k_win = k_band[:, :, col_idx]
        v_win = v_band[:, :, col_idx]
        dr = rs + jnp.arange(kh) - r + (NA_KH - 1)
        bias = rpb[:, dr[:, None, None], dc[None, :, :]]
        bias = jnp.transpose(bias, (0, 2, 1, 3)).astype(jnp.float32)
        s_loc = jnp.einsum('bjhd,bajkhd->bhjak', q_r, k_win).astype(jnp.float32) * scale + bias[None]
        s_ctx = jnp.einsum('bjhd,bchd->bhjc', q_r, k_ctx).astype(jnp.float32) * scale
        s = jnp.concatenate([s_loc.reshape(B, H, GRID_W, kh * kw), s_ctx], axis=-1)
        p = jax.nn.softmax(s, axis=-1).astype(v.dtype)
        p_loc = p[..., :kh * kw].reshape(B, H, GRID_W, kh, kw)
        p_ctx = p[..., kh * kw:]
        return (jnp.einsum('bhjak,bajkhd->bjhd', p_loc, v_win)
                + jnp.einsum('bhjc,bchd->bjhd', p_ctx, v_ctx))

    out = lax.map(row_block, jnp.arange(rows))
    return jnp.moveaxis(out, 0, 1).reshape(B, S, H * D)


def _context_attention(q, k, v):
    B, C, H, D = q.shape
    s = jnp.einsum('bqhd,bkhd->bhqk', q, k).astype(jnp.float32) * (D ** -0.5)
    p = jax.nn.softmax(s, axis=-1).astype(v.dtype)
    return jnp.einsum('bhqk,bkhd->bqhd', p, v).reshape(B, C, H * D)


def _rwkv7_scan(r, w, k, v, a, b, s0, reverse):
    def step(s, inp):
        r_t, w_t, k_t, v_t, a_t, b_t = inp
        sa = jnp.einsum('bhij,bhj->bhi', s, a_t)
        s = s * w_t[:, :, None, :] + v_t[..., :, None] * k_t[..., None, :] + sa[..., :, None] * b_t[..., None, :]
        return s, jnp.einsum('bhij,bhj->bhi', s, r_t)
    xs = tuple(jnp.moveaxis(t.astype(jnp.float32), 1, 0) for t in (r, w, k, v, a, b))
    s_final, ys = lax.scan(step, s0, xs, reverse=reverse)
    return jnp.moveaxis(ys, 0, 1), s_final


def _rwkv7_mix(rkv, lora, p, s0_f, s0_b):
    B, T, _ = rkv.shape
    heads = lambda t: t.reshape(B, T, RWKV_HEADS, RWKV_HEAD_DIM)
    r, k, v = jnp.split(rkv.astype(jnp.float32), 3, axis=-1)
    lora = lora.astype(jnp.float32)
    lw = lora[..., :2 * W_LORA].reshape(B, T, 2, W_LORA)
    la = lora[..., 2 * W_LORA:2 * (W_LORA + A_LORA)].reshape(B, T, 2, A_LORA)
    lg = lora[..., 2 * (W_LORA + A_LORA):]
    w_log = -jax.nn.softplus(-(p['w0'] + jnp.einsum('btdl,dlc->btdc', jnp.tanh(lw), p['w2']))) - 0.5
    decay = jnp.exp(-jnp.exp(w_log))
    a = jax.nn.sigmoid(p['a0'] + jnp.einsum('btdl,dlc->btdc', la, p['a2']))
    g = jax.nn.sigmoid(lg) @ p['g2']
    kk = heads(k * p['k_k'])
    kk = (kk / jnp.maximum(jnp.linalg.norm(kk, axis=-1, keepdims=True), 1e-12)).reshape(B, T, RWKV_WIDTH)
    k_dir = k[:, :, None, :] * (1 + (a - 1) * p['k_a'])
    y_f, s_f = _rwkv7_scan(heads(r), heads(decay[:, :, 0]), heads(k_dir[:, :, 0]), heads(v),
                           heads(-kk), heads(kk * a[:, :, 0]), s0_f, False)
    y_b, s_b = _rwkv7_scan(heads(r), heads(decay[:, :, 1]), heads(k_dir[:, :, 1]), heads(v),
                           heads(-kk), heads(kk * a[:, :, 1]), s0_b, True)
    y = y_f + y_b
    mu = jnp.mean(y, axis=-1, keepdims=True)
    var = jnp.mean(jnp.square(y - mu), axis=-1, keepdims=True)
    y = ((y - mu) * lax.rsqrt(var + GN_EPS)).reshape(B, T, RWKV_WIDTH) * p['gn_w'] + p['gn_b']
    bonus = jnp.sum(heads(r * (k_dir[:, :, 0] + k_dir[:, :, 1])) * p['r_k'], axis=-1, keepdims=True) * heads(v)
    out = (y + bonus.reshape(B, T, RWKV_WIDTH)) * g
    return out.astype(rkv.dtype), s_f, s_b


def _conv_ffn(h, p):
    u = _dwconv3(h @ p['w_ffn_in'], p['ffn_conv'])
    a, b = jnp.split(u, 2, axis=-1)
    return (jax.nn.silu(a) * b) @ p['w_ffn_out']


def _layer(x, ctx, mod_x, mod_c, p, row_pos, col_pos, ctx_out):
    B, S, _ = x.shape
    C = ctx.shape[1]
    sh1, sc1, gt1, sh2, sc2, gt2 = jnp.split(mod_x[:, None, :], 6, axis=-1)
    csh1, csc1, cgt1, csh2, csc2, cgt2 = jnp.split(mod_c[None, None, :], 6, axis=-1)
    hx = _rmsnorm(x, p['norm1_w']) * (1 + sc1) + sh1
    hc = _rmsnorm(ctx, p['norm1_w']) * (1 + csc1) + csh1
    zx = hx @ p['w_in']
    zc = hc @ p['w_in']
    fx = _fourier_mix(zx[..., :OFF_NA], p['fourier_w'], p['fourier_b'])
    qx, kx, vx = jnp.split(zx[..., OFF_NA:OFF_RWKV], 3, axis=-1)
    qc, kc, vc = jnp.split(zc[..., OFF_NA:OFF_RWKV], 3, axis=-1)
    qx = _axial_rope(_heads_norm(qx, p['q_norm_w'], NA_HEADS), row_pos, col_pos)
    kx = _axial_rope(_heads_norm(kx, p['k_norm_w'], NA_HEADS), row_pos, col_pos)
    vx = vx.reshape(B, S, NA_HEADS, NA_HEAD_DIM)
    kc = _heads_norm(kc, p['k_norm_w'], NA_HEADS)
    vc = vc.reshape(B, C, NA_HEADS, NA_HEAD_DIM)
    ax = _neighbourhood_attention(qx, kx, vx, kc, vc, p['rpb'])
    s0 = jnp.zeros((B, RWKV_HEADS, RWKV_HEAD_DIM, RWKV_HEAD_DIM), jnp.float32)
    rc, s_f, s_b = _rwkv7_mix(_dwconv3(zc[..., OFF_RWKV:OFF_LORA], p['rwkv_conv']), zc[..., OFF_LORA:], p, s0, s0)
    rx, _, _ = _rwkv7_mix(_dwconv3(zx[..., OFF_RWKV:OFF_LORA], p['rwkv_conv']), zx[..., OFF_LORA:], p, s_f, s_b)
    x = x + gt1 * (jnp.concatenate([fx, ax, rx], axis=-1) @ p['w_out'])
    x = x + gt2 * _conv_ffn(_rmsnorm(x, p['norm2_w']) * (1 + sc2) + sh2, p)
    if ctx_out:
        fc = _fourier_mix(zc[..., :OFF_NA], p['fourier_w'], p['fourier_b'])
        ac = _context_attention(_heads_norm(qc, p['q_norm_w'], NA_HEADS), kc, vc)
        ctx = ctx + cgt1 * (jnp.concatenate([fc, ac, rc], axis=-1) @ p['w_out'])
        ctx = ctx + cgt2 * _conv_ffn(_rmsnorm(ctx, p['norm2_w']) * (1 + csc2) + csh2, p)
    return x, ctx


def setup_inputs(seed: int = 0) -> dict:
    key = jax.random.key(seed)
    ks = jax.random.split(key, 32)
    L, D = DEPTH, D_MODEL
    nrm = lambda k, shape, s: s * jax.random.normal(k, shape, jnp.float32)
    return {
        'x': nrm(ks[0], (BATCH, SEQ, D), 1.0),
        'c': nrm(ks[1], (BATCH, D), 1.0),
        'ctx': nrm(ks[2], (BATCH, CTX_LEN, D), 1.0),
        'c_ctx': nrm(ks[3], (D,), 1.0),
        'ada_w': nrm(ks[4], (L, D, 6 * D), 0.5 * D ** -0.5),
        'ada_b': nrm(ks[5], (L, 6 * D), 0.02),
        'norm1_w': 1.0 + nrm(ks[6], (L, D), 0.02),
        'norm2_w': 1.0 + nrm(ks[7], (L, D), 0.02),
        'w_in': nrm(ks[8], (L, D, IN_COLS), D ** -0.5),
        'fourier_w': nrm(ks[9], (L, FOURIER_HEADS, FOURIER_DIM, FOURIER_DIM), FOURIER_DIM ** -0.5),
        'fourier_b': nrm(ks[10], (L, FOURIER_HEADS, FOURIER_DIM), 0.02),
        'q_norm_w': 1.0 + nrm(ks[11], (L, NA_HEAD_DIM), 0.02),
        'k_norm_w': 1.0 + nrm(ks[12], (L, NA_HEAD_DIM), 0.02),
        'rpb': nrm(ks[13], (L, NA_HEADS, 2 * NA_KH - 1, 2 * NA_KW - 1), 0.2),
        'rwkv_conv': nrm(ks[14], (L, 3, 3 * RWKV_WIDTH), 3 ** -0.5),
        'w0': jax.random.uniform(ks[15], (L, 2, RWKV_WIDTH), jnp.float32, -6.0, -1.0),
        'w2': nrm(ks[16], (L, 2, W_LORA, RWKV_WIDTH), 0.1 * W_LORA ** -0.5),
        'a0': nrm(ks[17], (L, 2, RWKV_WIDTH), 0.5),
        'a2': nrm(ks[18], (L, 2, A_LORA, RWKV_WIDTH), 0.5 * A_LORA ** -0.5),
        'g2': nrm(ks[19], (L, G_LORA, RWKV_WIDTH), G_LORA ** -0.5),
        'k_k': 0.85 + nrm(ks[20], (L, RWKV_WIDTH), 0.05),
        'k_a': 1.0 + nrm(ks[21], (L, RWKV_WIDTH), 0.05),
        'r_k': nrm(ks[22], (L, RWKV_HEADS, RWKV_HEAD_DIM), 0.1),
        'gn_w': 1.0 + nrm(ks[23], (L, RWKV_WIDTH), 0.02),
        'gn_b': nrm(ks[24], (L, RWKV_WIDTH), 0.02),
        'w_out': nrm(ks[25], (L, D, D), D ** -0.5),
        'ffn_conv': nrm(ks[26], (L, 3, 2 * D_FF), 3 ** -0.5),
        'w_ffn_in': nrm(ks[27], (L, D, 2 * D_FF), D ** -0.5),
        'w_ffn_out': nrm(ks[28], (L, D_FF, D), D_FF ** -0.5),
    }


def reference(x, c, ctx, c_ctx, ada_w, ada_b, norm1_w, norm2_w, w_in, fourier_w, fourier_b,
              q_norm_w, k_norm_w, rpb, rwkv_conv, w0, w2, a0, a2, g2, k_k, k_a, r_k, gn_w, gn_b,
              w_out, ffn_conv, w_ffn_in, w_ffn_out):
    S = x.shape[1]
    t = jnp.arange(S)
    row_pos = t // GRID_W
    col_pos = t % GRID_W
    silu_c = jax.nn.silu(c)
    silu_cc = jax.nn.silu(c_ctx)
    for l in range(DEPTH):
        p = {
            'norm1_w': norm1_w[l], 'norm2_w': norm2_w[l], 'w_in': w_in[l],
            'fourier_w': fourier_w[l], 'fourier_b': fourier_b[l],
            'q_norm_w': q_norm_w[l], 'k_norm_w': k_norm_w[l], 'rpb': rpb[l],
            'rwkv_conv': rwkv_conv[l], 'w0': w0[l], 'w2': w2[l], 'a0': a0[l], 'a2': a2[l],
            'g2': g2[l], 'k_k': k_k[l], 'k_a': k_a[l], 'r_k': r_k[l], 'gn_w': gn_w[l], 'gn_b': gn_b[l],
            'w_out': w_out[l], 'ffn_conv': ffn_conv[l], 'w_ffn_in': w_ffn_in[l], 'w_ffn_out': w_ffn_out[l],
        }
        mod_x = silu_c @ ada_w[l] + ada_b[l]
        mod_c = silu_cc @ ada_w[l] + ada_b[l]
        x, ctx = _layer(x, ctx, mod_x, mod_c, p, row_pos, col_pos, l < DEPTH - 1)
    return x
```

```python
import functools
import math

import jax
import jax.numpy as jnp
import numpy as np
from jax import lax
from jax.experimental import pallas as pl
from jax.experimental.pallas import tpu as pltpu

F32 = jnp.float32
BF16 = jnp.bfloat16

D_MODEL = 4096
GRID_W = 64
FOURIER_WIDTH = 1024
FOURIER_HEADS = 4
FOURIER_DIM = 256
NA_WIDTH = 1536
NA_HEAD_DIM = 128
NA_HEADS = 12
NA_KH = 8
NA_KW = 16
RWKV_WIDTH = 1536
RWKV_HEAD_DIM = 64
W_LORA = 64
A_LORA = 64
G_LORA = 224
D_FF = 5120
ROPE_THETA = 10000.0
NORM_EPS = 1e-6
GN_EPS = 64e-5

COL_Q = 0
COL_K = NA_WIDTH
COL_V = 2 * NA_WIDTH
COL_R = 3 * NA_WIDTH
COL_F = 6 * NA_WIDTH
COL_L = COL_F + FOURIER_WIDTH
LORA_PAD = 512
IN_COLS_PAD = COL_L + LORA_PAD

CHUNK = 64
GROUP = 256
N_GROUPS = RWKV_WIDTH // GROUP
FFT_B = 128

LANE = 128
VMEM_LIMIT = 48 * 1024 * 1024


def _cparams(sem):
    return pltpu.CompilerParams(dimension_semantics=sem, vmem_limit_bytes=VMEM_LIMIT)


def _dot(a, b):
    return jnp.dot(a, b, preferred_element_type=F32)


def _dot_nt(a, b):
    return lax.dot_general(a, b, (((1,), (1,)), ((), ())), preferred_element_type=F32)


def _dot_tn(a, b):
    return lax.dot_general(a, b, (((0,), (0,)), ((), ())), preferred_element_type=F32)


def _split(x):
    hi = x.astype(BF16)
    lo = (x - hi.astype(F32)).astype(BF16)
    return hi, lo


def _dot3(a, b):
    ah, al = _split(a)
    bh, bl = _split(b)
    return _dot(ah, bh) + _dot(ah, bl) + _dot(al, bh)


def _ada_kernel(s_ref, w_ref, b_ref, o_ref, acc_ref):
    k = pl.program_id(2)
    tk, tn = w_ref.shape[1], w_ref.shape[2]
    rep = tn // LANE

    @pl.when(k == 0)
    def _():
        acc_ref[...] = jnp.zeros_like(acc_ref)

    def body(i, carry):
        a0, a1 = carry
        r = pl.multiple_of(i * 8, 8)
        w = w_ref[0, pl.ds(r, 8), :]
        s0 = s_ref[0, pl.ds(r, 8), :]
        s1 = s_ref[1, pl.ds(r, 8), :]
        s0 = s0 * jax.nn.sigmoid(s0)
        s1 = s1 * jax.nn.sigmoid(s1)
        a0 = a0 + w * jnp.concatenate([s0] * rep, axis=1)
        a1 = a1 + w * jnp.concatenate([s1] * rep, axis=1)
        return a0, a1

    a0, a1 = lax.fori_loop(0, tk // 8, body, (acc_ref[0], acc_ref[1]), unroll=4)
    acc_ref[0] = a0
    acc_ref[1] = a1

    @pl.when(k == pl.num_programs(2) - 1)
    def _():
        o_ref[0, 0:1, :] = jnp.sum(a0, axis=0, keepdims=True) + b_ref[0]
        o_ref[0, 1:2, :] = jnp.sum(a1, axis=0, keepdims=True) + b_ref[0]


def ada_mod(cc, ada_w, ada_b):
    L, K, N = ada_w.shape
    tk, tn = 2048, 1024
    s_b = jnp.broadcast_to(cc[:, :, None], (2, K, LANE))
    return pl.pallas_call(
        _ada_kernel,
        grid=(L, N // tn, K // tk),
        in_specs=[
            pl.BlockSpec((2, tk, LANE), lambda l, j, k: (0, k, 0)),
            pl.BlockSpec((1, tk, tn), lambda l, j, k: (l, k, j)),
            pl.BlockSpec((1, 1, tn), lambda l, j, k: (l, 0, j)),
        ],
        out_specs=pl.BlockSpec((1, 2, tn), lambda l, j, k: (l, 0, j)),
        out_shape=jax.ShapeDtypeStruct((L, 2, N), F32),
        scratch_shapes=[pltpu.VMEM((2, 8, tn), F32)],
        compiler_params=_cparams(("parallel", "parallel", "arbitrary")),
    )(s_b, ada_w, ada_b.reshape(L, 1, N))


def _nmm_kernel(x_ref, nw_ref, sc_ref, sh_ref, w_ref, o_ref, h_ref):
    @pl.when(pl.program_id(1) == 0)
    def _():
        x = x_ref[...]
        ms = jnp.mean(x * x, axis=-1, keepdims=True)
        y = x * lax.rsqrt(ms + NORM_EPS) * nw_ref[...]
        h_ref[...] = (y * (1.0 + sc_ref[...]) + sh_ref[...]).astype(BF16)

    o_ref[...] = _dot(h_ref[...], w_ref[...]).astype(o_ref.dtype)


def norm_mod_matmul(x, nw, sc, sh, w, out_dtype=F32):
    M, K = x.shape
    N = w.shape[1]
    tm = min(512, M)
    tn = 512
    return pl.pallas_call(
        _nmm_kernel,
        grid=(M // tm, N // tn),
        in_specs=[
            pl.BlockSpec((tm, K), lambda i, j: (i, 0)),
            pl.BlockSpec((1, K), lambda i, j: (0, 0)),
            pl.BlockSpec((1, K), lambda i, j: (0, 0)),
            pl.BlockSpec((1, K), lambda i, j: (0, 0)),
            pl.BlockSpec((K, tn), lambda i, j: (0, j)),
        ],
        out_specs=pl.BlockSpec((tm, tn), lambda i, j: (i, j)),
        out_shape=jax.ShapeDtypeStruct((M, N), out_dtype),
        scratch_shapes=[pltpu.VMEM((tm, K), BF16)],
        compiler_params=_cparams(("parallel", "arbitrary")),
    )(x, nw.reshape(1, K), sc.reshape(1, K), sh.reshape(1, K), w)


def _mmres_kernel(*refs, ksplits):
    n = len(ksplits)
    a_refs = refs[:n]
    w_ref, x_ref, g_ref, o_ref = refs[n:]
    acc = None
    off = 0
    for a_ref, kp in zip(a_refs, ksplits):
        part = _dot(a_ref[...].astype(BF16), w_ref[off:off + kp, :])
        acc = part if acc is None else acc + part
        off += kp
    o_ref[...] = x_ref[...] + g_ref[...] * acc


def matmul_residual(parts, w, x, gate):
    M, N = x.shape
    K = w.shape[0]
    ksplits = tuple(p.shape[1] for p in parts)
    assert sum(ksplits) == K
    tm = min(512, M)
    tn = 512
    in_specs = [pl.BlockSpec((tm, kp), lambda i, j: (i, 0)) for kp in ksplits]
    in_specs += [
        pl.BlockSpec((K, tn), lambda i, j: (0, j)),
        pl.BlockSpec((tm, tn), lambda i, j: (i, j)),
        pl.BlockSpec((1, tn), lambda i, j: (0, j)),
    ]
    return pl.pallas_call(
        functools.partial(_mmres_kernel, ksplits=ksplits),
        grid=(M // tm, N // tn),
        in_specs=in_specs,
        out_specs=pl.BlockSpec((tm, tn), lambda i, j: (i, j)),
        out_shape=jax.ShapeDtypeStruct((M, N), F32),
        compiler_params=_cparams(("parallel", "arbitrary")),
    )(*parts, w, x, gate.reshape(1, N))


def _conv3(main, prev_row, next_row, w):
    tm = main.shape[0]
    row = lax.broadcasted_iota(jnp.int32, main.shape, 0)
    dn = jnp.where(row == 0, prev_row, pltpu.roll(main, 1, axis=0))
    up = jnp.where(row == tm - 1, next_row, pltpu.roll(main, tm - 1, axis=0))
    return dn * w[0:1, :] + main * w[1:2, :] + up * w[2:3, :]


def _halo_rows(prev_ref, next_ref, i, n_i):
    prev_row = jnp.where(i == 0, 0.0, prev_ref[7:8, :])
    next_row = jnp.where(i == n_i - 1, 0.0, next_ref[0:1, :])
    return prev_row, next_row


def _halo_specs(tm, tn, n_rows, col_fn):
    r8 = tm // 8
    last8 = n_rows // 8 - 1
    return [
        pl.BlockSpec((tm, tn), lambda i, j: (i, col_fn(j))),
        pl.BlockSpec((8, tn), lambda i, j: (jnp.maximum(i * r8 - 1, 0), col_fn(j))),
        pl.BlockSpec((8, tn), lambda i, j: (jnp.minimum((i + 1) * r8, last8), col_fn(j))),
    ]


def _convgate_kernel(a_ref, ap_ref, an_ref, b_ref, bp_ref, bn_ref, wa_ref, wb_ref, o_ref):
    i = pl.program_id(0)
    n_i = pl.num_programs(0)
    pa, na = _halo_rows(ap_ref, an_ref, i, n_i)
    pb, nb = _halo_rows(bp_ref, bn_ref, i, n_i)
    a = _conv3(a_ref[...], pa, na, wa_ref[...])
    b = _conv3(b_ref[...], pb, nb, wb_ref[...])
    o_ref[...] = (a * jax.nn.sigmoid(a) * b).astype(o_ref.dtype)


def conv_gate(u, conv_w):
    M, N2 = u.shape
    F = N2 // 2
    tm = min(256, M)
    tn = 512
    nb = F // tn
    specs = _halo_specs(tm, tn, M, lambda j: j) + _halo_specs(tm, tn, M, lambda j: j + nb)
    specs += [pl.BlockSpec((3, tn), lambda i, j: (0, j)), pl.BlockSpec((3, tn), lambda i, j: (0, j + nb))]
    return pl.pallas_call(
        _convgate_kernel,
        grid=(M // tm, nb),
        in_specs=specs,
        out_specs=pl.BlockSpec((tm, tn), lambda i, j: (i, j)),
        out_shape=jax.ShapeDtypeStruct((M, F), BF16),
        compiler_params=_cparams(("parallel", "parallel")),
    )(u, u, u, u, u, u, conv_w, conv_w)


def _dft_tables(T):
    A, B = T // FFT_B, FFT_B
    ka = jnp.arange(A, dtype=jnp.int32)
    a = jnp.arange(A, dtype=jnp.int32)
    b = jnp.arange(B, dtype=jnp.int32)
    n = (ka[None, :, None] * (B * a[None, None, :] + b[:, None, None])) % T
    ang = n.astype(F32) * (2.0 * math.pi / T)
    sa = 1.0 / math.sqrt(A)
    m1 = jnp.concatenate([jnp.cos(ang) * sa, -jnp.sin(ang) * sa], axis=1).astype(BF16)
    kb = jnp.arange(B, dtype=jnp.int32)
    n2 = (kb[:, None] * b[None, :]) % B
    ang2 = n2.astype(F32) * (2.0 * math.pi / B)
    sb = 1.0 / math.sqrt(B)
    m2 = jnp.concatenate([jnp.cos(ang2) * sb, jnp.sin(ang2) * sb], axis=1).astype(BF16)
    return m1, m2


def _channel_dft():
    c = np.arange(FOURIER_DIM)
    ang = 2.0 * np.pi * ((c[:, None] * c[None, :]) % FOURIER_DIM) / FOURIER_DIM
    s = 1.0 / math.sqrt(FOURIER_DIM)
    return np.cos(ang) * s, np.sin(ang) * s


def _f1_kernel(x_ref, m_ref, ch_ref, re_ref, im_ref):
    A = x_ref.shape[0]
    y = _dot(m_ref[0], x_ref[...].astype(BF16))
    ch = ch_ref[...]
    for h in range(2):
        sl = slice(h * FOURIER_DIM, (h + 1) * FOURIER_DIM)
        lhs = jnp.concatenate([y[:A, sl], y[A:, sl]], axis=1).astype(BF16)
        yp = _dot(lhs, ch)
        re_ref[0, :, sl] = yp[:, :FOURIER_DIM].astype(BF16)
        im_ref[0, :, sl] = yp[:, FOURIER_DIM:].astype(BF16)


def _f2_kernel(re_ref, im_ref, m_ref, w_ref, b_ref, o_ref, *, nka):
    rhs = jnp.concatenate([re_ref[...], im_ref[...]], axis=0)
    spec = _dot(m_ref[...], rhs)
    for q in range(nka):
        for h in range(FOURIER_HEADS):
            off = q * FOURIER_WIDTH + h * FOURIER_DIM
            out = _dot(spec[:, off:off + FOURIER_DIM].astype(BF16), w_ref[h].astype(BF16)) + b_ref[h]
            o_ref[:, off:off + FOURIER_DIM] = out.astype(o_ref.dtype)


def fourier_mix(z, fw, fb):
    T, NC = z.shape
    A, B = T // FFT_B, FFT_B
    m1, m2 = _dft_tables(T)
    cc, ss = _channel_dft()
    ch = jnp.asarray(np.block([[cc, -ss], [ss, cc]]), BF16)
    z2 = z.reshape(A, B * NC)
    cb0 = COL_F // 512
    ncb = NC // 512
    yre, yim = pl.pallas_call(
        _f1_kernel,
        grid=(B, 2),
        in_specs=[
            pl.BlockSpec((A, 512), lambda b, c: (0, b * ncb + cb0 + c)),
            pl.BlockSpec((1, 2 * A, A), lambda b, c: (b, 0, 0)),
            pl.BlockSpec((512, 512), lambda b, c: (0, 0)),
        ],
        out_specs=[pl.BlockSpec((1, A, 512), lambda b, c: (b, 0, c))] * 2,
        out_shape=[jax.ShapeDtypeStruct((B, A, FOURIER_WIDTH), BF16)] * 2,
        compiler_params=_cparams(("parallel", "parallel")),
    )(z2, m1, ch)
    nka = min(4, A)
    tn = nka * FOURIER_WIDTH
    out = pl.pallas_call(
        functools.partial(_f2_kernel, nka=nka),
        grid=(A // nka,),
        in_specs=[
            pl.BlockSpec((B, tn), lambda i: (0, i)),
            pl.BlockSpec((B, tn), lambda i: (0, i)),
            pl.BlockSpec((B, 2 * B), lambda i: (0, 0)),
            pl.BlockSpec((FOURIER_HEADS, FOURIER_DIM, FOURIER_DIM), lambda i: (0, 0, 0)),
            pl.BlockSpec((FOURIER_HEADS, 1, FOURIER_DIM), lambda i: (0, 0, 0)),
        ],
        out_specs=pl.BlockSpec((B, tn), lambda i: (0, i)),
        out_shape=jax.ShapeDtypeStruct((B, A * FOURIER_WIDTH), BF16),
        compiler_params=_cparams(("parallel",)),
    )(yre.reshape(B, A * FOURIER_WIDTH), yim.reshape(B, A * FOURIER_WIDTH), m2, fw,
      fb.reshape(FOURIER_HEADS, 1, FOURIER_DIM))
    return out.reshape(T, FOURIER_WIDTH)


def _fctx_kernel(f_ref, cs_ref, ts_ref, w_ref, b_ref, o_ref):
    for h in range(FOURIER_HEADS):
        sl = slice(h * FOURIER_DIM, (h + 1) * FOURIER_DIM)
        g = _dot(f_ref[:, sl].astype(BF16), cs_ref[...])
        gg = jnp.concatenate([g[:, :FOURIER_DIM], g[:, FOURIER_DIM:]], axis=0).astype(BF16)
        spec = _dot(ts_ref[...], gg)
        out = _dot(spec.astype(BF16), w_ref[h].astype(BF16)) + b_ref[h]
        o_ref[:, sl] = out.astype(o_ref.dtype)


def fourier_mix_ctx(zc, fw, fb):
    T = zc.shape[0]
    cc, ss = _channel_dft()
    cs = jnp.asarray(np.concatenate([cc, ss], axis=1), BF16)
    t = np.arange(T)
    ang = 2.0 * np.pi * ((t[:, None] * t[None, :]) % T) / T
    st = 1.0 / math.sqrt(T)
    ts = jnp.asarray(np.concatenate([np.cos(ang) * st, -np.sin(ang) * st], axis=1), BF16)
    return pl.pallas_call(
        _fctx_kernel,
        grid=(1,),
        in_specs=[
            pl.BlockSpec((T, FOURIER_WIDTH), lambda i: (0, COL_F // FOURIER_WIDTH)),
            pl.BlockSpec((FOURIER_DIM, 2 * FOURIER_DIM), lambda i: (0, 0)),
            pl.BlockSpec((T, 2 * T), lambda i: (0, 0)),
            pl.BlockSpec((FOURIER_HEADS, FOURIER_DIM, FOURIER_DIM), lambda i: (0, 0, 0)),
            pl.BlockSpec((FOURIER_HEADS, 1, FOURIER_DIM), lambda i: (0, 0, 0)),
        ],
        out_specs=pl.BlockSpec((T, FOURIER_WIDTH), lambda i: (0, 0)),
        out_shape=jax.ShapeDtypeStruct((T, FOURIER_WIDTH), BF16),
        compiler_params=_cparams(("arbitrary",)),
    )(zc, cs, ts, fw, fb.reshape(FOURIER_HEADS, 1, FOURIER_DIM))


def _rope_tables(T):
    nf = NA_HEAD_DIM // 4
    t = jnp.arange(T)
    inv = 1.0 / (ROPE_THETA ** (jnp.arange(nf, dtype=F32) / nf))
    lane = jnp.arange(NA_HEAD_DIM)
    pos = jnp.where(lane[None, :] < NA_HEAD_DIM // 2, (t // GRID_W)[:, None], (t % GRID_W)[:, None]).astype(F32)
    ang = pos * inv[lane % nf][None, :]
    sign = jnp.where((lane % (2 * nf)) < nf, -1.0, 1.0)[None, :]
    return jnp.cos(ang), jnp.sin(ang) * sign


def _head_norm_rope(x, w, cos, sin):
    ms = jnp.mean(x * x, axis=-1, keepdims=True)
    y = x * lax.rsqrt(ms + NORM_EPS) * w
    if cos is None:
        return y
    lane = lax.broadcasted_iota(jnp.int32, y.shape, 1)
    swap = jnp.where((lane % 64) < 32, pltpu.roll(y, 96, axis=1), pltpu.roll(y, 32, axis=1))
    return y * cos + swap * sin


def _qkprep_kernel(q_ref, k_ref, qw_ref, kw_ref, cos_ref, sin_ref, qo_ref, ko_ref, *, rope):
    cos = cos_ref[...] if rope else None
    sin = sin_ref[...] if rope else None
    qw = qw_ref[...] * (NA_HEAD_DIM ** -0.5)
    kw = kw_ref[...]
    for h in range(4):
        sl = slice(h * LANE, (h + 1) * LANE)
        qo_ref[:, sl] = _head_norm_rope(q_ref[:, sl], qw, cos, sin).astype(qo_ref.dtype)
        ko_ref[:, sl] = _head_norm_rope(k_ref[:, sl], kw, cos, sin).astype(ko_ref.dtype)


def qk_prep(z, qw, kw, rope):
    T = z.shape[0]
    tm = min(512, T)
    if rope:
        cos, sin = _rope_tables(T)
    else:
        cos = sin = jnp.zeros((T, LANE), F32)
    nq = NA_WIDTH // 512
    return pl.pallas_call(
        functools.partial(_qkprep_kernel, rope=rope),
        grid=(T // tm, nq),
        in_specs=[
            pl.BlockSpec((tm, 512), lambda i, j: (i, COL_Q // 512 + j)),
            pl.BlockSpec((tm, 512), lambda i, j: (i, COL_K // 512 + j)),
            pl.BlockSpec((1, LANE), lambda i, j: (0, 0)),
            pl.BlockSpec((1, LANE), lambda i, j: (0, 0)),
            pl.BlockSpec((tm, LANE), lambda i, j: (i, 0)),
            pl.BlockSpec((tm, LANE), lambda i, j: (i, 0)),
        ],
        out_specs=[pl.BlockSpec((tm, 512), lambda i, j: (i, j))] * 2,
        out_shape=[jax.ShapeDtypeStruct((T, NA_WIDTH), BF16)] * 2,
        compiler_params=_cparams(("parallel", "parallel")),
    )(z, z, qw.reshape(1, LANE), kw.reshape(1, LANE), cos, sin)


def _toeplitz_kernel(r_ref, e_ref, o_ref):
    r = r_ref[0]
    acc = jnp.zeros(o_ref.shape[1:], F32)
    for d in range(2 * NA_KW - 1):
        acc = acc + r[:, d:d + 1] * e_ref[d:d + 1, :]
    o_ref[0] = acc


def rpb_bias_tables(rpb):
    L, H = rpb.shape[0], rpb.shape[1]
    ndr, ndc = 2 * NA_KH - 1, 2 * NA_KW - 1
    q = np.arange(GRID_W)
    e = np.zeros((32, GRID_W, GRID_W), np.float32)
    for d in range(ndc):
        e[d] = (q[None, :] - q[:, None] + (NA_KW - 1)) == d
    e = jnp.asarray(e.reshape(32, GRID_W * GRID_W))
    rp = jnp.pad(rpb.reshape(L * H, ndr, ndc), ((0, 0), (0, 16 - ndr), (0, 32 - ndc)))
    toep = pl.pallas_call(
        _toeplitz_kernel,
        grid=(L * H,),
        in_specs=[pl.BlockSpec((1, 16, 32), lambda i: (i, 0, 0)),
                  pl.BlockSpec((32, GRID_W * GRID_W), lambda i: (0, 0))],
        out_specs=pl.BlockSpec((1, 16, GRID_W * GRID_W), lambda i: (i, 0, 0)),
        out_shape=jax.ShapeDtypeStruct((L * H, 16, GRID_W * GRID_W), F32),
        compiler_params=_cparams(("parallel",)),
    )(rp, e)
    toep = toep.reshape(L, H, 16, GRID_W, GRID_W)
    col_start = np.clip(q - NA_KW // 2, 0, GRID_W - NA_KW)
    in_win = (q[None, :] >= col_start[:, None]) & (q[None, :] < col_start[:, None] + NA_KW)
    mask = jnp.asarray(np.where(in_win, 0.0, -1e30).astype(np.float32))
    tabs = []
    for o in range(NA_KH):
        band = toep[:, :, NA_KH - 1 - o:2 * NA_KH - 1 - o]
        band = band + mask[None, None, None]
        tabs.append(jnp.transpose(band, (0, 1, 3, 2, 4)).reshape(L, H, GRID_W, NA_KH * GRID_W))
    return jnp.stack(tabs, axis=2)


def _na_kernel(q_ref, kp_ref, kc_ref, kn_ref, vp_ref, vc_ref, vn_ref, kx_ref, vx_ref, bt_ref, o_ref,
               kbuf, vbuf, *, nrows):
    m = pl.program_id(1)
    blk = NA_KH * GRID_W
    kbuf[0:blk] = kp_ref[...]
    kbuf[blk:2 * blk] = kc_ref[...]
    kbuf[2 * blk:3 * blk] = kn_ref[...]
    vbuf[0:blk] = vp_ref[...].astype(BF16)
    vbuf[blk:2 * blk] = vc_ref[...].astype(BF16)
    vbuf[2 * blk:3 * blk] = vn_ref[...].astype(BF16)
    kctx = kx_ref[...]
    vctx = vx_ref[...].astype(BF16)
    for j in range(NA_KH):
        r = m * NA_KH + j
        rs = jnp.clip(r - NA_KH // 2, 0, nrows - NA_KH)
        start = pl.multiple_of((rs - (m - 1) * NA_KH) * GRID_W, GRID_W)
        q = q_ref[j * GRID_W:(j + 1) * GRID_W, :]
        kb = kbuf[pl.ds(start, blk), :]
        vb = vbuf[pl.ds(start, blk), :]
        s = _dot_nt(q, kb) + bt_ref[0, r - rs]
        sc = _dot_nt(q, kctx)
        mx = jnp.maximum(jnp.max(s, axis=-1, keepdims=True), jnp.max(sc, axis=-1, keepdims=True))
        p = jnp.exp(s - mx)
        pc = jnp.exp(sc - mx)
        den = jnp.sum(p, axis=-1, keepdims=True) + jnp.sum(pc, axis=-1, keepdims=True)
        acc = _dot(p.astype(BF16), vb) + _dot(pc.astype(BF16), vctx)
        o_ref[j * GRID_W:(j + 1) * GRID_W, :] = (acc / den).astype(o_ref.dtype)


def na_attention(qn, kn, z, kcn, zc, bias_tab):
    T = qn.shape[0]
    C = kcn.shape[0]
    nrows = T // GRID_W
    blk = NA_KH * GRID_W
    nblk = T // blk
    vcol = COL_V // LANE
    prev = lambda h, m: (jnp.maximum(m - 1, 0), h)
    cur = lambda h, m: (m, h)
    nxt = lambda h, m: (jnp.minimum(m + 1, nblk - 1), h)
    vprev = lambda h, m: (jnp.maximum(m - 1, 0), vcol + h)
    vcur = lambda h, m: (m, vcol + h)
    vnxt = lambda h, m: (jnp.minimum(m + 1, nblk - 1), vcol + h)
    return pl.pallas_call(
        functools.partial(_na_kernel, nrows=nrows),
        grid=(NA_HEADS, nblk),
        in_specs=[
            pl.BlockSpec((blk, LANE), cur),
            pl.BlockSpec((blk, LANE), prev), pl.BlockSpec((blk, LANE), cur), pl.BlockSpec((blk, LANE), nxt),
            pl.BlockSpec((blk, LANE), vprev), pl.BlockSpec((blk, LANE), vcur), pl.BlockSpec((blk, LANE), vnxt),
            pl.BlockSpec((C, LANE), lambda h, m: (0, h)),
            pl.BlockSpec((C, LANE), lambda h, m: (0, vcol + h)),
            pl.BlockSpec((1, NA_KH, GRID_W, blk), lambda h, m: (h, 0, 0, 0)),
        ],
        out_specs=pl.BlockSpec((blk, LANE), cur),
        out_shape=jax.ShapeDtypeStruct((T, NA_WIDTH), BF16),
        scratch_shapes=[pltpu.VMEM((3 * blk, LANE), BF16), pltpu.VMEM((3 * blk, LANE), BF16)],
        compiler_params=_cparams(("parallel", "parallel")),
    )(qn, kn, kn, kn, z, z, z, kcn, zc, bias_tab)


def _ctxattn_kernel(q_ref, k_ref, v_ref, o_ref):
    s = _dot_nt(q_ref[...], k_ref[...])
    p = jnp.exp(s - jnp.max(s, axis=-1, keepdims=True))
    den = jnp.sum(p, axis=-1, keepdims=True)
    o_ref[...] = (_dot(p.astype(BF16), v_ref[...].astype(BF16)) / den).astype(o_ref.dtype)


def ctx_attention(qcn, kcn, zc):
    C = qcn.shape[0]
    vcol = COL_V // LANE
    return pl.pallas_call(
        _ctxattn_kernel,
        grid=(NA_HEADS,),
        in_specs=[pl.BlockSpec((C, LANE), lambda h: (0, h)), pl.BlockSpec((C, LANE), lambda h: (0, h)),
                  pl.BlockSpec((C, LANE), lambda h: (0, vcol + h))],
        out_specs=pl.BlockSpec((C, LANE), lambda h: (0, h)),
        out_shape=jax.ShapeDtypeStruct((C, NA_WIDTH), BF16),
        compiler_params=_cparams(("parallel",)),
    )(qcn, kcn, zc)


def _seg_sum64(x):
    lane = lax.broadcasted_iota(jnp.int32, x.shape, 1)
    low = lane < RWKV_HEAD_DIM
    s_lo = jnp.sum(jnp.where(low, x, 0.0), axis=-1, keepdims=True)
    s_hi = jnp.sum(jnp.where(low, 0.0, x), axis=-1, keepdims=True)
    return jnp.where(low, s_lo, s_hi)


def _seg_sum(x):
    return jnp.concatenate([_seg_sum64(x[:, i * LANE:(i + 1) * LANE]) for i in range(x.shape[1] // LANE)], axis=1)


def _rwkvprep_kernel(r_ref, rp_ref, rn_ref, k_ref, kp_ref, kn_ref, v_ref, vp_ref, vn_ref, lo_ref, cw_ref,
                     w2_ref, a2_ref, g2_ref, w0_ref, a0_ref, kk_ref, ka_ref, rk_ref,
                     r_o, v_o, kk_o, g_o, bonus_o, logw_o, kd_o, ag_o):
    i = pl.program_id(0)
    n_i = pl.num_programs(0)
    W = RWKV_WIDTH
    cw = cw_ref[...]
    r = _conv3(r_ref[...], *_halo_rows(rp_ref, rn_ref, i, n_i), cw[:, 0:W])
    k = _conv3(k_ref[...], *_halo_rows(kp_ref, kn_ref, i, n_i), cw[:, W:2 * W])
    v = _conv3(v_ref[...], *_halo_rows(vp_ref, vn_ref, i, n_i), cw[:, 2 * W:3 * W])
    lora = lo_ref[...]
    wl = _dot(jnp.tanh(lora[:, 0:2 * W_LORA]).astype(BF16), w2_ref[...])
    al = _dot(lora[:, 2 * W_LORA:2 * (W_LORA + A_LORA)].astype(BF16), a2_ref[...])
    g = _dot(jax.nn.sigmoid(lora[:, 2 * (W_LORA + A_LORA):]).astype(BF16), g2_ref[...])
    kkr = k * kk_ref[...]
    kk = kkr / jnp.maximum(jnp.sqrt(_seg_sum(kkr * kkr)), 1e-12)
    kds = []
    for d in range(2):
        u = -(w0_ref[d:d + 1, :] + wl[:, d * W:(d + 1) * W])
        w_log = -(jnp.maximum(u, 0.0) + jnp.log(1.0 + jnp.exp(-jnp.abs(u)))) - 0.5
        logw = -jnp.exp(w_log)
        a = jax.nn.sigmoid(a0_ref[d:d + 1, :] + al[:, d * W:(d + 1) * W])
        kd = k * (1.0 + (a - 1.0) * ka_ref[...])
        kds.append(kd)
        for gi in range(N_GROUPS):
            sl = slice(gi * GROUP, (gi + 1) * GROUP)
            logw_o[d, gi] = logw[:, sl]
            kd_o[d, gi] = kd[:, sl]
            ag_o[d, gi] = a[:, sl]
    bonus = _seg_sum(r * (kds[0] + kds[1]) * rk_ref[...]) * v
    for gi in range(N_GROUPS):
        sl = slice(gi * GROUP, (gi + 1) * GROUP)
        r_o[gi] = r[:, sl]
        v_o[gi] = v[:, sl]
        kk_o[gi] = kk[:, sl]
        g_o[gi] = g[:, sl]
        bonus_o[gi] = bonus[:, sl]


def rwkv_prep(z, p):
    T = z.shape[0]
    tm = min(128, T)
    W = RWKV_WIDTH
    cb = COL_R // W
    specs = []
    for c in range(3):
        specs += _halo_specs(tm, W, T, lambda j, c=c: cb + c)
    specs = [pl.BlockSpec(s.block_shape, lambda i, f=s.index_map: f(i, 0)) for s in specs]
    full = lambda shape: pl.BlockSpec(shape, lambda i: (0,) * len(shape))
    specs += [
        pl.BlockSpec((tm, LORA_PAD), lambda i: (i, COL_L // LORA_PAD)),
        full((3, 3 * W)), full((2 * W_LORA, 2 * W)), full((2 * A_LORA, 2 * W)), full((LORA_PAD - 256, W)),
        full((2, W)), full((2, W)), full((1, W)), full((1, W)), full((1, W)),
    ]
    g1 = pl.BlockSpec((N_GROUPS, tm, GROUP), lambda i: (0, i, 0))
    g2 = pl.BlockSpec((2, N_GROUPS, tm, GROUP), lambda i: (0, 0, i, 0))
    s1 = jax.ShapeDtypeStruct((N_GROUPS, T, GROUP), F32)
    s2 = jax.ShapeDtypeStruct((2, N_GROUPS, T, GROUP), F32)
    return pl.pallas_call(
        _rwkvprep_kernel,
        grid=(T // tm,),
        in_specs=specs,
        out_specs=[g1] * 5 + [g2] * 3,
        out_shape=[s1] * 5 + [s2] * 3,
        compiler_params=_cparams(("parallel",)),
    )(z, z, z, z, z, z, z, z, z, z, p['rwkv_conv'], p['w2bd'], p['a2bd'], p['g2p'], p['w0'], p['a0'],
      p['k_k'].reshape(1, W), p['k_a'].reshape(1, W), p['r_k'].reshape(1, W))


def _fold(x):
    c = CHUNK
    return x[0:c] + x[c:2 * c] + x[2 * c:3 * c] + x[3 * c:4 * c]


def _rwkvchunk_kernel(r_ref, v_ref, kk_ref, logw_ref, kd_ref, ag_ref, tri_ref, ms_ref, mi_ref, bd_ref, eye_ref,
                      mf_ref, nf_ref, rp_ref, y0_ref):
    tri = tri_ref[0]
    ones = jnp.ones((CHUNK, CHUNK), BF16)
    m_strict = ms_ref[0]
    m_incl = mi_ref[0]
    bd = bd_ref[...]
    eye = eye_ref[...]
    row = lax.broadcasted_iota(jnp.int32, (CHUNK, GROUP), 0)
    col = lax.broadcasted_iota(jnp.int32, (CHUNK, GROUP), 1)
    diag = row == (col % RWKV_HEAD_DIM)

    def expand(x):
        return jnp.concatenate([x.astype(BF16)] * 4, axis=0) * bd

    def body(g, carry):
        lw = logw_ref[0, g]
        hi, lo = _split(lw)
        hl = jnp.concatenate([hi, lo], axis=1)
        cs = _dot(tri, hl)
        linc = cs[:, :GROUP] + cs[:, GROUP:]
        ts = _dot(ones, hl)
        ltot = ts[:, :GROUP] + ts[:, GROUP:]
        e_inc = jnp.exp(linc)
        e_neg = jnp.exp(-linc)
        e_exc = jnp.exp(linc - lw)
        e_rem = jnp.exp(ltot - linc)
        e_tot = jnp.exp(ltot)
        kk = kk_ref[g]
        kd = kd_ref[0, g]
        r = r_ref[g]
        b = kk * ag_ref[0, g]
        rt = r * e_inc
        ax = expand(-kk * e_exc)
        rx = expand(rt)
        bx = expand(b * e_neg)
        kx = expand(kd * e_neg)
        bhx = expand(b * e_rem)
        khx = expand(kd * e_rem)
        vx = expand(v_ref[g])
        gram = _dot_nt(jnp.concatenate([ax, rx], axis=0), jnp.concatenate([bx, kx], axis=0))
        a_ab = gram[:GROUP, :GROUP] * m_strict
        a_ak = gram[:GROUP, GROUP:] * m_strict
        a_rb = gram[GROUP:, :GROUP] * m_incl
        a_rk = gram[GROUP:, GROUP:] * m_incl
        pw = a_ab
        tinv = eye + a_ab
        for _ in range(5):
            pb = pw.astype(BF16)
            pw = _dot(pb, pb)
            tinv = tinv + _dot(pw.astype(BF16), tinv.astype(BF16))
        w1 = _dot(a_ak.astype(BF16), vx)
        x = _dot(tinv.astype(BF16), jnp.concatenate([ax, w1.astype(BF16)], axis=1))
        xb = x.astype(BF16)
        q = _dot(a_rb.astype(BF16), xb)
        rp_ref[0, g] = rt + _fold(q[:, :GROUP])
        y0_ref[0, g] = _fold(q[:, GROUP:] + _dot(a_rk.astype(BF16), vx))
        mn = _dot_tn(bhx, xb)
        mf_ref[0, g] = _fold(mn[:, :GROUP]) + jnp.where(diag, e_tot, 0.0)
        nf_ref[0, g] = _fold(mn[:, GROUP:] + _dot_tn(khx, vx))
        return carry

    lax.fori_loop(0, N_GROUPS, body, 0)


def _chunk_masks():
    i = np.arange(CHUNK)
    tri = np.stack([i[None, :] <= i[:, None], i[None, :] >= i[:, None]]).astype(np.float32)
    j = np.arange(GROUP)
    same = (j[:, None] // CHUNK) == (j[None, :] // CHUNK)
    jt, js = j[:, None] % CHUNK, j[None, :] % CHUNK
    strict = np.stack([same & (js < jt), same & (js > jt)]).astype(np.float32)
    incl = np.stack([same & (js <= jt), same & (js >= jt)]).astype(np.float32)
    return (jnp.asarray(tri, BF16), jnp.asarray(strict), jnp.asarray(incl), jnp.asarray(same.astype(np.float32), BF16),
            jnp.asarray(np.eye(GROUP, dtype=np.float32)), jnp.asarray(same.astype(np.float32)))


def rwkv_chunk(r, v, kk, logw, kd, ag):
    T = r.shape[1]
    nc = T // CHUNK
    tri, strict, incl, bd, eye, _ = _chunk_masks()
    b1 = pl.BlockSpec((N_GROUPS, CHUNK, GROUP), lambda d, c: (0, c, 0))
    b2 = pl.BlockSpec((1, N_GROUPS, CHUNK, GROUP), lambda d, c: (d, 0, c, 0))
    full = lambda shape: pl.BlockSpec(shape, lambda d, c: (0,) * len(shape))
    sd = jax.ShapeDtypeStruct((2, N_GROUPS, T, GROUP), F32)
    return pl.pallas_call(
        _rwkvchunk_kernel,
        grid=(2, nc),
        in_specs=[b1, b1, b1, b2, b2, b2,
                  pl.BlockSpec((1, CHUNK, CHUNK), lambda d, c: (d, 0, 0)),
                  pl.BlockSpec((1, GROUP, GROUP), lambda d, c: (d, 0, 0)),
                  pl.BlockSpec((1, GROUP, GROUP), lambda d, c: (d, 0, 0)),
                  full((GROUP, GROUP)), full((GROUP, GROUP))],
        out_specs=[b2] * 4,
        out_shape=[sd] * 4,
        compiler_params=_cparams(("parallel", "parallel")),
    )(r, v, kk, logw, kd, ag, tri, strict, incl, bd, eye)


def _rwkvscan_kernel(mf_ref, nf_ref, rp_ref, y0_ref, z0_ref, bd_ref, y_ref, zf_ref, z_scr):
    c = pl.program_id(1)

    @pl.when(c == 0)
    def _():
        z_scr[...] = z0_ref[0]

    bd = bd_ref[...]

    def body(g, carry):
        z = z_scr[g]
        zh, zl = _split(z)
        rh, rl = _split(rp_ref[0, g])
        y_ref[0, g] = _dot(rh, zh) + _dot(rh, zl) + _dot(rl, zh) + y0_ref[0, g]
        mbd = jnp.concatenate([mf_ref[0, g]] * 4, axis=0) * bd
        nbd = jnp.concatenate([nf_ref[0, g]] * 4, axis=0) * bd
        mh, ml = _split(mbd)
        z_scr[g] = _dot(mh, zh) + _dot(mh, zl) + _dot(ml, zh) + nbd
        return carry

    lax.fori_loop(0, N_GROUPS, body, 0)

    @pl.when(c == pl.num_programs(1) - 1)
    def _():
        zf_ref[0] = z_scr[...]


def rwkv_scan(mf, nf, rp, y0, z0):
    T = mf.shape[2]
    nc = T // CHUNK
    bdf = _chunk_masks()[5]
    order = lambda d, c: (d, 0, c + d * (nc - 1 - 2 * c), 0)
    blk = pl.BlockSpec((1, N_GROUPS, CHUNK, GROUP), order)
    zspec = pl.BlockSpec((1, N_GROUPS, GROUP, GROUP), lambda d, c: (d, 0, 0, 0))
    return pl.pallas_call(
        _rwkvscan_kernel,
        grid=(2, nc),
        in_specs=[blk, blk, blk, blk, zspec, pl.BlockSpec((GROUP, GROUP), lambda d, c: (0, 0))],
        out_specs=[blk, zspec],
        out_shape=[jax.ShapeDtypeStruct((2, N_GROUPS, T, GROUP), F32),
                   jax.ShapeDtypeStruct((2, N_GROUPS, GROUP, GROUP), F32)],
        scratch_shapes=[pltpu.VMEM((N_GROUPS, GROUP, GROUP), F32)],
        compiler_params=_cparams(("arbitrary", "arbitrary")),
    )(mf, nf, rp, y0, z0, bdf)


def _rwkvpost_kernel(y_ref, g_ref, bonus_ref, gw_ref, gb_ref, o_ref):
    for gi in range(N_GROUPS):
        y = y_ref[0, gi] + y_ref[1, gi]
        mu = _seg_sum(y) * (1.0 / RWKV_HEAD_DIM)
        yc = y - mu
        var = _seg_sum(yc * yc) * (1.0 / RWKV_HEAD_DIM)
        sl = slice(gi * GROUP, (gi + 1) * GROUP)
        yn = yc * lax.rsqrt(var + GN_EPS) * gw_ref[:, sl] + gb_ref[:, sl]
        o_ref[:, sl] = ((yn + bonus_ref[gi]) * g_ref[gi]).astype(o_ref.dtype)


def rwkv_post(y, g, bonus, gn_w, gn_b):
    T = y.shape[2]
    tm = min(256, T)
    W = RWKV_WIDTH
    return pl.pallas_call(
        _rwkvpost_kernel,
        grid=(T // tm,),
        in_specs=[pl.BlockSpec((2, N_GROUPS, tm, GROUP), lambda i: (0, 0, i, 0)),
                  pl.BlockSpec((N_GROUPS, tm, GROUP), lambda i: (0, i, 0)),
                  pl.BlockSpec((N_GROUPS, tm, GROUP), lambda i: (0, i, 0)),
                  pl.BlockSpec((1, W), lambda i: (0, 0)), pl.BlockSpec((1, W), lambda i: (0, 0))],
        out_specs=pl.BlockSpec((tm, W), lambda i: (i, 0)),
        out_shape=jax.ShapeDtypeStruct((T, W), BF16),
        compiler_params=_cparams(("parallel",)),
    )(y, g, bonus, gn_w.reshape(1, W), gn_b.reshape(1, W))


def rwkv_mix(z, p, z0):
    r, v, kk, g, bonus, logw, kd, ag = rwkv_prep(z, p)
    mf, nf, rp, y0 = rwkv_chunk(r, v, kk, logw, kd, ag)
    y, zf = rwkv_scan(mf, nf, rp, y0, z0)
    return rwkv_post(y, g, bonus, p['gn_w'], p['gn_b']), zf


def _prep_layer_params(l, w_in, w2, a2, g2, w_out, w_ffn_in, w_ffn_out):
    wi = w_in[l]
    f_w = wi[:, 0:FOURIER_WIDTH]
    na = wi[:, FOURIER_WIDTH:FOURIER_WIDTH + 3 * NA_WIDTH]
    rk = wi[:, FOURIER_WIDTH + 3 * NA_WIDTH:FOURIER_WIDTH + 3 * NA_WIDTH + 3 * RWKV_WIDTH]
    lo = wi[:, FOURIER_WIDTH + 3 * NA_WIDTH + 3 * RWKV_WIDTH:]
    lo = jnp.pad(lo, ((0, 0), (0, LORA_PAD - lo.shape[1])))
    w_in_p = jnp.concatenate([na, rk, f_w, lo], axis=1).astype(BF16)
    W = RWKV_WIDTH
    zw = jnp.zeros((W_LORA, W), F32)
    w2bd = jnp.concatenate([jnp.concatenate([w2[l, 0], zw], axis=1), jnp.concatenate([zw, w2[l, 1]], axis=1)], axis=0)
    a2bd = jnp.concatenate([jnp.concatenate([a2[l, 0], zw], axis=1), jnp.concatenate([zw, a2[l, 1]], axis=1)], axis=0)
    g2p = jnp.pad(g2[l], ((0, LORA_PAD - 256 - G_LORA), (0, 0)))
    return dict(w_in=w_in_p, w2bd=w2bd.astype(BF16), a2bd=a2bd.astype(BF16), g2p=g2p.astype(BF16),
                w_out=w_out[l].astype(BF16), w_ffn_in=w_ffn_in[l].astype(BF16), w_ffn_out=w_ffn_out[l].astype(BF16))


def _layer(x, ctx, mod_x, mod_c, p, bias_tab, ctx_out):
    Dm = D_MODEL
    sh1, sc1, gt1, sh2, sc2, gt2 = [mod_x[i * Dm:(i + 1) * Dm] for i in range(6)]
    csh1, csc1, cgt1, csh2, csc2, cgt2 = [mod_c[i * Dm:(i + 1) * Dm] for i in range(6)]
    zx = norm_mod_matmul(x, p['norm1_w'], sc1, sh1, p['w_in'])
    zc = norm_mod_matmul(ctx, p['norm1_w'], csc1, csh1, p['w_in'])
    fx = fourier_mix(zx, p['fourier_w'], p['fourier_b'])
    qx, kx = qk_prep(zx, p['q_norm_w'], p['k_norm_w'], rope=True)
    qc, kc = qk_prep(zc, p['q_norm_w'], p['k_norm_w'], rope=False)
    ax = na_attention(qx, kx, zx, kc, zc, bias_tab)
    z0 = jnp.zeros((2, N_GROUPS, GROUP, GROUP), F32)
    rc, zf = rwkv_mix(zc, p, z0)
    rx, _ = rwkv_mix(zx, p, zf)
    x = matmul_residual([fx, ax, rx], p['w_out'], x, gt1)
    u = norm_mod_matmul(x, p['norm2_w'], sc2, sh2, p['w_ffn_in'])
    x = matmul_residual([conv_gate(u, p['ffn_conv'])], p['w_ffn_out'], x, gt2)
    if ctx_out:
        fc = fourier_mix_ctx(zc, p['fourier_w'], p['fourier_b'])
        ac = ctx_attention(qc, kc, zc)
        ctx = matmul_residual([fc, ac, rc], p['w_out'], ctx, cgt1)
        uc = norm_mod_matmul(ctx, p['norm2_w'], csc2, csh2, p['w_ffn_in'])
        ctx = matmul_residual([conv_gate(uc, p['ffn_conv'])], p['w_ffn_out'], ctx, cgt2)
    return x, ctx


def kernel(x, c, ctx, c_ctx, ada_w, ada_b, norm1_w, norm2_w, w_in, fourier_w, fourier_b, q_norm_w, k_norm_w, rpb,
           rwkv_conv, w0, w2, a0, a2, g2, k_k, k_a, r_k, gn_w, gn_b, w_out, ffn_conv, w_ffn_in, w_ffn_out):
    L = ada_w.shape[0]
    xs = x[0]
    cs = ctx[0]
    mods = ada_mod(jnp.concatenate([c, c_ctx[None, :]], axis=0), ada_w, ada_b)
    bias_tabs = rpb_bias_tables(rpb)
    for l in range(L):
        p = _prep_layer_params(l, w_in, w2, a2, g2, w_out, w_ffn_in, w_ffn_out)
        p.update(norm1_w=norm1_w[l], norm2_w=norm2_w[l], fourier_w=fourier_w[l], fourier_b=fourier_b[l],
                 q_norm_w=q_norm_w[l], k_norm_w=k_norm_w[l], rwkv_conv=rwkv_conv[l], w0=w0[l], a0=a0[l],
                 k_k=k_k[l], k_a=k_a[l], r_k=r_k[l], gn_w=gn_w[l], gn_b=gn_b[l], ffn_conv=ffn_conv[l])
        xs, cs = _layer(xs, cs, mods[l, 0], mods[l, 1], p, bias_tabs[l], l < L - 1)
    return xs[None]
```

```python
import functools
import math

import jax
import jax.numpy as jnp
import numpy as np
from jax import lax
from jax.experimental import pallas as pl
from jax.experimental.pallas import tpu as pltpu

F32 = jnp.float32
BF16 = jnp.bfloat16

D_MODEL = 4096
GRID_W = 64
FOURIER_WIDTH = 1024
FOURIER_HEADS = 4
FOURIER_DIM = 256
NA_WIDTH = 1536
NA_HEAD_DIM = 128
NA_HEADS = 12
NA_KH = 8
NA_KW = 16
RWKV_WIDTH = 1536
RWKV_HEAD_DIM = 64
W_LORA = 64
A_LORA = 64
G_LORA = 224
D_FF = 5120
ROPE_THETA = 10000.0
NORM_EPS = 1e-6
GN_EPS = 64e-5

COL_Q = 0
COL_K = NA_WIDTH
COL_V = 2 * NA_WIDTH
COL_R = 3 * NA_WIDTH
COL_F = 6 * NA_WIDTH
COL_L = COL_F + FOURIER_WIDTH
LORA_PAD = 512
IN_COLS_PAD = COL_L + LORA_PAD

CHUNK = 64
GROUP = 256
N_GROUPS = RWKV_WIDTH // GROUP
CHUNK_GROUPS_PER_ITER = 6
FFT_B = 128

LANE = 128
VMEM_LIMIT = 48 * 1024 * 1024


def _cparams(sem):
    return pltpu.CompilerParams(dimension_semantics=sem, vmem_limit_bytes=VMEM_LIMIT)


def _dot(a, b):
    return jnp.dot(a, b, preferred_element_type=F32)


def _dot_nt(a, b):
    return lax.dot_general(a, b, (((1,), (1,)), ((), ())), preferred_element_type=F32)


def _dot_tn(a, b):
    return lax.dot_general(a, b, (((0,), (0,)), ((), ())), preferred_element_type=F32)


def _split(x):
    hi = x.astype(BF16)
    lo = (x - hi.astype(F32)).astype(BF16)
    return hi, lo


def _dot3(a, b):
    ah, al = _split(a)
    bh, bl = _split(b)
    return _dot(ah, bh) + _dot(ah, bl) + _dot(al, bh)


def _ada_kernel(s_ref, w_ref, b_ref, o_ref, acc_ref):
    k = pl.program_id(2)
    tk, tn = w_ref.shape[1], w_ref.shape[2]
    rep = tn // LANE

    @pl.when(k == 0)
    def _():
        acc_ref[...] = jnp.zeros_like(acc_ref)

    def body(i, carry):
        a0, a1 = carry
        r = pl.multiple_of(i * 8, 8)
        w = w_ref[0, pl.ds(r, 8), :]
        s0 = s_ref[0, pl.ds(r, 8), :]
        s1 = s_ref[1, pl.ds(r, 8), :]
        s0 = s0 * jax.nn.sigmoid(s0)
        s1 = s1 * jax.nn.sigmoid(s1)
        a0 = a0 + w * jnp.concatenate([s0] * rep, axis=1)
        a1 = a1 + w * jnp.concatenate([s1] * rep, axis=1)
        return a0, a1

    a0, a1 = lax.fori_loop(0, tk // 8, body, (acc_ref[0], acc_ref[1]), unroll=4)
    acc_ref[0] = a0
    acc_ref[1] = a1

    @pl.when(k == pl.num_programs(2) - 1)
    def _():
        o_ref[0, 0:1, :] = jnp.sum(a0, axis=0, keepdims=True) + b_ref[0]
        o_ref[0, 1:2, :] = jnp.sum(a1, axis=0, keepdims=True) + b_ref[0]


def ada_mod(cc, ada_w, ada_b):
    L, K, N = ada_w.shape
    tk, tn = 2048, 1024
    s_b = jnp.broadcast_to(cc[:, :, None], (2, K, LANE))
    return pl.pallas_call(
        _ada_kernel,
        grid=(L, N // tn, K // tk),
        in_specs=[
            pl.BlockSpec((2, tk, LANE), lambda l, j, k: (0, k, 0)),
            pl.BlockSpec((1, tk, tn), lambda l, j, k: (l, k, j)),
            pl.BlockSpec((1, 1, tn), lambda l, j, k: (l, 0, j)),
        ],
        out_specs=pl.BlockSpec((1, 2, tn), lambda l, j, k: (l, 0, j)),
        out_shape=jax.ShapeDtypeStruct((L, 2, N), F32),
        scratch_shapes=[pltpu.VMEM((2, 8, tn), F32)],
        compiler_params=_cparams(("parallel", "parallel", "arbitrary")),
    )(s_b, ada_w, ada_b.reshape(L, 1, N))


def _nmm_kernel(x_ref, nw_ref, sc_ref, sh_ref, w_ref, o_ref, h_ref):
    @pl.when(pl.program_id(1) == 0)
    def _():
        x = x_ref[...]
        ms = jnp.mean(x * x, axis=-1, keepdims=True)
        y = x * lax.rsqrt(ms + NORM_EPS) * nw_ref[...]
        h_ref[...] = (y * (1.0 + sc_ref[...]) + sh_ref[...]).astype(BF16)

    o_ref[...] = _dot(h_ref[...], w_ref[0]).astype(o_ref.dtype)


def _cast_kernel(x_ref, o_ref):
    o_ref[...] = x_ref[...].astype(o_ref.dtype)


def cast_bf16(w):
    L, K, N = w.shape
    tk, tn = 512, 2048
    spec = pl.BlockSpec((1, tk, tn), lambda l, i, j: (l, i, j))
    return pl.pallas_call(
        _cast_kernel,
        grid=(L, K // tk, N // tn),
        in_specs=[spec],
        out_specs=spec,
        out_shape=jax.ShapeDtypeStruct((L, K, N), BF16),
        compiler_params=_cparams(("parallel", "parallel", "parallel")),
    )(w)


LORA_COLS = 2 * W_LORA + 2 * A_LORA + G_LORA
W_IN_BLOCKS = IN_COLS_PAD // 512


def _cast_win_kernel(x_ref, o_ref):
    col = lax.broadcasted_iota(jnp.int32, x_ref.shape[1:], 1)
    keep = (pl.program_id(2) < W_IN_BLOCKS - 1) | (col < LORA_COLS)
    o_ref[0] = jnp.where(keep, x_ref[0], 0.0).astype(BF16)


def cast_w_in(w_in):
    L, K, _ = w_in.shape
    tk = 1024
    n_front = (FOURIER_WIDTH) // 512
    n_mid = W_IN_BLOCKS - 1 - n_front

    def src(j):
        return jnp.where(j < n_mid, j + n_front, jnp.where(j < W_IN_BLOCKS - 1, j - n_mid, W_IN_BLOCKS - 1))

    return pl.pallas_call(
        _cast_win_kernel,
        grid=(L, K // tk, W_IN_BLOCKS),
        in_specs=[pl.BlockSpec((1, tk, 512), lambda l, i, j: (l, i, src(j)))],
        out_specs=pl.BlockSpec((1, tk, 512), lambda l, i, j: (l, i, j)),
        out_shape=jax.ShapeDtypeStruct((L, K, IN_COLS_PAD), BF16),
        compiler_params=_cparams(("parallel", "parallel", "parallel")),
    )(w_in)


def norm_mod_matmul(x, nw, sc, sh, w, l, out_dtype=F32):
    M, K = x.shape
    N = w.shape[2]
    tm = min(512, M)
    tn = 512
    return pl.pallas_call(
        _nmm_kernel,
        grid=(M // tm, N // tn),
        in_specs=[
            pl.BlockSpec((tm, K), lambda i, j: (i, 0)),
            pl.BlockSpec((1, K), lambda i, j: (0, 0)),
            pl.BlockSpec((1, K), lambda i, j: (0, 0)),
            pl.BlockSpec((1, K), lambda i, j: (0, 0)),
            pl.BlockSpec((1, K, tn), lambda i, j: (l, 0, j)),
        ],
        out_specs=pl.BlockSpec((tm, tn), lambda i, j: (i, j)),
        out_shape=jax.ShapeDtypeStruct((M, N), out_dtype),
        scratch_shapes=[pltpu.VMEM((tm, K), BF16)],
        compiler_params=_cparams(("parallel", "arbitrary")),
    )(x, nw.reshape(1, K), sc.reshape(1, K), sh.reshape(1, K), w)


def _mmres_kernel(*refs, ksplits):
    n = len(ksplits)
    a_refs = refs[:n]
    w_ref, x_ref, g_ref, o_ref = refs[n:]
    acc = None
    off = 0
    for a_ref, kp in zip(a_refs, ksplits):
        part = _dot(a_ref[...].astype(BF16), w_ref[0, off:off + kp, :])
        acc = part if acc is None else acc + part
        off += kp
    o_ref[...] = x_ref[...] + g_ref[...] * acc


def matmul_residual(parts, w, l, x, gate):
    M, N = x.shape
    K = w.shape[1]
    ksplits = tuple(p.shape[1] for p in parts)
    assert sum(ksplits) == K
    tm = min(512, M)
    tn = 512
    in_specs = [pl.BlockSpec((tm, kp), lambda i, j: (i, 0)) for kp in ksplits]
    in_specs += [
        pl.BlockSpec((1, K, tn), lambda i, j: (l, 0, j)),
        pl.BlockSpec((tm, tn), lambda i, j: (i, j)),
        pl.BlockSpec((1, tn), lambda i, j: (0, j)),
    ]
    return pl.pallas_call(
        functools.partial(_mmres_kernel, ksplits=ksplits),
        grid=(M // tm, N // tn),
        in_specs=in_specs,
        out_specs=pl.BlockSpec((tm, tn), lambda i, j: (i, j)),
        out_shape=jax.ShapeDtypeStruct((M, N), F32),
        compiler_params=_cparams(("parallel", "arbitrary")),
    )(*parts, w, x, gate.reshape(1, N))


def _conv3(main, prev_row, next_row, w):
    tm = main.shape[0]
    row = lax.broadcasted_iota(jnp.int32, main.shape, 0)
    dn = jnp.where(row == 0, prev_row, pltpu.roll(main, 1, axis=0))
    up = jnp.where(row == tm - 1, next_row, pltpu.roll(main, tm - 1, axis=0))
    return dn * w[0:1, :] + main * w[1:2, :] + up * w[2:3, :]


def _halo_rows(prev_ref, next_ref, i, n_i):
    prev = prev_ref[...].astype(F32)
    nxt = next_ref[...].astype(F32)
    hr = prev.shape[0]
    prev_row = jnp.where(i == 0, 0.0, prev[hr - 1:hr, :])
    next_row = jnp.where(i == n_i - 1, 0.0, nxt[0:1, :])
    return prev_row, next_row


def _halo_specs(tm, tn, n_rows, col_fn, hr=8):
    rb = tm // hr
    last = n_rows // hr - 1
    return [
        pl.BlockSpec((tm, tn), lambda i, j: (i, col_fn(j))),
        pl.BlockSpec((hr, tn), lambda i, j: (jnp.maximum(i * rb - 1, 0), col_fn(j))),
        pl.BlockSpec((hr, tn), lambda i, j: (jnp.minimum((i + 1) * rb, last), col_fn(j))),
    ]


def _convgate_kernel(a_ref, ap_ref, an_ref, b_ref, bp_ref, bn_ref, wa_ref, wb_ref, o_ref):
    i = pl.program_id(0)
    n_i = pl.num_programs(0)
    pa, na = _halo_rows(ap_ref, an_ref, i, n_i)
    pb, nb = _halo_rows(bp_ref, bn_ref, i, n_i)
    a = _conv3(a_ref[...].astype(F32), pa, na, wa_ref[...])
    b = _conv3(b_ref[...].astype(F32), pb, nb, wb_ref[...])
    o_ref[...] = (a * jax.nn.sigmoid(a) * b).astype(o_ref.dtype)


def conv_gate(u, conv_w):
    M, N2 = u.shape
    F = N2 // 2
    tm = min(512, M)
    tn = 1024
    nb = F // tn
    specs = _halo_specs(tm, tn, M, lambda j: j, 16) + _halo_specs(tm, tn, M, lambda j: j + nb, 16)
    specs += [pl.BlockSpec((3, tn), lambda i, j: (0, j)), pl.BlockSpec((3, tn), lambda i, j: (0, j + nb))]
    return pl.pallas_call(
        _convgate_kernel,
        grid=(M // tm, nb),
        in_specs=specs,
        out_specs=pl.BlockSpec((tm, tn), lambda i, j: (i, j)),
        out_shape=jax.ShapeDtypeStruct((M, F), BF16),
        compiler_params=_cparams(("parallel", "parallel")),
    )(u, u, u, u, u, u, conv_w, conv_w)


def _dft_tables(T):
    A, B = T // FFT_B, FFT_B
    ka = jnp.arange(A, dtype=jnp.int32)
    a = jnp.arange(A, dtype=jnp.int32)
    b = jnp.arange(B, dtype=jnp.int32)
    n = (ka[None, :, None] * (B * a[None, None, :] + b[:, None, None])) % T
    ang = n.astype(F32) * (2.0 * math.pi / T)
    sa = 1.0 / math.sqrt(A)
    m1 = jnp.concatenate([jnp.cos(ang) * sa, -jnp.sin(ang) * sa], axis=1).astype(BF16)
    kb = jnp.arange(B, dtype=jnp.int32)
    n2 = (kb[:, None] * b[None, :]) % B
    ang2 = n2.astype(F32) * (2.0 * math.pi / B)
    sb = 1.0 / math.sqrt(B)
    m2 = jnp.concatenate([jnp.cos(ang2) * sb, jnp.sin(ang2) * sb], axis=1).astype(BF16)
    return m1, m2


def _channel_dft():
    c = np.arange(FOURIER_DIM)
    ang = 2.0 * np.pi * ((c[:, None] * c[None, :]) % FOURIER_DIM) / FOURIER_DIM
    s = 1.0 / math.sqrt(FOURIER_DIM)
    return np.cos(ang) * s, np.sin(ang) * s


FFT_SUB = 8


def _f1_kernel(x_ref, m_ref, ch_ref, re_ref, im_ref):
    A = x_ref.shape[0]
    ch = ch_ref[...]
    for j in range(FFT_SUB):
        y = _dot(m_ref[j], x_ref[:, j, :].astype(BF16))
        for h in range(2):
            sl = slice(h * FOURIER_DIM, (h + 1) * FOURIER_DIM)
            lhs = jnp.concatenate([y[:A, sl], y[A:, sl]], axis=1).astype(BF16)
            yp = _dot(lhs, ch)
            re_ref[j, :, sl] = yp[:, :FOURIER_DIM]
            im_ref[j, :, sl] = yp[:, FOURIER_DIM:]


def _f2_kernel(re_ref, im_ref, m_ref, w_ref, b_ref, o_ref):
    for j in range(FFT_SUB):
        rhs = jnp.concatenate([re_ref[:, j, :], im_ref[:, j, :]], axis=0).astype(BF16)
        spec = _dot(m_ref[...], rhs)
        outs = []
        for h in range(FOURIER_HEADS):
            sl = slice(h * FOURIER_DIM, (h + 1) * FOURIER_DIM)
            outs.append(_dot(spec[:, sl].astype(BF16), w_ref[h].astype(BF16)) + b_ref[h])
        o_ref[:, j, :] = jnp.concatenate(outs, axis=1)


def fourier_mix(z, fw, fb):
    T, NC = z.shape
    A, B = T // FFT_B, FFT_B
    m1, m2 = _dft_tables(T)
    cc, ss = _channel_dft()
    ch = jnp.asarray(np.block([[cc, -ss], [ss, cc]]), BF16)
    cb0 = COL_F // 512
    yre, yim = pl.pallas_call(
        _f1_kernel,
        grid=(B // FFT_SUB, 2),
        in_specs=[
            pl.BlockSpec((A, FFT_SUB, 512), lambda b, c: (0, b, cb0 + c)),
            pl.BlockSpec((FFT_SUB, 2 * A, A), lambda b, c: (b, 0, 0)),
            pl.BlockSpec((512, 512), lambda b, c: (0, 0)),
        ],
        out_specs=[pl.BlockSpec((FFT_SUB, A, 512), lambda b, c: (b, 0, c))] * 2,
        out_shape=[jax.ShapeDtypeStruct((B, A, FOURIER_WIDTH), F32)] * 2,
        compiler_params=_cparams(("parallel", "parallel")),
    )(z.reshape(A, B, NC), m1, ch)
    blk = pl.BlockSpec((B, FFT_SUB, FOURIER_WIDTH), lambda i: (0, i, 0))
    out = pl.pallas_call(
        _f2_kernel,
        grid=(A // FFT_SUB,),
        in_specs=[
            blk, blk,
            pl.BlockSpec((B, 2 * B), lambda i: (0, 0)),
            pl.BlockSpec((FOURIER_HEADS, FOURIER_DIM, FOURIER_DIM), lambda i: (0, 0, 0)),
            pl.BlockSpec((FOURIER_HEADS, 1, FOURIER_DIM), lambda i: (0, 0, 0)),
        ],
        out_specs=blk,
        out_shape=jax.ShapeDtypeStruct((B, A, FOURIER_WIDTH), F32),
        compiler_params=_cparams(("parallel",)),
    )(yre, yim, m2, fw, fb.reshape(FOURIER_HEADS, 1, FOURIER_DIM))
    return out.reshape(T, FOURIER_WIDTH)


def _fctx_kernel(f_ref, cs_ref, ts_ref, w_ref, b_ref, o_ref):
    for h in range(FOURIER_HEADS):
        sl = slice(h * FOURIER_DIM, (h + 1) * FOURIER_DIM)
        g = _dot(f_ref[:, sl].astype(BF16), cs_ref[...])
        gg = jnp.concatenate([g[:, :FOURIER_DIM], g[:, FOURIER_DIM:]], axis=0).astype(BF16)
        spec = _dot(ts_ref[...], gg)
        out = _dot(spec.astype(BF16), w_ref[h].astype(BF16)) + b_ref[h]
        o_ref[:, sl] = out.astype(o_ref.dtype)


def fourier_mix_ctx(zc, fw, fb):
    T = zc.shape[0]
    cc, ss = _channel_dft()
    cs = jnp.asarray(np.concatenate([cc, ss], axis=1), BF16)
    t = np.arange(T)
    ang = 2.0 * np.pi * ((t[:, None] * t[None, :]) % T) / T
    st = 1.0 / math.sqrt(T)
    ts = jnp.asarray(np.concatenate([np.cos(ang) * st, -np.sin(ang) * st], axis=1), BF16)
    return pl.pallas_call(
        _fctx_kernel,
        grid=(1,),
        in_specs=[
            pl.BlockSpec((T, FOURIER_WIDTH), lambda i: (0, COL_F // FOURIER_WIDTH)),
            pl.BlockSpec((FOURIER_DIM, 2 * FOURIER_DIM), lambda i: (0, 0)),
            pl.BlockSpec((T, 2 * T), lambda i: (0, 0)),
            pl.BlockSpec((FOURIER_HEADS, FOURIER_DIM, FOURIER_DIM), lambda i: (0, 0, 0)),
            pl.BlockSpec((FOURIER_HEADS, 1, FOURIER_DIM), lambda i: (0, 0, 0)),
        ],
        out_specs=pl.BlockSpec((T, FOURIER_WIDTH), lambda i: (0, 0)),
        out_shape=jax.ShapeDtypeStruct((T, FOURIER_WIDTH), BF16),
        compiler_params=_cparams(("arbitrary",)),
    )(zc, cs, ts, fw, fb.reshape(FOURIER_HEADS, 1, FOURIER_DIM))


def _rope_tables(T):
    nf = NA_HEAD_DIM // 4
    t = jnp.arange(T)
    inv = 1.0 / (ROPE_THETA ** (jnp.arange(nf, dtype=F32) / nf))
    lane = jnp.arange(NA_HEAD_DIM)
    pos = jnp.where(lane[None, :] < NA_HEAD_DIM // 2, (t // GRID_W)[:, None], (t % GRID_W)[:, None]).astype(F32)
    ang = pos * inv[lane % nf][None, :]
    sign = jnp.where((lane % (2 * nf)) < nf, -1.0, 1.0)[None, :]
    return jnp.cos(ang), jnp.sin(ang) * sign


def _head_norm_rope(x, w, cos, sin):
    ms = jnp.mean(x * x, axis=-1, keepdims=True)
    y = x * lax.rsqrt(ms + NORM_EPS) * w
    if cos is None:
        return y
    lane = lax.broadcasted_iota(jnp.int32, y.shape, 1)
    swap = jnp.where((lane % 64) < 32, pltpu.roll(y, 96, axis=1), pltpu.roll(y, 32, axis=1))
    return y * cos + swap * sin


def _qkprep_kernel(q_ref, k_ref, qw_ref, kw_ref, cos_ref, sin_ref, qo_ref, ko_ref, *, rope):
    cos = cos_ref[...] if rope else None
    sin = sin_ref[...] if rope else None
    qw = qw_ref[...] * (NA_HEAD_DIM ** -0.5)
    kw = kw_ref[...]
    for h in range(4):
        sl = slice(h * LANE, (h + 1) * LANE)
        qo_ref[:, sl] = _head_norm_rope(q_ref[:, sl], qw, cos, sin).astype(qo_ref.dtype)
        ko_ref[:, sl] = _head_norm_rope(k_ref[:, sl], kw, cos, sin).astype(ko_ref.dtype)


def qk_prep(z, qw, kw, rope):
    T = z.shape[0]
    tm = min(512, T)
    if rope:
        cos, sin = _rope_tables(T)
    else:
        cos = sin = jnp.zeros((T, LANE), F32)
    nq = NA_WIDTH // 512
    return pl.pallas_call(
        functools.partial(_qkprep_kernel, rope=rope),
        grid=(T // tm, nq),
        in_specs=[
            pl.BlockSpec((tm, 512), lambda i, j: (i, COL_Q // 512 + j)),
            pl.BlockSpec((tm, 512), lambda i, j: (i, COL_K // 512 + j)),
            pl.BlockSpec((1, LANE), lambda i, j: (0, 0)),
            pl.BlockSpec((1, LANE), lambda i, j: (0, 0)),
            pl.BlockSpec((tm, LANE), lambda i, j: (i, 0)),
            pl.BlockSpec((tm, LANE), lambda i, j: (i, 0)),
        ],
        out_specs=[pl.BlockSpec((tm, 512), lambda i, j: (i, j))] * 2,
        out_shape=[jax.ShapeDtypeStruct((T, NA_WIDTH), BF16)] * 2,
        compiler_params=_cparams(("parallel", "parallel")),
    )(z, z, qw.reshape(1, LANE), kw.reshape(1, LANE), cos, sin)


def _toeplitz_kernel(r_ref, e_ref, o_ref):
    r = r_ref[0]
    acc = jnp.zeros(o_ref.shape[1:], F32)
    for d in range(2 * NA_KW - 1):
        acc = acc + r[:, d:d + 1] * e_ref[d:d + 1, :]
    o_ref[0] = acc


def rpb_bias_tables(rpb):
    L, H = rpb.shape[0], rpb.shape[1]
    ndr, ndc = 2 * NA_KH - 1, 2 * NA_KW - 1
    q = np.arange(GRID_W)
    e = np.zeros((32, GRID_W, GRID_W), np.float32)
    for d in range(ndc):
        e[d] = (q[None, :] - q[:, None] + (NA_KW - 1)) == d
    e = jnp.asarray(e.reshape(32, GRID_W * GRID_W))
    rp = jnp.pad(rpb.reshape(L * H, ndr, ndc), ((0, 0), (0, 16 - ndr), (0, 32 - ndc)))
    toep = pl.pallas_call(
        _toeplitz_kernel,
        grid=(L * H,),
        in_specs=[pl.BlockSpec((1, 16, 32), lambda i: (i, 0, 0)),
                  pl.BlockSpec((32, GRID_W * GRID_W), lambda i: (0, 0))],
        out_specs=pl.BlockSpec((1, 16, GRID_W * GRID_W), lambda i: (i, 0, 0)),
        out_shape=jax.ShapeDtypeStruct((L * H, 16, GRID_W * GRID_W), F32),
        compiler_params=_cparams(("parallel",)),
    )(rp, e)
    toep = toep.reshape(L, H, 16, GRID_W, GRID_W)
    col_start = np.clip(q - NA_KW // 2, 0, GRID_W - NA_KW)
    in_win = (q[None, :] >= col_start[:, None]) & (q[None, :] < col_start[:, None] + NA_KW)
    mask = jnp.asarray(np.where(in_win, 0.0, -1e30).astype(np.float32))
    tabs = []
    for o in range(NA_KH):
        band = toep[:, :, NA_KH - 1 - o:2 * NA_KH - 1 - o]
        band = band + mask[None, None, None]
        tabs.append(jnp.transpose(band, (0, 1, 3, 2, 4)).reshape(L, H, GRID_W, NA_KH * GRID_W))
    return jnp.stack(tabs, axis=2)


def _na_kernel(q_ref, kp_ref, kc_ref, kn_ref, vp_ref, vc_ref, vn_ref, kx_ref, vx_ref, bt_ref, o_ref,
               kbuf, vbuf, *, nrows):
    m = pl.program_id(1)
    blk = NA_KH * GRID_W
    kbuf[0:blk] = kp_ref[...]
    kbuf[blk:2 * blk] = kc_ref[...]
    kbuf[2 * blk:3 * blk] = kn_ref[...]
    vbuf[0:blk] = vp_ref[...].astype(BF16)
    vbuf[blk:2 * blk] = vc_ref[...].astype(BF16)
    vbuf[2 * blk:3 * blk] = vn_ref[...].astype(BF16)
    kctx = kx_ref[...]
    vctx = vx_ref[...].astype(BF16)
    starts, scores = [], []
    for j in range(NA_KH):
        r = m * NA_KH + j
        rs = jnp.clip(r - NA_KH // 2, 0, nrows - NA_KH)
        start = pl.multiple_of((rs - (m - 1) * NA_KH) * GRID_W, GRID_W)
        q = q_ref[j * GRID_W:(j + 1) * GRID_W, :]
        s = _dot_nt(q, kbuf[pl.ds(start, blk), :]) + bt_ref[0, r - rs]
        starts.append(start)
        scores.append((s, _dot_nt(q, kctx)))
    probs = []
    for s, sc in scores:
        mx = jnp.maximum(jnp.max(s, axis=-1, keepdims=True), jnp.max(sc, axis=-1, keepdims=True))
        p = jnp.exp(s - mx)
        pc = jnp.exp(sc - mx)
        den = jnp.sum(p, axis=-1, keepdims=True) + jnp.sum(pc, axis=-1, keepdims=True)
        probs.append((p.astype(BF16), pc.astype(BF16), den))
    for j, (p, pc, den) in enumerate(probs):
        acc = _dot(p, vbuf[pl.ds(starts[j], blk), :]) + _dot(pc, vctx)
        o_ref[j * GRID_W:(j + 1) * GRID_W, :] = (acc / den).astype(o_ref.dtype)


def na_attention(qn, kn, z, kcn, zc, bias_tab):
    T = qn.shape[0]
    C = kcn.shape[0]
    nrows = T // GRID_W
    blk = NA_KH * GRID_W
    nblk = T // blk
    vcol = COL_V // LANE
    prev = lambda h, m: (jnp.maximum(m - 1, 0), h)
    cur = lambda h, m: (m, h)
    nxt = lambda h, m: (jnp.minimum(m + 1, nblk - 1), h)
    vprev = lambda h, m: (jnp.maximum(m - 1, 0), vcol + h)
    vcur = lambda h, m: (m, vcol + h)
    vnxt = lambda h, m: (jnp.minimum(m + 1, nblk - 1), vcol + h)
    return pl.pallas_call(
        functools.partial(_na_kernel, nrows=nrows),
        grid=(NA_HEADS, nblk),
        in_specs=[
            pl.BlockSpec((blk, LANE), cur),
            pl.BlockSpec((blk, LANE), prev), pl.BlockSpec((blk, LANE), cur), pl.BlockSpec((blk, LANE), nxt),
            pl.BlockSpec((blk, LANE), vprev), pl.BlockSpec((blk, LANE), vcur), pl.BlockSpec((blk, LANE), vnxt),
            pl.BlockSpec((C, LANE), lambda h, m: (0, h)),
            pl.BlockSpec((C, LANE), lambda h, m: (0, vcol + h)),
            pl.BlockSpec((1, NA_KH, GRID_W, blk), lambda h, m: (h, 0, 0, 0)),
        ],
        out_specs=pl.BlockSpec((blk, LANE), cur),
        out_shape=jax.ShapeDtypeStruct((T, NA_WIDTH), BF16),
        scratch_shapes=[pltpu.VMEM((3 * blk, LANE), BF16), pltpu.VMEM((3 * blk, LANE), BF16)],
        compiler_params=_cparams(("parallel", "parallel")),
    )(qn, kn, kn, kn, z, z, z, kcn, zc, bias_tab)


def _ctxattn_kernel(q_ref, k_ref, v_ref, o_ref):
    s = _dot_nt(q_ref[...], k_ref[...])
    p = jnp.exp(s - jnp.max(s, axis=-1, keepdims=True))
    den = jnp.sum(p, axis=-1, keepdims=True)
    o_ref[...] = (_dot(p.astype(BF16), v_ref[...].astype(BF16)) / den).astype(o_ref.dtype)


def ctx_attention(qcn, kcn, zc):
    C = qcn.shape[0]
    vcol = COL_V // LANE
    return pl.pallas_call(
        _ctxattn_kernel,
        grid=(NA_HEADS,),
        in_specs=[pl.BlockSpec((C, LANE), lambda h: (0, h)), pl.BlockSpec((C, LANE), lambda h: (0, h)),
                  pl.BlockSpec((C, LANE), lambda h: (0, vcol + h))],
        out_specs=pl.BlockSpec((C, LANE), lambda h: (0, h)),
        out_shape=jax.ShapeDtypeStruct((C, NA_WIDTH), BF16),
        compiler_params=_cparams(("parallel",)),
    )(qcn, kcn, zc)


def _seg_sum64(x):
    lane = lax.broadcasted_iota(jnp.int32, x.shape, 1)
    low = lane < RWKV_HEAD_DIM
    s_lo = jnp.sum(jnp.where(low, x, 0.0), axis=-1, keepdims=True)
    s_hi = jnp.sum(jnp.where(low, 0.0, x), axis=-1, keepdims=True)
    return jnp.where(low, s_lo, s_hi)


def _seg_sum(x):
    return jnp.concatenate([_seg_sum64(x[:, i * LANE:(i + 1) * LANE]) for i in range(x.shape[1] // LANE)], axis=1)


def _rwkvprep_kernel(r_ref, rp_ref, rn_ref, k_ref, kp_ref, kn_ref, v_ref, vp_ref, vn_ref, lo_ref, cw_ref,
                     w2_ref, a2_ref, g2_ref, w0_ref, a0_ref, kk_ref, ka_ref, rk_ref,
                     r_o, v_o, kk_o, g_o, bonus_o, logw_o, kd_o, ag_o):
    i = pl.program_id(0)
    n_i = pl.num_programs(0)
    W = RWKV_WIDTH
    cw = cw_ref[...]
    r = _conv3(r_ref[...], *_halo_rows(rp_ref, rn_ref, i, n_i), cw[:, 0:W])
    k = _conv3(k_ref[...], *_halo_rows(kp_ref, kn_ref, i, n_i), cw[:, W:2 * W])
    v = _conv3(v_ref[...], *_halo_rows(vp_ref, vn_ref, i, n_i), cw[:, 2 * W:3 * W])
    lora = lo_ref[...]
    wl = _dot(jnp.tanh(lora[:, 0:2 * W_LORA]).astype(BF16), w2_ref[...])
    al = _dot(lora[:, 2 * W_LORA:2 * (W_LORA + A_LORA)].astype(BF16), a2_ref[...])
    g = _dot(jax.nn.sigmoid(lora[:, 2 * (W_LORA + A_LORA):]).astype(BF16), g2_ref[...])
    kkr = k * kk_ref[...]
    kk = kkr / jnp.maximum(jnp.sqrt(_seg_sum(kkr * kkr)), 1e-12)
    kds = []
    for d in range(2):
        u = -(w0_ref[d:d + 1, :] + wl[:, d * W:(d + 1) * W])
        w_log = -(jnp.maximum(u, 0.0) + jnp.log(1.0 + jnp.exp(-jnp.abs(u)))) - 0.5
        logw = -jnp.exp(w_log)
        a = jax.nn.sigmoid(a0_ref[d:d + 1, :] + al[:, d * W:(d + 1) * W])
        kd = k * (1.0 + (a - 1.0) * ka_ref[...])
        kds.append(kd)
        for gi in range(N_GROUPS):
            sl = slice(gi * GROUP, (gi + 1) * GROUP)
            logw_o[d, gi] = logw[:, sl]
            kd_o[d, gi] = kd[:, sl]
            ag_o[d, gi] = a[:, sl]
    bonus = _seg_sum(r * (kds[0] + kds[1]) * rk_ref[...]) * v
    for gi in range(N_GROUPS):
        sl = slice(gi * GROUP, (gi + 1) * GROUP)
        r_o[gi] = r[:, sl]
        v_o[gi] = v[:, sl]
        kk_o[gi] = kk[:, sl]
        g_o[gi] = g[:, sl]
        bonus_o[gi] = bonus[:, sl]


def rwkv_prep(z, p):
    T = z.shape[0]
    tm = min(128, T)
    W = RWKV_WIDTH
    cb = COL_R // W
    specs = []
    for c in range(3):
        specs += _halo_specs(tm, W, T, lambda j, c=c: cb + c)
    specs = [pl.BlockSpec(s.block_shape, lambda i, f=s.index_map: f(i, 0)) for s in specs]
    full = lambda shape: pl.BlockSpec(shape, lambda i: (0,) * len(shape))
    specs += [
        pl.BlockSpec((tm, LORA_PAD), lambda i: (i, COL_L // LORA_PAD)),
        full((3, 3 * W)), full((2 * W_LORA, 2 * W)), full((2 * A_LORA, 2 * W)), full((LORA_PAD - 256, W)),
        full((2, W)), full((2, W)), full((1, W)), full((1, W)), full((1, W)),
    ]
    g1 = pl.BlockSpec((N_GROUPS, tm, GROUP), lambda i: (0, i, 0))
    g2 = pl.BlockSpec((2, N_GROUPS, tm, GROUP), lambda i: (0, 0, i, 0))
    s1 = jax.ShapeDtypeStruct((N_GROUPS, T, GROUP), F32)
    s2 = jax.ShapeDtypeStruct((2, N_GROUPS, T, GROUP), F32)
    return pl.pallas_call(
        _rwkvprep_kernel,
        grid=(T // tm,),
        in_specs=specs,
        out_specs=[g1] * 5 + [g2] * 3,
        out_shape=[s1] * 5 + [s2] * 3,
        compiler_params=_cparams(("parallel",)),
    )(z, z, z, z, z, z, z, z, z, z, p['rwkv_conv'], p['w2bd'], p['a2bd'], p['g2p'], p['w0'], p['a0'],
      p['k_k'].reshape(1, W), p['k_a'].reshape(1, W), p['r_k'].reshape(1, W))


def _fold(x):
    c = CHUNK
    return x[0:c] + x[c:2 * c] + x[2 * c:3 * c] + x[3 * c:4 * c]


def _rwkvchunk_kernel(r_ref, v_ref, kk_ref, logw_ref, kd_ref, ag_ref, tri_ref, ms_ref, mi_ref, bd_ref, eye_ref,
                      mf_ref, nf_ref, rp_ref, y0_ref):
    forward = pl.program_id(0) == 0
    tri = tri_ref[0]
    m_strict = ms_ref[0]
    m_incl = mi_ref[0]
    bd = bd_ref[...]
    eye = eye_ref[...]
    row = lax.broadcasted_iota(jnp.int32, (CHUNK, GROUP), 0)
    col = lax.broadcasted_iota(jnp.int32, (CHUNK, GROUP), 1)
    diag = row == (col % RWKV_HEAD_DIM)

    def expand(x):
        return jnp.concatenate([x.astype(BF16)] * 4, axis=0) * bd

    def one_group(g):
        lw = logw_ref[0, g]
        hi, lo = _split(lw)
        cs = _dot(tri, jnp.concatenate([hi, lo], axis=1))
        linc = cs[:, :GROUP] + cs[:, GROUP:]
        ltot = jnp.where(forward, linc[CHUNK - 1:CHUNK, :], linc[0:1, :])
        e_inc = jnp.exp(linc)
        e_neg = jnp.exp(-linc)
        e_exc = jnp.exp(linc - lw)
        e_rem = jnp.exp(ltot - linc)
        e_tot = jnp.exp(ltot)
        kk = kk_ref[g]
        kd = kd_ref[0, g]
        r = r_ref[g]
        b = kk * ag_ref[0, g]
        rt = r * e_inc
        ax = expand(-kk * e_exc)
        rx = expand(rt)
        bx = expand(b * e_neg)
        kx = expand(kd * e_neg)
        bhx = expand(b * e_rem)
        khx = expand(kd * e_rem)
        vx = expand(v_ref[g])
        gram = _dot_nt(jnp.concatenate([ax, rx], axis=0), jnp.concatenate([bx, kx], axis=0))
        a_ab = gram[:GROUP, :GROUP] * m_strict
        a_ak = gram[:GROUP, GROUP:] * m_strict
        a_rb = gram[GROUP:, :GROUP] * m_incl
        a_rk = gram[GROUP:, GROUP:] * m_incl
        return dict(g=g, rt=rt, e_tot=e_tot, ax=ax, bhx=bhx, khx=khx, vx=vx, pw=a_ab, tinv=eye + a_ab,
                    a_ak=a_ak.astype(BF16), a_rb=a_rb.astype(BF16), a_rk=a_rk.astype(BF16))

    def body(i, carry):
        st = [one_group(i * CHUNK_GROUPS_PER_ITER + u) for u in range(CHUNK_GROUPS_PER_ITER)]
        for _ in range(5):
            for s in st:
                pb = s['pw'].astype(BF16)
                s['pw'] = _dot(pb, pb)
            for s in st:
                s['tinv'] = s['tinv'] + _dot(s['pw'].astype(BF16), s['tinv'].astype(BF16))
        for s in st:
            s['w1'] = _dot(s['a_ak'], s['vx'])
        for s in st:
            x = _dot(s['tinv'].astype(BF16), jnp.concatenate([s['ax'], s['w1'].astype(BF16)], axis=1))
            s['xb'] = x.astype(BF16)
        for s in st:
            q = _dot(s['a_rb'], s['xb'])
            rp_ref[0, s['g']] = s['rt'] + _fold(q[:, :GROUP])
            y0_ref[0, s['g']] = _fold(q[:, GROUP:] + _dot(s['a_rk'], s['vx']))
        for s in st:
            mn = _dot_tn(s['bhx'], s['xb'])
            mf_ref[0, s['g']] = _fold(mn[:, :GROUP]) + jnp.where(diag, s['e_tot'], 0.0)
            nf_ref[0, s['g']] = _fold(mn[:, GROUP:] + _dot_tn(s['khx'], s['vx']))
        return carry

    lax.fori_loop(0, N_GROUPS // CHUNK_GROUPS_PER_ITER, body, 0)


def _chunk_masks():
    i = np.arange(CHUNK)
    tri = np.stack([i[None, :] <= i[:, None], i[None, :] >= i[:, None]]).astype(np.float32)
    j = np.arange(GROUP)
    same = (j[:, None] // CHUNK) == (j[None, :] // CHUNK)
    jt, js = j[:, None] % CHUNK, j[None, :] % CHUNK
    strict = np.stack([same & (js < jt), same & (js > jt)]).astype(np.float32)
    incl = np.stack([same & (js <= jt), same & (js >= jt)]).astype(np.float32)
    return (jnp.asarray(tri, BF16), jnp.asarray(strict), jnp.asarray(incl), jnp.asarray(same.astype(np.float32), BF16),
            jnp.asarray(np.eye(GROUP, dtype=np.float32)), jnp.asarray(same.astype(np.float32)))


def rwkv_chunk(r, v, kk, logw, kd, ag):
    T = r.shape[1]
    nc = T // CHUNK
    tri, strict, incl, bd, eye, _ = _chunk_masks()
    b1 = pl.BlockSpec((N_GROUPS, CHUNK, GROUP), lambda d, c: (0, c, 0))
    b2 = pl.BlockSpec((1, N_GROUPS, CHUNK, GROUP), lambda d, c: (d, 0, c, 0))
    full = lambda shape: pl.BlockSpec(shape, lambda d, c: (0,) * len(shape))
    sd = jax.ShapeDtypeStruct((2, N_GROUPS, T, GROUP), F32)
    return pl.pallas_call(
        _rwkvchunk_kernel,
        grid=(2, nc),
        in_specs=[b1, b1, b1, b2, b2, b2,
                  pl.BlockSpec((1, CHUNK, CHUNK), lambda d, c: (d, 0, 0)),
                  pl.BlockSpec((1, GROUP, GROUP), lambda d, c: (d, 0, 0)),
                  pl.BlockSpec((1, GROUP, GROUP), lambda d, c: (d, 0, 0)),
                  full((GROUP, GROUP)), full((GROUP, GROUP))],
        out_specs=[b2] * 4,
        out_shape=[sd] * 4,
        compiler_params=_cparams(("parallel", "parallel")),
    )(r, v, kk, logw, kd, ag, tri, strict, incl, bd, eye)


def _rwkvscan_kernel(mf_ref, nf_ref, rp_ref, y0_ref, z0_ref, bd_ref, y_ref, zf_ref, z_scr):
    c = pl.program_id(1)

    @pl.when(c == 0)
    def _():
        z_scr[...] = z0_ref[0]

    bd = bd_ref[...]

    for g in range(N_GROUPS):
        z = z_scr[g]
        zh, zl = _split(z)
        y_ref[0, g] = _dot(rp_ref[0, g].astype(BF16), zh) + y0_ref[0, g]
        mbd = jnp.concatenate([mf_ref[0, g]] * 4, axis=0) * bd
        nbd = jnp.concatenate([nf_ref[0, g]] * 4, axis=0) * bd
        mh, ml = _split(mbd)
        z_scr[g] = _dot(mh, zh) + _dot(mh, zl) + _dot(ml, zh) + nbd

    @pl.when(c == pl.num_programs(1) - 1)
    def _():
        zf_ref[0] = z_scr[...]


def rwkv_scan(mf, nf, rp, y0, z0):
    T = mf.shape[2]
    nc = T // CHUNK
    bdf = _chunk_masks()[5]
    order = lambda d, c: (d, 0, c + d * (nc - 1 - 2 * c), 0)
    blk = pl.BlockSpec((1, N_GROUPS, CHUNK, GROUP), order)
    zspec = pl.BlockSpec((1, N_GROUPS, GROUP, GROUP), lambda d, c: (d, 0, 0, 0))
    return pl.pallas_call(
        _rwkvscan_kernel,
        grid=(2, nc),
        in_specs=[blk, blk, blk, blk, zspec, pl.BlockSpec((GROUP, GROUP), lambda d, c: (0, 0))],
        out_specs=[blk, zspec],
        out_shape=[jax.ShapeDtypeStruct((2, N_GROUPS, T, GROUP), F32),
                   jax.ShapeDtypeStruct((2, N_GROUPS, GROUP, GROUP), F32)],
        scratch_shapes=[pltpu.VMEM((N_GROUPS, GROUP, GROUP), F32)],
        compiler_params=_cparams(("arbitrary", "arbitrary")),
    )(mf, nf, rp, y0, z0, bdf)


def _rwkvpost_kernel(y_ref, g_ref, bonus_ref, gw_ref, gb_ref, o_ref):
    for gi in range(N_GROUPS):
        y = y_ref[0, gi] + y_ref[1, gi]
        mu = _seg_sum(y) * (1.0 / RWKV_HEAD_DIM)
        yc = y - mu
        var = _seg_sum(yc * yc) * (1.0 / RWKV_HEAD_DIM)
        sl = slice(gi * GROUP, (gi + 1) * GROUP)
        yn = yc * lax.rsqrt(var + GN_EPS) * gw_ref[:, sl] + gb_ref[:, sl]
        o_ref[:, sl] = ((yn + bonus_ref[gi]) * g_ref[gi]).astype(o_ref.dtype)


def rwkv_post(y, g, bonus, gn_w, gn_b):
    T = y.shape[2]
    tm = min(256, T)
    W = RWKV_WIDTH
    return pl.pallas_call(
        _rwkvpost_kernel,
        grid=(T // tm,),
        in_specs=[pl.BlockSpec((2, N_GROUPS, tm, GROUP), lambda i: (0, 0, i, 0)),
                  pl.BlockSpec((N_GROUPS, tm, GROUP), lambda i: (0, i, 0)),
                  pl.BlockSpec((N_GROUPS, tm, GROUP), lambda i: (0, i, 0)),
                  pl.BlockSpec((1, W), lambda i: (0, 0)), pl.BlockSpec((1, W), lambda i: (0, 0))],
        out_specs=pl.BlockSpec((tm, W), lambda i: (i, 0)),
        out_shape=jax.ShapeDtypeStruct((T, W), BF16),
        compiler_params=_cparams(("parallel",)),
    )(y, g, bonus, gn_w.reshape(1, W), gn_b.reshape(1, W))


def rwkv_mix(z, p, z0):
    r, v, kk, g, bonus, logw, kd, ag = rwkv_prep(z, p)
    mf, nf, rp, y0 = rwkv_chunk(r, v, kk, logw, kd, ag)
    y, zf = rwkv_scan(mf, nf, rp, y0, z0)
    return rwkv_post(y, g, bonus, p['gn_w'], p['gn_b']), zf


def _lora_heads(l, w2, a2, g2):
    W = RWKV_WIDTH
    zw = jnp.zeros((W_LORA, W), F32)
    w2bd = jnp.concatenate([jnp.concatenate([w2[l, 0], zw], axis=1), jnp.concatenate([zw, w2[l, 1]], axis=1)], axis=0)
    a2bd = jnp.concatenate([jnp.concatenate([a2[l, 0], zw], axis=1), jnp.concatenate([zw, a2[l, 1]], axis=1)], axis=0)
    g2p = jnp.pad(g2[l], ((0, LORA_PAD - 256 - G_LORA), (0, 0)))
    return dict(w2bd=w2bd.astype(BF16), a2bd=a2bd.astype(BF16), g2p=g2p.astype(BF16))


def _layer(x, ctx, mod_x, mod_c, p, l, bias_tab, ctx_out):
    Dm = D_MODEL
    sh1, sc1, gt1, sh2, sc2, gt2 = [mod_x[i * Dm:(i + 1) * Dm] for i in range(6)]
    csh1, csc1, cgt1, csh2, csc2, cgt2 = [mod_c[i * Dm:(i + 1) * Dm] for i in range(6)]
    zx = norm_mod_matmul(x, p['norm1_w'], sc1, sh1, p['w_in'], l)
    zc = norm_mod_matmul(ctx, p['norm1_w'], csc1, csh1, p['w_in'], l)
    fx = fourier_mix(zx, p['fourier_w'], p['fourier_b'])
    qx, kx = qk_prep(zx, p['q_norm_w'], p['k_norm_w'], rope=True)
    qc, kc = qk_prep(zc, p['q_norm_w'], p['k_norm_w'], rope=False)
    ax = na_attention(qx, kx, zx, kc, zc, bias_tab)
    z0 = jnp.zeros((2, N_GROUPS, GROUP, GROUP), F32)
    rc, zf = rwkv_mix(zc, p, z0)
    rx, _ = rwkv_mix(zx, p, zf)
    x = matmul_residual([fx, ax, rx], p['w_out'], l, x, gt1)
    u = norm_mod_matmul(x, p['norm2_w'], sc2, sh2, p['w_ffn_in'], l, out_dtype=BF16)
    x = matmul_residual([conv_gate(u, p['ffn_conv'])], p['w_ffn_out'], l, x, gt2)
    if ctx_out:
        fc = fourier_mix_ctx(zc, p['fourier_w'], p['fourier_b'])
        ac = ctx_attention(qc, kc, zc)
        ctx = matmul_residual([fc, ac, rc], p['w_out'], l, ctx, cgt1)
        uc = norm_mod_matmul(ctx, p['norm2_w'], csc2, csh2, p['w_ffn_in'], l, out_dtype=BF16)
        ctx = matmul_residual([conv_gate(uc, p['ffn_conv'])], p['w_ffn_out'], l, ctx, cgt2)
    return x, ctx


def kernel(x, c, ctx, c_ctx, ada_w, ada_b, norm1_w, norm2_w, w_in, fourier_w, fourier_b, q_norm_w, k_norm_w, rpb,
           rwkv_conv, w0, w2, a0, a2, g2, k_k, k_a, r_k, gn_w, gn_b, w_out, ffn_conv, w_ffn_in, w_ffn_out):
    L = ada_w.shape[0]
    xs = x[0]
    cs = ctx[0]
    mods = ada_mod(jnp.concatenate([c, c_ctx[None, :]], axis=0), ada_w, ada_b)
    bias_tabs = rpb_bias_tables(rpb)
    big = dict(w_in=cast_w_in(w_in), w_out=cast_bf16(w_out), w_ffn_in=cast_bf16(w_ffn_in), w_ffn_out=cast_bf16(w_ffn_out))
    for l in range(L):
        p = dict(big)
        p.update(_lora_heads(l, w2, a2, g2))
        p.update(norm1_w=norm1_w[l], norm2_w=norm2_w[l], fourier_w=fourier_w[l], fourier_b=fourier_b[l],
                 q_norm_w=q_norm_w[l], k_norm_w=k_norm_w[l], rwkv_conv=rwkv_conv[l], w0=w0[l], a0=a0[l],
                 k_k=k_k[l], k_a=k_a[l], r_k=r_k[l], gn_w=gn_w[l], gn_b=gn_b[l], ffn_conv=ffn_conv[l])
        xs, cs = _layer(xs, cs, mods[l, 0], mods[l, 1], p, l, bias_tabs[l], l < L - 1)
    return xs[None]
```

```python
import functools
import math

import jax
import jax.numpy as jnp
import numpy as np
from jax import lax
from jax.experimental import pallas as pl
from jax.experimental.pallas import tpu as pltpu

F32 = jnp.float32
BF16 = jnp.bfloat16

D_MODEL = 4096
GRID_W = 64
FOURIER_WIDTH = 1024
FOURIER_HEADS = 4
FOURIER_DIM = 256
NA_WIDTH = 1536
NA_HEAD_DIM = 128
NA_HEADS = 12
NA_KH = 8
NA_KW = 16
RWKV_WIDTH = 1536
RWKV_HEAD_DIM = 64
W_LORA = 64
A_LORA = 64
G_LORA = 224
D_FF = 5120
ROPE_THETA = 10000.0
NORM_EPS = 1e-6
GN_EPS = 64e-5

COL_Q = 0
COL_K = NA_WIDTH
COL_V = 2 * NA_WIDTH
COL_R = 3 * NA_WIDTH
COL_F = 6 * NA_WIDTH
COL_L = COL_F + FOURIER_WIDTH
LORA_PAD = 512
IN_COLS_PAD = COL_L + LORA_PAD

CHUNK = 64
GROUP = 256
N_GROUPS = RWKV_WIDTH // GROUP
FFT_B = 128

LANE = 128
VMEM_LIMIT = 48 * 1024 * 1024


def _cparams(sem):
    return pltpu.CompilerParams(dimension_semantics=sem, vmem_limit_bytes=VMEM_LIMIT)


def _dot(a, b):
    return jnp.dot(a, b, preferred_element_type=F32)


def _dot_nt(a, b):
    return lax.dot_general(a, b, (((1,), (1,)), ((), ())), preferred_element_type=F32)


def _dot_tn(a, b):
    return lax.dot_general(a, b, (((0,), (0,)), ((), ())), preferred_element_type=F32)


def _split(x):
    hi = x.astype(BF16)
    lo = (x - hi.astype(F32)).astype(BF16)
    return hi, lo


def _ada_kernel(s_ref, w_ref, b_ref, o_ref, acc_ref):
    k = pl.program_id(2)
    tk, tn = w_ref.shape[1], w_ref.shape[2]
    rep = tn // LANE

    @pl.when(k == 0)
    def _():
        acc_ref[...] = jnp.zeros_like(acc_ref)

    def body(i, carry):
        a0, a1 = carry
        r = pl.multiple_of(i * 8, 8)
        w = w_ref[0, pl.ds(r, 8), :]
        s0 = s_ref[0, pl.ds(r, 8), :]
        s1 = s_ref[1, pl.ds(r, 8), :]
        s0 = s0 * jax.nn.sigmoid(s0)
        s1 = s1 * jax.nn.sigmoid(s1)
        a0 = a0 + w * jnp.concatenate([s0] * rep, axis=1)
        a1 = a1 + w * jnp.concatenate([s1] * rep, axis=1)
        return a0, a1

    a0, a1 = lax.fori_loop(0, tk // 8, body, (acc_ref[0], acc_ref[1]), unroll=4)
    acc_ref[0] = a0
    acc_ref[1] = a1

    @pl.when(k == pl.num_programs(2) - 1)
    def _():
        o_ref[0, 0:1, :] = jnp.sum(a0, axis=0, keepdims=True) + b_ref[0]
        o_ref[0, 1:2, :] = jnp.sum(a1, axis=0, keepdims=True) + b_ref[0]


def ada_mod(cc, ada_w, ada_b):
    L, K, N = ada_w.shape
    tk, tn = 2048, 1024
    s_b = jnp.broadcast_to(cc[:, :, None], (2, K, LANE))
    return pl.pallas_call(
        _ada_kernel,
        grid=(L, N // tn, K // tk),
        in_specs=[
            pl.BlockSpec((2, tk, LANE), lambda l, j, k: (0, k, 0)),
            pl.BlockSpec((1, tk, tn), lambda l, j, k: (l, k, j)),
            pl.BlockSpec((1, 1, tn), lambda l, j, k: (l, 0, j)),
        ],
        out_specs=pl.BlockSpec((1, 2, tn), lambda l, j, k: (l, 0, j)),
        out_shape=jax.ShapeDtypeStruct((L, 2, N), F32),
        scratch_shapes=[pltpu.VMEM((2, 8, tn), F32)],
        compiler_params=_cparams(("parallel", "parallel", "arbitrary")),
    )(s_b, ada_w, ada_b.reshape(L, 1, N))


def _normmod_kernel(x_ref, nw_ref, sc_ref, sh_ref, o_ref):
    x = x_ref[...]
    ms = jnp.mean(x * x, axis=-1, keepdims=True)
    y = x * lax.rsqrt(ms + NORM_EPS) * nw_ref[...]
    o_ref[...] = (y * (1.0 + sc_ref[...]) + sh_ref[...]).astype(o_ref.dtype)


def _mm_kernel(a_ref, w_ref, o_ref):
    o_ref[...] = _dot(a_ref[...], w_ref[0]).astype(o_ref.dtype)


def _cast_kernel(x_ref, o_ref):
    o_ref[...] = x_ref[...].astype(o_ref.dtype)


def cast_bf16(w):
    L, K, N = w.shape
    tk, tn = 512, 2048
    spec = pl.BlockSpec((1, tk, tn), lambda l, i, j: (l, i, j))
    return pl.pallas_call(
        _cast_kernel,
        grid=(L, K // tk, N // tn),
        in_specs=[spec],
        out_specs=spec,
        out_shape=jax.ShapeDtypeStruct((L, K, N), BF16),
        compiler_params=_cparams(("parallel", "parallel", "parallel")),
    )(w)


LORA_COLS = 2 * W_LORA + 2 * A_LORA + G_LORA
W_IN_BLOCKS = IN_COLS_PAD // 512


def _cast_win_kernel(x_ref, o_ref):
    row = lax.broadcasted_iota(jnp.int32, x_ref.shape[1:], 0)
    keep = (pl.program_id(2) < W_IN_BLOCKS - 1) | (row < LORA_COLS)
    o_ref[0] = jnp.where(keep, x_ref[0], 0.0).T.astype(BF16)


def cast_w_in(w_in):
    L, K, _ = w_in.shape
    tk = 1024
    n_front = (FOURIER_WIDTH) // 512
    n_mid = W_IN_BLOCKS - 1 - n_front

    def src(j):
        return jnp.where(j < n_mid, j + n_front, jnp.where(j < W_IN_BLOCKS - 1, j - n_mid, W_IN_BLOCKS - 1))

    return pl.pallas_call(
        _cast_win_kernel,
        grid=(L, K // tk, W_IN_BLOCKS),
        in_specs=[pl.BlockSpec((1, 512, tk), lambda l, i, j: (l, src(j), i))],
        out_specs=pl.BlockSpec((1, tk, 512), lambda l, i, j: (l, i, j)),
        out_shape=jax.ShapeDtypeStruct((L, K, IN_COLS_PAD), BF16),
        compiler_params=_cparams(("parallel", "parallel", "parallel")),
    )(jnp.swapaxes(w_in, 1, 2))


def norm_mod_matmul(x, nw, sc, sh, w, l, out_dtype=F32):
    M, K = x.shape
    N = w.shape[2]
    tr = min(256, M)
    vec = pl.BlockSpec((1, K), lambda i: (0, 0))
    h = pl.pallas_call(
        _normmod_kernel,
        grid=(M // tr,),
        in_specs=[pl.BlockSpec((tr, K), lambda i: (i, 0)), vec, vec, vec],
        out_specs=pl.BlockSpec((tr, K), lambda i: (i, 0)),
        out_shape=jax.ShapeDtypeStruct((M, K), BF16),
        compiler_params=_cparams(("parallel",)),
    )(x, nw.reshape(1, K), sc.reshape(1, K), sh.reshape(1, K))
    tm = min(1024, M)
    tn = next(t for t in (1024, 768, 512) if N % t == 0)
    return pl.pallas_call(
        _mm_kernel,
        grid=(M // tm, N // tn),
        in_specs=[
            pl.BlockSpec((tm, K), lambda i, j: (i, 0)),
            pl.BlockSpec((1, K, tn), lambda i, j: (l, 0, j)),
        ],
        out_specs=pl.BlockSpec((tm, tn), lambda i, j: (i, j)),
        out_shape=jax.ShapeDtypeStruct((M, N), out_dtype),
        compiler_params=_cparams(("parallel", "arbitrary")),
    )(h, w)


def _mmres_kernel(*refs, ksplits):
    n = len(ksplits)
    a_refs = refs[:n]
    w_ref, x_ref, g_ref, o_ref = refs[n:]
    acc = None
    off = 0
    for a_ref, kp in zip(a_refs, ksplits):
        part = _dot(a_ref[...].astype(BF16), w_ref[0, off:off + kp, :])
        acc = part if acc is None else acc + part
        off += kp
    o_ref[...] = x_ref[...] + g_ref[...] * acc


def matmul_residual(parts, w, l, x, gate):
    M, N = x.shape
    K = w.shape[1]
    ksplits = tuple(p.shape[1] for p in parts)
    assert sum(ksplits) == K
    tm = min(1024, M)
    tn = 512
    in_specs = [pl.BlockSpec((tm, kp), lambda i, j: (i, 0)) for kp in ksplits]
    in_specs += [
        pl.BlockSpec((1, K, tn), lambda i, j: (l, 0, j)),
        pl.BlockSpec((tm, tn), lambda i, j: (i, j)),
        pl.BlockSpec((1, tn), lambda i, j: (0, j)),
    ]
    return pl.pallas_call(
        functools.partial(_mmres_kernel, ksplits=ksplits),
        grid=(M // tm, N // tn),
        in_specs=in_specs,
        out_specs=pl.BlockSpec((tm, tn), lambda i, j: (i, j)),
        out_shape=jax.ShapeDtypeStruct((M, N), F32),
        compiler_params=_cparams(("parallel", "arbitrary")),
    )(*parts, w, x, gate.reshape(1, N))


def _conv3(main, prev_row, next_row, w):
    tm = main.shape[0]
    row = lax.broadcasted_iota(jnp.int32, main.shape, 0)
    dn = jnp.where(row == 0, prev_row, pltpu.roll(main, 1, axis=0))
    up = jnp.where(row == tm - 1, next_row, pltpu.roll(main, tm - 1, axis=0))
    return dn * w[0:1, :] + main * w[1:2, :] + up * w[2:3, :]


def _halo_rows(prev_ref, next_ref, i, n_i):
    prev = prev_ref[...].astype(F32)
    nxt = next_ref[...].astype(F32)
    hr = prev.shape[0]
    prev_row = jnp.where(i == 0, 0.0, prev[hr - 1:hr, :])
    next_row = jnp.where(i == n_i - 1, 0.0, nxt[0:1, :])
    return prev_row, next_row


def _halo_specs(tm, tn, n_rows, col_fn, hr=8):
    rb = tm // hr
    last = n_rows // hr - 1
    return [
        pl.BlockSpec((tm, tn), lambda i, j: (i, col_fn(j))),
        pl.BlockSpec((hr, tn), lambda i, j: (jnp.maximum(i * rb - 1, 0), col_fn(j))),
        pl.BlockSpec((hr, tn), lambda i, j: (jnp.minimum((i + 1) * rb, last), col_fn(j))),
    ]


def _convgate_kernel(a_ref, ap_ref, an_ref, b_ref, bp_ref, bn_ref, wa_ref, wb_ref, o_ref):
    i = pl.program_id(0)
    n_i = pl.num_programs(0)
    pa, na = _halo_rows(ap_ref, an_ref, i, n_i)
    pb, nb = _halo_rows(bp_ref, bn_ref, i, n_i)
    a = _conv3(a_ref[...].astype(F32), pa, na, wa_ref[...])
    b = _conv3(b_ref[...].astype(F32), pb, nb, wb_ref[...])
    o_ref[...] = (a * jax.nn.sigmoid(a) * b).astype(o_ref.dtype)


def conv_gate(u, conv_w):
    M, N2 = u.shape
    F = N2 // 2
    tm = min(512, M)
    tn = 1024
    nb = F // tn
    specs = _halo_specs(tm, tn, M, lambda j: j, 16) + _halo_specs(tm, tn, M, lambda j: j + nb, 16)
    specs += [pl.BlockSpec((3, tn), lambda i, j: (0, j)), pl.BlockSpec((3, tn), lambda i, j: (0, j + nb))]
    return pl.pallas_call(
        _convgate_kernel,
        grid=(M // tm, nb),
        in_specs=specs,
        out_specs=pl.BlockSpec((tm, tn), lambda i, j: (i, j)),
        out_shape=jax.ShapeDtypeStruct((M, F), BF16),
        compiler_params=_cparams(("parallel", "parallel")),
    )(u, u, u, u, u, u, conv_w, conv_w)


def _dft_tables(T):
    A, B = T // FFT_B, FFT_B
    ka = jnp.arange(A, dtype=jnp.int32)
    a = jnp.arange(A, dtype=jnp.int32)
    b = jnp.arange(B, dtype=jnp.int32)
    n = (ka[None, :, None] * (B * a[None, None, :] + b[:, None, None])) % T
    ang = n.astype(F32) * (2.0 * math.pi / T)
    sa = 1.0 / math.sqrt(A)
    m1 = jnp.concatenate([jnp.cos(ang) * sa, -jnp.sin(ang) * sa], axis=1).astype(BF16)
    kb = jnp.arange(B, dtype=jnp.int32)
    n2 = (kb[:, None] * b[None, :]) % B
    ang2 = n2.astype(F32) * (2.0 * math.pi / B)
    sb = 1.0 / math.sqrt(B)
    m2 = jnp.concatenate([jnp.cos(ang2) * sb, jnp.sin(ang2) * sb], axis=1).astype(BF16)
    return m1, m2


def _channel_dft():
    c = np.arange(FOURIER_DIM)
    ang = 2.0 * np.pi * ((c[:, None] * c[None, :]) % FOURIER_DIM) / FOURIER_DIM
    s = 1.0 / math.sqrt(FOURIER_DIM)
    return np.cos(ang) * s, np.sin(ang) * s


FFT_SUB = 8


def _f1_kernel(x_ref, m_ref, ch_ref, re_ref, im_ref):
    A = x_ref.shape[0]
    ch = ch_ref[...]
    for j in range(FFT_SUB):
        y = _dot(m_ref[j], x_ref[:, j, :].astype(BF16))
        for h in range(2):
            sl = slice(h * FOURIER_DIM, (h + 1) * FOURIER_DIM)
            lhs = jnp.concatenate([y[:A, sl], y[A:, sl]], axis=1).astype(BF16)
            yp = _dot(lhs, ch)
            re_ref[j, :, sl] = yp[:, :FOURIER_DIM]
            im_ref[j, :, sl] = yp[:, FOURIER_DIM:]


def _f2_kernel(re_ref, im_ref, m_ref, w_ref, b_ref, o_ref):
    for j in range(FFT_SUB):
        rhs = jnp.concatenate([re_ref[:, j, :], im_ref[:, j, :]], axis=0).astype(BF16)
        spec = _dot(m_ref[...], rhs)
        outs = []
        for h in range(FOURIER_HEADS):
            sl = slice(h * FOURIER_DIM, (h + 1) * FOURIER_DIM)
            outs.append(_dot(spec[:, sl].astype(BF16), w_ref[h].astype(BF16)) + b_ref[h])
        o_ref[:, j, :] = jnp.concatenate(outs, axis=1)


def fourier_mix(z, fw, fb):
    T, NC = z.shape
    A, B = T // FFT_B, FFT_B
    m1, m2 = _dft_tables(T)
    cc, ss = _channel_dft()
    ch = jnp.asarray(np.block([[cc, -ss], [ss, cc]]), BF16)
    cb0 = COL_F // 512
    yre, yim = pl.pallas_call(
        _f1_kernel,
        grid=(B // FFT_SUB, 2),
        in_specs=[
            pl.BlockSpec((A, FFT_SUB, 512), lambda b, c: (0, b, cb0 + c)),
            pl.BlockSpec((FFT_SUB, 2 * A, A), lambda b, c: (b, 0, 0)),
            pl.BlockSpec((512, 512), lambda b, c: (0, 0)),
        ],
        out_specs=[pl.BlockSpec((FFT_SUB, A, 512), lambda b, c: (b, 0, c))] * 2,
        out_shape=[jax.ShapeDtypeStruct((B, A, FOURIER_WIDTH), F32)] * 2,
        compiler_params=_cparams(("parallel", "parallel")),
    )(z.reshape(A, B, NC), m1, ch)
    blk = pl.BlockSpec((B, FFT_SUB, FOURIER_WIDTH), lambda i: (0, i, 0))
    out = pl.pallas_call(
        _f2_kernel,
        grid=(A // FFT_SUB,),
        in_specs=[
            blk, blk,
            pl.BlockSpec((B, 2 * B), lambda i: (0, 0)),
            pl.BlockSpec((FOURIER_HEADS, FOURIER_DIM, FOURIER_DIM), lambda i: (0, 0, 0)),
            pl.BlockSpec((FOURIER_HEADS, 1, FOURIER_DIM), lambda i: (0, 0, 0)),
        ],
        out_specs=blk,
        out_shape=jax.ShapeDtypeStruct((B, A, FOURIER_WIDTH), F32),
        compiler_params=_cparams(("parallel",)),
    )(yre, yim, m2, fw, fb.reshape(FOURIER_HEADS, 1, FOURIER_DIM))
    return out.reshape(T, FOURIER_WIDTH)


def _fctx_kernel(f_ref, cs_ref, ts_ref, w_ref, b_ref, o_ref):
    for h in range(FOURIER_HEADS):
        sl = slice(h * FOURIER_DIM, (h + 1) * FOURIER_DIM)
        g = _dot(f_ref[:, sl].astype(BF16), cs_ref[...])
        gg = jnp.concatenate([g[:, :FOURIER_DIM], g[:, FOURIER_DIM:]], axis=0).astype(BF16)
        spec = _dot(ts_ref[...], gg)
        out = _dot(spec.astype(BF16), w_ref[h].astype(BF16)) + b_ref[h]
        o_ref[:, sl] = out.astype(o_ref.dtype)


def fourier_mix_ctx(zc, fw, fb):
    T = zc.shape[0]
    cc, ss = _channel_dft()
    cs = jnp.asarray(np.concatenate([cc, ss], axis=1), BF16)
    t = np.arange(T)
    ang = 2.0 * np.pi * ((t[:, None] * t[None, :]) % T) / T
    st = 1.0 / math.sqrt(T)
    ts = jnp.asarray(np.concatenate([np.cos(ang) * st, -np.sin(ang) * st], axis=1), BF16)
    return pl.pallas_call(
        _fctx_kernel,
        grid=(1,),
        in_specs=[
            pl.BlockSpec((T, FOURIER_WIDTH), lambda i: (0, COL_F // FOURIER_WIDTH)),
            pl.BlockSpec((FOURIER_DIM, 2 * FOURIER_DIM), lambda i: (0, 0)),
            pl.BlockSpec((T, 2 * T), lambda i: (0, 0)),
            pl.BlockSpec((FOURIER_HEADS, FOURIER_DIM, FOURIER_DIM), lambda i: (0, 0, 0)),
            pl.BlockSpec((FOURIER_HEADS, 1, FOURIER_DIM), lambda i: (0, 0, 0)),
        ],
        out_specs=pl.BlockSpec((T, FOURIER_WIDTH), lambda i: (0, 0)),
        out_shape=jax.ShapeDtypeStruct((T, FOURIER_WIDTH), BF16),
        compiler_params=_cparams(("arbitrary",)),
    )(zc, cs, ts, fw, fb.reshape(FOURIER_HEADS, 1, FOURIER_DIM))


def _rope_tables(T):
    nf = NA_HEAD_DIM // 4
    t = jnp.arange(T)
    inv = 1.0 / (ROPE_THETA ** (jnp.arange(nf, dtype=F32) / nf))
    lane = jnp.arange(NA_HEAD_DIM)
    pos = jnp.where(lane[None, :] < NA_HEAD_DIM // 2, (t // GRID_W)[:, None], (t % GRID_W)[:, None]).astype(F32)
    ang = pos * inv[lane % nf][None, :]
    sign = jnp.where((lane % (2 * nf)) < nf, -1.0, 1.0)[None, :]
    return jnp.cos(ang), jnp.sin(ang) * sign


def _head_norm_rope(x, w, cos, sin):
    ms = jnp.mean(x * x, axis=-1, keepdims=True)
    y = x * lax.rsqrt(ms + NORM_EPS) * w
    if cos is None:
        return y
    lane = lax.broadcasted_iota(jnp.int32, y.shape, 1)
    swap = jnp.where((lane % 64) < 32, pltpu.roll(y, 96, axis=1), pltpu.roll(y, 32, axis=1))
    return y * cos + swap * sin


def _qkprep_kernel(q_ref, k_ref, qw_ref, kw_ref, cos_ref, sin_ref, qo_ref, ko_ref, *, rope):
    cos = cos_ref[...] if rope else None
    sin = sin_ref[...] if rope else None
    qw = qw_ref[...] * (NA_HEAD_DIM ** -0.5)
    kw = kw_ref[...]
    for h in range(4):
        sl = slice(h * LANE, (h + 1) * LANE)
        qo_ref[:, sl] = _head_norm_rope(q_ref[:, sl], qw, cos, sin).astype(qo_ref.dtype)
        ko_ref[:, sl] = _head_norm_rope(k_ref[:, sl], kw, cos, sin).astype(ko_ref.dtype)


def qk_prep(z, qw, kw, rope):
    T = z.shape[0]
    tm = min(512, T)
    if rope:
        cos, sin = _rope_tables(T)
    else:
        cos = sin = jnp.zeros((T, LANE), F32)
    nq = NA_WIDTH // 512
    return pl.pallas_call(
        functools.partial(_qkprep_kernel, rope=rope),
        grid=(T // tm, nq),
        in_specs=[
            pl.BlockSpec((tm, 512), lambda i, j: (i, COL_Q // 512 + j)),
            pl.BlockSpec((tm, 512), lambda i, j: (i, COL_K // 512 + j)),
            pl.BlockSpec((1, LANE), lambda i, j: (0, 0)),
            pl.BlockSpec((1, LANE), lambda i, j: (0, 0)),
            pl.BlockSpec((tm, LANE), lambda i, j: (i, 0)),
            pl.BlockSpec((tm, LANE), lambda i, j: (i, 0)),
        ],
        out_specs=[pl.BlockSpec((tm, 512), lambda i, j: (i, j))] * 2,
        out_shape=[jax.ShapeDtypeStruct((T, NA_WIDTH), BF16)] * 2,
        compiler_params=_cparams(("parallel", "parallel")),
    )(z, z, qw.reshape(1, LANE), kw.reshape(1, LANE), cos, sin)


def _toeplitz_kernel(r_ref, e_ref, o_ref):
    r = r_ref[0]
    acc = jnp.zeros(o_ref.shape[1:], F32)
    for d in range(2 * NA_KW - 1):
        acc = acc + r[:, d:d + 1] * e_ref[d:d + 1, :]
    o_ref[0] = acc


def rpb_bias_tables(rpb):
    L, H = rpb.shape[0], rpb.shape[1]
    ndr, ndc = 2 * NA_KH - 1, 2 * NA_KW - 1
    q = np.arange(GRID_W)
    e = np.zeros((32, GRID_W, GRID_W), np.float32)
    for d in range(ndc):
        e[d] = (q[None, :] - q[:, None] + (NA_KW - 1)) == d
    e = jnp.asarray(e.reshape(32, GRID_W * GRID_W))
    rp = jnp.pad(rpb.reshape(L * H, ndr, ndc), ((0, 0), (0, 16 - ndr), (0, 32 - ndc)))
    toep = pl.pallas_call(
        _toeplitz_kernel,
        grid=(L * H,),
        in_specs=[pl.BlockSpec((1, 16, 32), lambda i: (i, 0, 0)),
                  pl.BlockSpec((32, GRID_W * GRID_W), lambda i: (0, 0))],
        out_specs=pl.BlockSpec((1, 16, GRID_W * GRID_W), lambda i: (i, 0, 0)),
        out_shape=jax.ShapeDtypeStruct((L * H, 16, GRID_W * GRID_W), F32),
        compiler_params=_cparams(("parallel",)),
    )(rp, e)
    toep = toep.reshape(L, H, 16, GRID_W, GRID_W)
    col_start = np.clip(q - NA_KW // 2, 0, GRID_W - NA_KW)
    in_win = (q[None, :] >= col_start[:, None]) & (q[None, :] < col_start[:, None] + NA_KW)
    mask = jnp.asarray(np.where(in_win, 0.0, -1e30).astype(np.float32))
    tabs = []
    for o in range(NA_KH):
        band = toep[:, :, NA_KH - 1 - o:2 * NA_KH - 1 - o]
        band = band + mask[None, None, None]
        tabs.append(jnp.transpose(band, (0, 1, 3, 2, 4)).reshape(L, H, GRID_W, NA_KH * GRID_W))
    return jnp.stack(tabs, axis=2)


def _na_kernel(q_ref, kp_ref, kc_ref, kn_ref, vp_ref, vc_ref, vn_ref, kx_ref, vx_ref, bt_ref, o_ref,
               kbuf, vbuf, *, nrows):
    m = pl.program_id(1)
    blk = NA_KH * GRID_W
    kbuf[0:blk] = kp_ref[...]
    kbuf[blk:2 * blk] = kc_ref[...]
    kbuf[2 * blk:3 * blk] = kn_ref[...]
    vbuf[0:blk] = vp_ref[...].astype(BF16)
    vbuf[blk:2 * blk] = vc_ref[...].astype(BF16)
    vbuf[2 * blk:3 * blk] = vn_ref[...].astype(BF16)
    kctx = kx_ref[...]
    vctx = vx_ref[...].astype(BF16)
    starts, scores = [], []
    for j in range(NA_KH):
        r = m * NA_KH + j
        rs = jnp.clip(r - NA_KH // 2, 0, nrows - NA_KH)
        start = pl.multiple_of((rs - (m - 1) * NA_KH) * GRID_W, GRID_W)
        q = q_ref[j * GRID_W:(j + 1) * GRID_W, :]
        s = _dot_nt(q, kbuf[pl.ds(start, blk), :]) + bt_ref[0, r - rs]
        starts.append(start)
        scores.append((s, _dot_nt(q, kctx)))
    probs = []
    for s, sc in scores:
        mx = jnp.maximum(jnp.max(s, axis=-1, keepdims=True), jnp.max(sc, axis=-1, keepdims=True))
        p = jnp.exp(s - mx)
        pc = jnp.exp(sc - mx)
        den = jnp.sum(p, axis=-1, keepdims=True) + jnp.sum(pc, axis=-1, keepdims=True)
        probs.append((p.astype(BF16), pc.astype(BF16), den))
    for j, (p, pc, den) in enumerate(probs):
        acc = _dot(p, vbuf[pl.ds(starts[j], blk), :]) + _dot(pc, vctx)
        o_ref[j * GRID_W:(j + 1) * GRID_W, :] = (acc / den).astype(o_ref.dtype)


def na_attention(qn, kn, z, kcn, zc, bias_tab):
    T = qn.shape[0]
    C = kcn.shape[0]
    nrows = T // GRID_W
    blk = NA_KH * GRID_W
    nblk = T // blk
    vcol = COL_V // LANE
    prev = lambda h, m: (jnp.maximum(m - 1, 0), h)
    cur = lambda h, m: (m, h)
    nxt = lambda h, m: (jnp.minimum(m + 1, nblk - 1), h)
    vprev = lambda h, m: (jnp.maximum(m - 1, 0), vcol + h)
    vcur = lambda h, m: (m, vcol + h)
    vnxt = lambda h, m: (jnp.minimum(m + 1, nblk - 1), vcol + h)
    return pl.pallas_call(
        functools.partial(_na_kernel, nrows=nrows),
        grid=(NA_HEADS, nblk),
        in_specs=[
            pl.BlockSpec((blk, LANE), cur),
            pl.BlockSpec((blk, LANE), prev), pl.BlockSpec((blk, LANE), cur), pl.BlockSpec((blk, LANE), nxt),
            pl.BlockSpec((blk, LANE), vprev), pl.BlockSpec((blk, LANE), vcur), pl.BlockSpec((blk, LANE), vnxt),
            pl.BlockSpec((C, LANE), lambda h, m: (0, h)),
            pl.BlockSpec((C, LANE), lambda h, m: (0, vcol + h)),
            pl.BlockSpec((1, NA_KH, GRID_W, blk), lambda h, m: (h, 0, 0, 0)),
        ],
        out_specs=pl.BlockSpec((blk, LANE), cur),
        out_shape=jax.ShapeDtypeStruct((T, NA_WIDTH), BF16),
        scratch_shapes=[pltpu.VMEM((3 * blk, LANE), BF16), pltpu.VMEM((3 * blk, LANE), BF16)],
        compiler_params=_cparams(("parallel", "parallel")),
    )(qn, kn, kn, kn, z, z, z, kcn, zc, bias_tab)


def _ctxattn_kernel(q_ref, k_ref, v_ref, o_ref):
    s = _dot_nt(q_ref[...], k_ref[...])
    p = jnp.exp(s - jnp.max(s, axis=-1, keepdims=True))
    den = jnp.sum(p, axis=-1, keepdims=True)
    o_ref[...] = (_dot(p.astype(BF16), v_ref[...].astype(BF16)) / den).astype(o_ref.dtype)


def ctx_attention(qcn, kcn, zc):
    C = qcn.shape[0]
    vcol = COL_V // LANE
    return pl.pallas_call(
        _ctxattn_kernel,
        grid=(NA_HEADS,),
        in_specs=[pl.BlockSpec((C, LANE), lambda h: (0, h)), pl.BlockSpec((C, LANE), lambda h: (0, h)),
                  pl.BlockSpec((C, LANE), lambda h: (0, vcol + h))],
        out_specs=pl.BlockSpec((C, LANE), lambda h: (0, h)),
        out_shape=jax.ShapeDtypeStruct((C, NA_WIDTH), BF16),
        compiler_params=_cparams(("parallel",)),
    )(qcn, kcn, zc)


def _seg_sum64(x):
    lane = lax.broadcasted_iota(jnp.int32, x.shape, 1)
    low = lane < RWKV_HEAD_DIM
    s_lo = jnp.sum(jnp.where(low, x, 0.0), axis=-1, keepdims=True)
    s_hi = jnp.sum(jnp.where(low, 0.0, x), axis=-1, keepdims=True)
    return jnp.where(low, s_lo, s_hi)


def _seg_sum(x):
    return jnp.concatenate([_seg_sum64(x[:, i * LANE:(i + 1) * LANE]) for i in range(x.shape[1] // LANE)], axis=1)


def _rwkvprep_kernel(r_ref, rp_ref, rn_ref, k_ref, kp_ref, kn_ref, v_ref, vp_ref, vn_ref, lo_ref, cw_ref,
                     w2_ref, a2_ref, g2_ref, w0_ref, a0_ref, kk_ref, ka_ref, rk_ref,
                     r_o, v_o, kk_o, g_o, bonus_o, logw_o, kd_o, ag_o):
    i = pl.program_id(0)
    n_i = pl.num_programs(0)
    W = RWKV_WIDTH
    cw = cw_ref[...]
    r = _conv3(r_ref[...], *_halo_rows(rp_ref, rn_ref, i, n_i), cw[:, 0:W])
    k = _conv3(k_ref[...], *_halo_rows(kp_ref, kn_ref, i, n_i), cw[:, W:2 * W])
    v = _conv3(v_ref[...], *_halo_rows(vp_ref, vn_ref, i, n_i), cw[:, 2 * W:3 * W])
    lora = lo_ref[...]
    wl = _dot(jnp.tanh(lora[:, 0:2 * W_LORA]).astype(BF16), w2_ref[...])
    al = _dot(lora[:, 2 * W_LORA:2 * (W_LORA + A_LORA)].astype(BF16), a2_ref[...])
    g = _dot(jax.nn.sigmoid(lora[:, 2 * (W_LORA + A_LORA):]).astype(BF16), g2_ref[...])
    kkr = k * kk_ref[...]
    kk = kkr / jnp.maximum(jnp.sqrt(_seg_sum(kkr * kkr)), 1e-12)
    kds = []
    for d in range(2):
        u = -(w0_ref[d:d + 1, :] + wl[:, d * W:(d + 1) * W])
        w_log = -(jnp.maximum(u, 0.0) + jnp.log(1.0 + jnp.exp(-jnp.abs(u)))) - 0.5
        logw = -jnp.exp(w_log)
        a = jax.nn.sigmoid(a0_ref[d:d + 1, :] + al[:, d * W:(d + 1) * W])
        kd = k * (1.0 + (a - 1.0) * ka_ref[...])
        kds.append(kd)
        for gi in range(N_GROUPS):
            sl = slice(gi * GROUP, (gi + 1) * GROUP)
            logw_o[d, gi] = logw[:, sl]
            kd_o[d, gi] = kd[:, sl].astype(kd_o.dtype)
            ag_o[d, gi] = a[:, sl].astype(ag_o.dtype)
    bonus = _seg_sum(r * (kds[0] + kds[1]) * rk_ref[...]) * v
    for gi in range(N_GROUPS):
        sl = slice(gi * GROUP, (gi + 1) * GROUP)
        r_o[gi] = r[:, sl].astype(r_o.dtype)
        v_o[gi] = v[:, sl].astype(v_o.dtype)
        kk_o[gi] = kk[:, sl].astype(kk_o.dtype)
        g_o[gi] = g[:, sl].astype(g_o.dtype)
        bonus_o[gi] = bonus[:, sl].astype(bonus_o.dtype)


def rwkv_prep(z, p):
    T = z.shape[0]
    tm = min(128, T)
    W = RWKV_WIDTH
    cb = COL_R // W
    specs = []
    for c in range(3):
        specs += _halo_specs(tm, W, T, lambda j, c=c: cb + c)
    specs = [pl.BlockSpec(s.block_shape, lambda i, f=s.index_map: f(i, 0)) for s in specs]
    full = lambda shape: pl.BlockSpec(shape, lambda i: (0,) * len(shape))
    specs += [
        pl.BlockSpec((tm, LORA_PAD), lambda i: (i, COL_L // LORA_PAD)),
        full((3, 3 * W)), full((2 * W_LORA, 2 * W)), full((2 * A_LORA, 2 * W)), full((LORA_PAD - 256, W)),
        full((2, W)), full((2, W)), full((1, W)), full((1, W)), full((1, W)),
    ]
    g1 = pl.BlockSpec((N_GROUPS, tm, GROUP), lambda i: (0, i, 0))
    g2 = pl.BlockSpec((2, N_GROUPS, tm, GROUP), lambda i: (0, 0, i, 0))
    s1 = jax.ShapeDtypeStruct((N_GROUPS, T, GROUP), BF16)
    s2 = jax.ShapeDtypeStruct((2, N_GROUPS, T, GROUP), BF16)
    s2f = jax.ShapeDtypeStruct((2, N_GROUPS, T, GROUP), F32)
    return pl.pallas_call(
        _rwkvprep_kernel,
        grid=(T // tm,),
        in_specs=specs,
        out_specs=[g1] * 5 + [g2] * 3,
        out_shape=[s1] * 5 + [s2f, s2, s2],
        compiler_params=_cparams(("parallel",)),
    )(z, z, z, z, z, z, z, z, z, z, p['rwkv_conv'], p['w2bd'], p['a2bd'], p['g2p'], p['w0'], p['a0'],
      p['k_k'].reshape(1, W), p['k_a'].reshape(1, W), p['r_k'].reshape(1, W))


def _fold(x):
    c = CHUNK
    return x[0:c] + x[c:2 * c] + x[2 * c:3 * c] + x[3 * c:4 * c]


def _rwkv_kernel(r_ref, v_ref, kk_ref, logw_ref, kd_ref, ag_ref, tri_ref, ms_ref, mi_ref, bd_ref, bdf_ref, z0_ref,
                 y_ref, zf_ref, z_scr):
    forward = pl.program_id(0) == 0
    c = pl.program_id(1)

    @pl.when(c == 0)
    def _():
        z_scr[...] = z0_ref[0]

    tri = tri_ref[0]
    m_strict = ms_ref[0]
    m_incl = mi_ref[0]
    bd = bd_ref[...]
    bdf = bdf_ref[...]
    row = lax.broadcasted_iota(jnp.int32, (CHUNK, GROUP), 0)
    col = lax.broadcasted_iota(jnp.int32, (CHUNK, GROUP), 1)
    diag = row == (col % RWKV_HEAD_DIM)
    eye_f = jnp.where(diag, 1.0, 0.0)

    def expand(x):
        return jnp.concatenate([x.astype(BF16)] * 4, axis=0) * bd

    def stack(*xs):
        return jnp.concatenate([x.astype(BF16) for x in xs], axis=0)

    def prep(g):
        lw = logw_ref[0, g]
        hi, lo = _split(lw)
        cs = _dot(tri, jnp.concatenate([hi, lo], axis=1))
        linc = cs[:, :GROUP] + cs[:, GROUP:]
        ltot = jnp.where(forward, linc[CHUNK - 1:CHUNK, :], linc[0:1, :])
        e_inc = jnp.exp(linc)
        e_neg = jnp.exp(-linc)
        e_exc = jnp.exp(linc - lw)
        e_rem = jnp.exp(ltot - linc)
        kk = kk_ref[g].astype(F32)
        kd = kd_ref[0, g].astype(F32)
        b = kk * ag_ref[0, g].astype(F32)
        at = -kk * e_exc
        rt = r_ref[g].astype(F32) * e_inc
        gram = _dot_nt(stack(at, rt), jnp.concatenate([expand(b * e_neg), expand(kd * e_neg)], axis=0))
        fab = gram[:CHUNK, :GROUP] * m_strict
        return dict(g=g, rt=rt, at=at, e_tot=jnp.exp(ltot), bh=b * e_rem, kh=kd * e_rem, v=v_ref[g],
                    fp=fab, ft=eye_f + fab,
                    fak=gram[:CHUNK, GROUP:] * m_strict, frb=gram[CHUNK:, :GROUP] * m_incl,
                    frk=gram[CHUNK:, GROUP:] * m_incl)

    st = [prep(g) for g in range(N_GROUPS)]
    for s in st:
        s['fp'] = _dot(s['fp'].astype(BF16), expand(s['fp']))
    for _ in range(4):
        for s in st:
            res = _dot(stack(s['fp'], s['ft']), expand(s['fp']))
            s['fp'] = res[:CHUNK]
            s['ft'] = s['ft'] + res[CHUNK:]
    for s in st:
        s['ft'] = s['ft'] + _dot(s['ft'].astype(BF16), expand(s['fp']))
    for s in st:
        res = _dot(stack(s['fak'], s['frk']), expand(s['v']))
        s['w1'] = res[:CHUNK]
        s['rkv'] = res[CHUNK:]
    for s in st:
        fx = _dot(s['ft'].astype(BF16), jnp.concatenate([expand(s['at']), expand(s['w1'])], axis=1))
        s['fa1'] = fx[:, :GROUP]
        s['fu0'] = fx[:, GROUP:]
    for s in st:
        q = _dot(s['frb'].astype(BF16), jnp.concatenate([expand(s['fa1']), expand(s['fu0'])], axis=1))
        s['rp'] = s['rt'] + q[:, :GROUP]
        s['y0'] = q[:, GROUP:] + s['rkv']
    for s in st:
        mt = _dot_tn(s['bh'].astype(BF16), s['fa1'].astype(BF16))
        nt = _dot_tn(stack(s['bh'], s['kh']), stack(s['fu0'], s['v']))
        s['mf'] = _fold(mt * bdf) + jnp.where(diag, s['e_tot'], 0.0)
        s['nf'] = _fold(nt * bdf)
    for s in st:
        g = s['g']
        zh, zl = _split(z_scr[g])
        zhx = expand(zh)
        mh, ml = _split(s['mf'])
        res = _dot(jnp.concatenate([s['rp'].astype(BF16), mh, ml], axis=0), zhx)
        y_ref[0, g] = res[:CHUNK] + s['y0']
        z_scr[g] = res[CHUNK:2 * CHUNK] + res[2 * CHUNK:] + _dot(mh, expand(zl)) + s['nf']

    @pl.when(c == pl.num_programs(1) - 1)
    def _():
        zf_ref[0] = z_scr[...]


def _chunk_masks():
    i = np.arange(CHUNK)
    tri = np.stack([i[None, :] <= i[:, None], i[None, :] >= i[:, None]]).astype(np.float32)
    j = np.arange(GROUP)
    same = ((j[:, None] // CHUNK) == (j[None, :] // CHUNK)).astype(np.float32)
    js = j[None, :] % CHUNK
    strict = np.stack([js < i[:, None], js > i[:, None]]).astype(np.float32)
    incl = np.stack([js <= i[:, None], js >= i[:, None]]).astype(np.float32)
    return jnp.asarray(tri, BF16), jnp.asarray(strict), jnp.asarray(incl), jnp.asarray(same, BF16), jnp.asarray(same)


def rwkv_scan(r, v, kk, logw, kd, ag, z0):
    T = r.shape[1]
    nc = T // CHUNK
    tri, strict, incl, bd, bdf = _chunk_masks()
    order = lambda d, c: c + d * (nc - 1 - 2 * c)
    b1 = pl.BlockSpec((N_GROUPS, CHUNK, GROUP), lambda d, c: (0, order(d, c), 0))
    b2 = pl.BlockSpec((1, N_GROUPS, CHUNK, GROUP), lambda d, c: (d, 0, order(d, c), 0))
    per_dir = lambda shape: pl.BlockSpec((1,) + shape, lambda d, c: (d,) + (0,) * len(shape))
    full = lambda shape: pl.BlockSpec(shape, lambda d, c: (0,) * len(shape))
    zspec = per_dir((N_GROUPS, CHUNK, GROUP))
    return pl.pallas_call(
        _rwkv_kernel,
        grid=(2, nc),
        in_specs=[b1, b1, b1, b2, b2, b2, per_dir((CHUNK, CHUNK)), per_dir((CHUNK, GROUP)), per_dir((CHUNK, GROUP)),
                  full((GROUP, GROUP)), full((GROUP, GROUP)), zspec],
        out_specs=[b2, zspec],
        out_shape=[jax.ShapeDtypeStruct((2, N_GROUPS, T, GROUP), F32),
                   jax.ShapeDtypeStruct((2, N_GROUPS, CHUNK, GROUP), F32)],
        scratch_shapes=[pltpu.VMEM((N_GROUPS, CHUNK, GROUP), F32)],
        compiler_params=_cparams(("arbitrary", "arbitrary")),
    )(r, v, kk, logw, kd, ag, tri, strict, incl, bd, bdf, z0)


def _rwkvpost_kernel(y_ref, g_ref, bonus_ref, gw_ref, gb_ref, o_ref):
    for gi in range(N_GROUPS):
        y = y_ref[0, gi] + y_ref[1, gi]
        mu = _seg_sum(y) * (1.0 / RWKV_HEAD_DIM)
        yc = y - mu
        var = _seg_sum(yc * yc) * (1.0 / RWKV_HEAD_DIM)
        sl = slice(gi * GROUP, (gi + 1) * GROUP)
        yn = yc * lax.rsqrt(var + GN_EPS) * gw_ref[:, sl] + gb_ref[:, sl]
        o_ref[:, sl] = ((yn + bonus_ref[gi].astype(F32)) * g_ref[gi].astype(F32)).astype(o_ref.dtype)


def rwkv_post(y, g, bonus, gn_w, gn_b):
    T = y.shape[2]
    tm = min(256, T)
    W = RWKV_WIDTH
    return pl.pallas_call(
        _rwkvpost_kernel,
        grid=(T // tm,),
        in_specs=[pl.BlockSpec((2, N_GROUPS, tm, GROUP), lambda i: (0, 0, i, 0)),
                  pl.BlockSpec((N_GROUPS, tm, GROUP), lambda i: (0, i, 0)),
                  pl.BlockSpec((N_GROUPS, tm, GROUP), lambda i: (0, i, 0)),
                  pl.BlockSpec((1, W), lambda i: (0, 0)), pl.BlockSpec((1, W), lambda i: (0, 0))],
        out_specs=pl.BlockSpec((tm, W), lambda i: (i, 0)),
        out_shape=jax.ShapeDtypeStruct((T, W), BF16),
        compiler_params=_cparams(("parallel",)),
    )(y, g, bonus, gn_w.reshape(1, W), gn_b.reshape(1, W))


def rwkv_mix(z, p, z0):
    r, v, kk, g, bonus, logw, kd, ag = rwkv_prep(z, p)
    y, zf = rwkv_scan(r, v, kk, logw, kd, ag, z0)
    return rwkv_post(y, g, bonus, p['gn_w'], p['gn_b']), zf


def _lora_heads(l, w2, a2, g2):
    W = RWKV_WIDTH
    zw = jnp.zeros((W_LORA, W), F32)
    w2bd = jnp.concatenate([jnp.concatenate([w2[l, 0], zw], axis=1), jnp.concatenate([zw, w2[l, 1]], axis=1)], axis=0)
    a2bd = jnp.concatenate([jnp.concatenate([a2[l, 0], zw], axis=1), jnp.concatenate([zw, a2[l, 1]], axis=1)], axis=0)
    g2p = jnp.pad(g2[l], ((0, LORA_PAD - 256 - G_LORA), (0, 0)))
    return dict(w2bd=w2bd.astype(BF16), a2bd=a2bd.astype(BF16), g2p=g2p.astype(BF16))


def _layer(x, ctx, mod_x, mod_c, p, l, bias_tab, ctx_out):
    Dm = D_MODEL
    sh1, sc1, gt1, sh2, sc2, gt2 = [mod_x[i * Dm:(i + 1) * Dm] for i in range(6)]
    csh1, csc1, cgt1, csh2, csc2, cgt2 = [mod_c[i * Dm:(i + 1) * Dm] for i in range(6)]
    zx = norm_mod_matmul(x, p['norm1_w'], sc1, sh1, p['w_in'], l)
    zc = norm_mod_matmul(ctx, p['norm1_w'], csc1, csh1, p['w_in'], l)
    fx = fourier_mix(zx, p['fourier_w'], p['fourier_b'])
    qx, kx = qk_prep(zx, p['q_norm_w'], p['k_norm_w'], rope=True)
    qc, kc = qk_prep(zc, p['q_norm_w'], p['k_norm_w'], rope=False)
    ax = na_attention(qx, kx, zx, kc, zc, bias_tab)
    z0 = jnp.zeros((2, N_GROUPS, CHUNK, GROUP), F32)
    rc, zf = rwkv_mix(zc, p, z0)
    rx, _ = rwkv_mix(zx, p, zf)
    x = matmul_residual([fx, ax, rx], p['w_out'], l, x, gt1)
    u = norm_mod_matmul(x, p['norm2_w'], sc2, sh2, p['w_ffn_in'], l, out_dtype=BF16)
    x = matmul_residual([conv_gate(u, p['ffn_conv'])], p['w_ffn_out'], l, x, gt2)
    if ctx_out:
        fc = fourier_mix_ctx(zc, p['fourier_w'], p['fourier_b'])
        ac = ctx_attention(qc, kc, zc)
        ctx = matmul_residual([fc, ac, rc], p['w_out'], l, ctx, cgt1)
        uc = norm_mod_matmul(ctx, p['norm2_w'], csc2, csh2, p['w_ffn_in'], l, out_dtype=BF16)
        ctx = matmul_residual([conv_gate(uc, p['ffn_conv'])], p['w_ffn_out'], l, ctx, cgt2)
    return x, ctx


def kernel(x, c, ctx, c_ctx, ada_w, ada_b, norm1_w, norm2_w, w_in, fourier_w, fourier_b, q_norm_w, k_norm_w, rpb,
           rwkv_conv, w0, w2, a0, a2, g2, k_k, k_a, r_k, gn_w, gn_b, w_out, ffn_conv, w_ffn_in, w_ffn_out):
    L = ada_w.shape[0]
    xs = x[0]
    cs = ctx[0]
    mods = ada_mod(jnp.concatenate([c, c_ctx[None, :]], axis=0), ada_w, ada_b)
    bias_tabs = rpb_bias_tables(rpb)
    big = dict(w_in=cast_w_in(w_in), w_out=cast_bf16(w_out), w_ffn_in=cast_bf16(w_ffn_in), w_ffn_out=cast_bf16(w_ffn_out))
    for l in range(L):
        p = dict(big)
        p.update(_lora_heads(l, w2, a2, g2))
        p.update(norm1_w=norm1_w[l], norm2_w=norm2_w[l], fourier_w=fourier_w[l], fourier_b=fourier_b[l],
                 q_norm_w=q_norm_w[l], k_norm_w=k_norm_w[l], rwkv_conv=rwkv_conv[l], w0=w0[l], a0=a0[l],
                 k_k=k_k[l], k_a=k_a[l], r_k=r_k[l], gn_w=gn_w[l], gn_b=gn_b[l], ffn_conv=ffn_conv[l])
        xs, cs = _layer(xs, cs, mods[l, 0], mods[l, 1], p, l, bias_tabs[l], l < L - 1)
    return xs[None]
```

```python
import functools
import math

import jax
import jax.numpy as jnp
import numpy as np
from jax import lax
from jax.experimental import pallas as pl
from jax.experimental.pallas import tpu as pltpu

F32 = jnp.float32
BF16 = jnp.bfloat16

D_MODEL = 4096
GRID_W = 64
FOURIER_WIDTH = 1024
FOURIER_HEADS = 4
FOURIER_DIM = 256
NA_WIDTH = 1536
NA_HEAD_DIM = 128
NA_HEADS = 12
NA_KH = 8
NA_KW = 16
RWKV_WIDTH = 1536
RWKV_HEAD_DIM = 64
W_LORA = 64
A_LORA = 64
G_LORA = 224
D_FF = 5120
ROPE_THETA = 10000.0
NORM_EPS = 1e-6
GN_EPS = 64e-5

COL_Q = 0
COL_K = NA_WIDTH
COL_V = 2 * NA_WIDTH
COL_R = 3 * NA_WIDTH
COL_F = 6 * NA_WIDTH
COL_L = COL_F + FOURIER_WIDTH
LORA_PAD = 512
IN_COLS_PAD = COL_L + LORA_PAD

CHUNK = 64
GROUP = 256
N_GROUPS = RWKV_WIDTH // GROUP
FFT_B = 128

LANE = 128
VMEM_LIMIT = 48 * 1024 * 1024


def _cparams(sem):
    return pltpu.CompilerParams(dimension_semantics=sem, vmem_limit_bytes=VMEM_LIMIT)


def _dot(a, b):
    return jnp.dot(a, b, preferred_element_type=F32)


def _dot_nt(a, b):
    return lax.dot_general(a, b, (((1,), (1,)), ((), ())), preferred_element_type=F32)


def _dot_tn(a, b):
    return lax.dot_general(a, b, (((0,), (0,)), ((), ())), preferred_element_type=F32)


def _split(x):
    hi = x.astype(BF16)
    lo = (x - hi.astype(F32)).astype(BF16)
    return hi, lo


def _ada_kernel(s_ref, w_ref, b_ref, o_ref, acc_ref):
    k = pl.program_id(2)
    tk, tn = w_ref.shape[1], w_ref.shape[2]
    rep = tn // LANE

    @pl.when(k == 0)
    def _():
        acc_ref[...] = jnp.zeros_like(acc_ref)

    def body(i, carry):
        a0, a1 = carry
        r = pl.multiple_of(i * 8, 8)
        w = w_ref[0, pl.ds(r, 8), :]
        s0 = s_ref[0, pl.ds(r, 8), :]
        s1 = s_ref[1, pl.ds(r, 8), :]
        s0 = s0 * jax.nn.sigmoid(s0)
        s1 = s1 * jax.nn.sigmoid(s1)
        a0 = a0 + w * jnp.concatenate([s0] * rep, axis=1)
        a1 = a1 + w * jnp.concatenate([s1] * rep, axis=1)
        return a0, a1

    a0, a1 = lax.fori_loop(0, tk // 8, body, (acc_ref[0], acc_ref[1]), unroll=4)
    acc_ref[0] = a0
    acc_ref[1] = a1

    @pl.when(k == pl.num_programs(2) - 1)
    def _():
        o_ref[0, 0:1, :] = jnp.sum(a0, axis=0, keepdims=True) + b_ref[0]
        o_ref[0, 1:2, :] = jnp.sum(a1, axis=0, keepdims=True) + b_ref[0]


def ada_mod(cc, ada_w, ada_b):
    L, K, N = ada_w.shape
    tk, tn = 2048, 1024
    s_b = jnp.broadcast_to(cc[:, :, None], (2, K, LANE))
    return pl.pallas_call(
        _ada_kernel,
        grid=(L, N // tn, K // tk),
        in_specs=[
            pl.BlockSpec((2, tk, LANE), lambda l, j, k: (0, k, 0)),
            pl.BlockSpec((1, tk, tn), lambda l, j, k: (l, k, j)),
            pl.BlockSpec((1, 1, tn), lambda l, j, k: (l, 0, j)),
        ],
        out_specs=pl.BlockSpec((1, 2, tn), lambda l, j, k: (l, 0, j)),
        out_shape=jax.ShapeDtypeStruct((L, 2, N), F32),
        scratch_shapes=[pltpu.VMEM((2, 8, tn), F32)],
        compiler_params=_cparams(("parallel", "parallel", "arbitrary")),
    )(s_b, ada_w, ada_b.reshape(L, 1, N))


def _normmod_kernel(x_ref, nw_ref, sc_ref, sh_ref, o_ref):
    x = x_ref[...]
    ms = jnp.mean(x * x, axis=-1, keepdims=True)
    y = x * lax.rsqrt(ms + NORM_EPS) * nw_ref[...]
    o_ref[...] = (y * (1.0 + sc_ref[...]) + sh_ref[...]).astype(o_ref.dtype)


def _mm_kernel(a_ref, w_ref, o_ref):
    o_ref[...] = _dot(a_ref[...], w_ref[0]).astype(o_ref.dtype)


def _cast_kernel(x_ref, o_ref):
    o_ref[...] = x_ref[...].astype(o_ref.dtype)


def cast_bf16(w):
    L, K, N = w.shape
    tk, tn = 512, 2048
    spec = pl.BlockSpec((1, tk, tn), lambda l, i, j: (l, i, j))
    return pl.pallas_call(
        _cast_kernel,
        grid=(L, K // tk, N // tn),
        in_specs=[spec],
        out_specs=spec,
        out_shape=jax.ShapeDtypeStruct((L, K, N), BF16),
        compiler_params=_cparams(("parallel", "parallel", "parallel")),
    )(w)


LORA_COLS = 2 * W_LORA + 2 * A_LORA + G_LORA
W_IN_BLOCKS = IN_COLS_PAD // 512


def _cast_win_kernel(x_ref, o_ref):
    row = lax.broadcasted_iota(jnp.int32, x_ref.shape[1:], 0)
    keep = (pl.program_id(2) < W_IN_BLOCKS - 1) | (row < LORA_COLS)
    o_ref[0] = jnp.where(keep, x_ref[0], 0.0).T.astype(BF16)


def cast_w_in(w_in):
    L, K, _ = w_in.shape
    tk = 1024
    n_front = (FOURIER_WIDTH) // 512
    n_mid = W_IN_BLOCKS - 1 - n_front

    def src(j):
        return jnp.where(j < n_mid, j + n_front, jnp.where(j < W_IN_BLOCKS - 1, j - n_mid, W_IN_BLOCKS - 1))

    return pl.pallas_call(
        _cast_win_kernel,
        grid=(L, K // tk, W_IN_BLOCKS),
        in_specs=[pl.BlockSpec((1, 512, tk), lambda l, i, j: (l, src(j), i))],
        out_specs=pl.BlockSpec((1, tk, 512), lambda l, i, j: (l, i, j)),
        out_shape=jax.ShapeDtypeStruct((L, K, IN_COLS_PAD), BF16),
        compiler_params=_cparams(("parallel", "parallel", "parallel")),
    )(jnp.swapaxes(w_in, 1, 2))


def norm_mod_matmul(x, nw, sc, sh, w, l, out_dtype=F32):
    M, K = x.shape
    N = w.shape[2]
    tr = min(256, M)
    vec = pl.BlockSpec((1, K), lambda i: (0, 0))
    h = pl.pallas_call(
        _normmod_kernel,
        grid=(M // tr,),
        in_specs=[pl.BlockSpec((tr, K), lambda i: (i, 0)), vec, vec, vec],
        out_specs=pl.BlockSpec((tr, K), lambda i: (i, 0)),
        out_shape=jax.ShapeDtypeStruct((M, K), BF16),
        compiler_params=_cparams(("parallel",)),
    )(x, nw.reshape(1, K), sc.reshape(1, K), sh.reshape(1, K))
    tm = min(1024, M)
    tn = next(t for t in (1024, 768, 512) if N % t == 0)
    return pl.pallas_call(
        _mm_kernel,
        grid=(M // tm, N // tn),
        in_specs=[
            pl.BlockSpec((tm, K), lambda i, j: (i, 0)),
            pl.BlockSpec((1, K, tn), lambda i, j: (l, 0, j)),
        ],
        out_specs=pl.BlockSpec((tm, tn), lambda i, j: (i, j)),
        out_shape=jax.ShapeDtypeStruct((M, N), out_dtype),
        compiler_params=_cparams(("parallel", "arbitrary")),
    )(h, w)


def _mmres_kernel(*refs, ksplits):
    n = len(ksplits)
    a_refs = refs[:n]
    w_ref, x_ref, g_ref, o_ref = refs[n:]
    acc = None
    off = 0
    for a_ref, kp in zip(a_refs, ksplits):
        part = _dot(a_ref[...].astype(BF16), w_ref[0, off:off + kp, :])
        acc = part if acc is None else acc + part
        off += kp
    o_ref[...] = x_ref[...] + g_ref[...] * acc


def matmul_residual(parts, w, l, x, gate):
    M, N = x.shape
    K = w.shape[1]
    ksplits = tuple(p.shape[1] for p in parts)
    assert sum(ksplits) == K
    tm = min(1024, M)
    tn = 512
    in_specs = [pl.BlockSpec((tm, kp), lambda i, j: (i, 0)) for kp in ksplits]
    in_specs += [
        pl.BlockSpec((1, K, tn), lambda i, j: (l, 0, j)),
        pl.BlockSpec((tm, tn), lambda i, j: (i, j)),
        pl.BlockSpec((1, tn), lambda i, j: (0, j)),
    ]
    return pl.pallas_call(
        functools.partial(_mmres_kernel, ksplits=ksplits),
        grid=(M // tm, N // tn),
        in_specs=in_specs,
        out_specs=pl.BlockSpec((tm, tn), lambda i, j: (i, j)),
        out_shape=jax.ShapeDtypeStruct((M, N), F32),
        compiler_params=_cparams(("parallel", "arbitrary")),
    )(*parts, w, x, gate.reshape(1, N))


def _conv3(main, prev_row, next_row, w):
    tm = main.shape[0]
    row = lax.broadcasted_iota(jnp.int32, main.shape, 0)
    dn = jnp.where(row == 0, prev_row, pltpu.roll(main, 1, axis=0))
    up = jnp.where(row == tm - 1, next_row, pltpu.roll(main, tm - 1, axis=0))
    return dn * w[0:1, :] + main * w[1:2, :] + up * w[2:3, :]


def _halo_rows(prev_ref, next_ref, i, n_i):
    prev = prev_ref[...].astype(F32)
    nxt = next_ref[...].astype(F32)
    hr = prev.shape[0]
    prev_row = jnp.where(i == 0, 0.0, prev[hr - 1:hr, :])
    next_row = jnp.where(i == n_i - 1, 0.0, nxt[0:1, :])
    return prev_row, next_row


def _halo_specs(tm, tn, n_rows, col_fn, hr=8):
    rb = tm // hr
    last = n_rows // hr - 1
    return [
        pl.BlockSpec((tm, tn), lambda i, j: (i, col_fn(j))),
        pl.BlockSpec((hr, tn), lambda i, j: (jnp.maximum(i * rb - 1, 0), col_fn(j))),
        pl.BlockSpec((hr, tn), lambda i, j: (jnp.minimum((i + 1) * rb, last), col_fn(j))),
    ]


def _convgate_kernel(a_ref, ap_ref, an_ref, b_ref, bp_ref, bn_ref, wa_ref, wb_ref, o_ref):
    i = pl.program_id(0)
    n_i = pl.num_programs(0)
    pa, na = _halo_rows(ap_ref, an_ref, i, n_i)
    pb, nb = _halo_rows(bp_ref, bn_ref, i, n_i)
    a = _conv3(a_ref[...].astype(F32), pa, na, wa_ref[...])
    b = _conv3(b_ref[...].astype(F32), pb, nb, wb_ref[...])
    o_ref[...] = (a * jax.nn.sigmoid(a) * b).astype(o_ref.dtype)


def conv_gate(u, conv_w):
    M, N2 = u.shape
    F = N2 // 2
    tm = min(512, M)
    tn = 1024
    nb = F // tn
    specs = _halo_specs(tm, tn, M, lambda j: j, 16) + _halo_specs(tm, tn, M, lambda j: j + nb, 16)
    specs += [pl.BlockSpec((3, tn), lambda i, j: (0, j)), pl.BlockSpec((3, tn), lambda i, j: (0, j + nb))]
    return pl.pallas_call(
        _convgate_kernel,
        grid=(M // tm, nb),
        in_specs=specs,
        out_specs=pl.BlockSpec((tm, tn), lambda i, j: (i, j)),
        out_shape=jax.ShapeDtypeStruct((M, F), BF16),
        compiler_params=_cparams(("parallel", "parallel")),
    )(u, u, u, u, u, u, conv_w, conv_w)


def _dft_tables(T):
    A, B = T // FFT_B, FFT_B
    ka = jnp.arange(A, dtype=jnp.int32)
    a = jnp.arange(A, dtype=jnp.int32)
    b = jnp.arange(B, dtype=jnp.int32)
    n = (ka[None, :, None] * (B * a[None, None, :] + b[:, None, None])) % T
    ang = n.astype(F32) * (2.0 * math.pi / T)
    sa = 1.0 / math.sqrt(A)
    m1 = jnp.concatenate([jnp.cos(ang) * sa, -jnp.sin(ang) * sa], axis=1).astype(BF16)
    kb = jnp.arange(B, dtype=jnp.int32)
    n2 = (kb[:, None] * b[None, :]) % B
    ang2 = n2.astype(F32) * (2.0 * math.pi / B)
    sb = 1.0 / math.sqrt(B)
    m2 = jnp.concatenate([jnp.cos(ang2) * sb, jnp.sin(ang2) * sb], axis=1).astype(BF16)
    return m1, m2


def _channel_dft():
    c = np.arange(FOURIER_DIM)
    ang = 2.0 * np.pi * ((c[:, None] * c[None, :]) % FOURIER_DIM) / FOURIER_DIM
    s = 1.0 / math.sqrt(FOURIER_DIM)
    return np.cos(ang) * s, np.sin(ang) * s


FFT_SUB = 8


def _f1_kernel(x_ref, m_ref, ch_ref, re_ref, im_ref):
    A = x_ref.shape[0]
    ch = ch_ref[...]
    for j in range(FFT_SUB):
        y = _dot(m_ref[j], x_ref[:, j, :].astype(BF16))
        for h in range(2):
            sl = slice(h * FOURIER_DIM, (h + 1) * FOURIER_DIM)
            lhs = jnp.concatenate([y[:A, sl], y[A:, sl]], axis=1).astype(BF16)
            yp = _dot(lhs, ch)
            re_ref[j, :, sl] = yp[:, :FOURIER_DIM]
            im_ref[j, :, sl] = yp[:, FOURIER_DIM:]


def _f2_kernel(re_ref, im_ref, m_ref, w_ref, b_ref, o_ref):
    for j in range(FFT_SUB):
        rhs = jnp.concatenate([re_ref[:, j, :], im_ref[:, j, :]], axis=0).astype(BF16)
        spec = _dot(m_ref[...], rhs)
        outs = []
        for h in range(FOURIER_HEADS):
            sl = slice(h * FOURIER_DIM, (h + 1) * FOURIER_DIM)
            outs.append(_dot(spec[:, sl].astype(BF16), w_ref[h].astype(BF16)) + b_ref[h])
        o_ref[:, j, :] = jnp.concatenate(outs, axis=1)


def fourier_mix(z, fw, fb):
    T, NC = z.shape
    A, B = T // FFT_B, FFT_B
    m1, m2 = _dft_tables(T)
    cc, ss = _channel_dft()
    ch = jnp.asarray(np.block([[cc, -ss], [ss, cc]]), BF16)
    cb0 = COL_F // 512
    yre, yim = pl.pallas_call(
        _f1_kernel,
        grid=(B // FFT_SUB, 2),
        in_specs=[
            pl.BlockSpec((A, FFT_SUB, 512), lambda b, c: (0, b, cb0 + c)),
            pl.BlockSpec((FFT_SUB, 2 * A, A), lambda b, c: (b, 0, 0)),
            pl.BlockSpec((512, 512), lambda b, c: (0, 0)),
        ],
        out_specs=[pl.BlockSpec((FFT_SUB, A, 512), lambda b, c: (b, 0, c))] * 2,
        out_shape=[jax.ShapeDtypeStruct((B, A, FOURIER_WIDTH), F32)] * 2,
        compiler_params=_cparams(("parallel", "parallel")),
    )(z.reshape(A, B, NC), m1, ch)
    blk = pl.BlockSpec((B, FFT_SUB, FOURIER_WIDTH), lambda i: (0, i, 0))
    out = pl.pallas_call(
        _f2_kernel,
        grid=(A // FFT_SUB,),
        in_specs=[
            blk, blk,
            pl.BlockSpec((B, 2 * B), lambda i: (0, 0)),
            pl.BlockSpec((FOURIER_HEADS, FOURIER_DIM, FOURIER_DIM), lambda i: (0, 0, 0)),
            pl.BlockSpec((FOURIER_HEADS, 1, FOURIER_DIM), lambda i: (0, 0, 0)),
        ],
        out_specs=blk,
        out_shape=jax.ShapeDtypeStruct((B, A, FOURIER_WIDTH), F32),
        compiler_params=_cparams(("parallel",)),
    )(yre, yim, m2, fw, fb.reshape(FOURIER_HEADS, 1, FOURIER_DIM))
    return out.reshape(T, FOURIER_WIDTH)


def _fctx_kernel(f_ref, cs_ref, ts_ref, w_ref, b_ref, o_ref):
    for h in range(FOURIER_HEADS):
        sl = slice(h * FOURIER_DIM, (h + 1) * FOURIER_DIM)
        g = _dot(f_ref[:, sl].astype(BF16), cs_ref[...])
        gg = jnp.concatenate([g[:, :FOURIER_DIM], g[:, FOURIER_DIM:]], axis=0).astype(BF16)
        spec = _dot(ts_ref[...], gg)
        out = _dot(spec.astype(BF16), w_ref[h].astype(BF16)) + b_ref[h]
        o_ref[:, sl] = out.astype(o_ref.dtype)


def fourier_mix_ctx(zc, fw, fb):
    T = zc.shape[0]
    cc, ss = _channel_dft()
    cs = jnp.asarray(np.concatenate([cc, ss], axis=1), BF16)
    t = np.arange(T)
    ang = 2.0 * np.pi * ((t[:, None] * t[None, :]) % T) / T
    st = 1.0 / math.sqrt(T)
    ts = jnp.asarray(np.concatenate([np.cos(ang) * st, -np.sin(ang) * st], axis=1), BF16)
    return pl.pallas_call(
        _fctx_kernel,
        grid=(1,),
        in_specs=[
            pl.BlockSpec((T, FOURIER_WIDTH), lambda i: (0, COL_F // FOURIER_WIDTH)),
            pl.BlockSpec((FOURIER_DIM, 2 * FOURIER_DIM), lambda i: (0, 0)),
            pl.BlockSpec((T, 2 * T), lambda i: (0, 0)),
            pl.BlockSpec((FOURIER_HEADS, FOURIER_DIM, FOURIER_DIM), lambda i: (0, 0, 0)),
            pl.BlockSpec((FOURIER_HEADS, 1, FOURIER_DIM), lambda i: (0, 0, 0)),
        ],
        out_specs=pl.BlockSpec((T, FOURIER_WIDTH), lambda i: (0, 0)),
        out_shape=jax.ShapeDtypeStruct((T, FOURIER_WIDTH), BF16),
        compiler_params=_cparams(("arbitrary",)),
    )(zc, cs, ts, fw, fb.reshape(FOURIER_HEADS, 1, FOURIER_DIM))


def _rope_tables(T):
    nf = NA_HEAD_DIM // 4
    t = jnp.arange(T)
    inv = 1.0 / (ROPE_THETA ** (jnp.arange(nf, dtype=F32) / nf))
    lane = jnp.arange(NA_HEAD_DIM)
    pos = jnp.where(lane[None, :] < NA_HEAD_DIM // 2, (t // GRID_W)[:, None], (t % GRID_W)[:, None]).astype(F32)
    ang = pos * inv[lane % nf][None, :]
    sign = jnp.where((lane % (2 * nf)) < nf, -1.0, 1.0)[None, :]
    return jnp.cos(ang), jnp.sin(ang) * sign


def _head_norm_rope(x, w, cos, sin):
    ms = jnp.mean(x * x, axis=-1, keepdims=True)
    y = x * lax.rsqrt(ms + NORM_EPS) * w
    if cos is None:
        return y
    lane = lax.broadcasted_iota(jnp.int32, y.shape, 1)
    swap = jnp.where((lane % 64) < 32, pltpu.roll(y, 96, axis=1), pltpu.roll(y, 32, axis=1))
    return y * cos + swap * sin


def _qkprep_kernel(q_ref, k_ref, qw_ref, kw_ref, cos_ref, sin_ref, qo_ref, ko_ref, *, rope):
    cos = cos_ref[...] if rope else None
    sin = sin_ref[...] if rope else None
    qw = qw_ref[...] * (NA_HEAD_DIM ** -0.5)
    kw = kw_ref[...]
    for h in range(4):
        sl = slice(h * LANE, (h + 1) * LANE)
        qo_ref[:, sl] = _head_norm_rope(q_ref[:, sl], qw, cos, sin).astype(qo_ref.dtype)
        ko_ref[:, sl] = _head_norm_rope(k_ref[:, sl], kw, cos, sin).astype(ko_ref.dtype)


def qk_prep(z, qw, kw, rope):
    T = z.shape[0]
    tm = min(512, T)
    if rope:
        cos, sin = _rope_tables(T)
    else:
        cos = sin = jnp.zeros((T, LANE), F32)
    nq = NA_WIDTH // 512
    return pl.pallas_call(
        functools.partial(_qkprep_kernel, rope=rope),
        grid=(T // tm, nq),
        in_specs=[
            pl.BlockSpec((tm, 512), lambda i, j: (i, COL_Q // 512 + j)),
            pl.BlockSpec((tm, 512), lambda i, j: (i, COL_K // 512 + j)),
            pl.BlockSpec((1, LANE), lambda i, j: (0, 0)),
            pl.BlockSpec((1, LANE), lambda i, j: (0, 0)),
            pl.BlockSpec((tm, LANE), lambda i, j: (i, 0)),
            pl.BlockSpec((tm, LANE), lambda i, j: (i, 0)),
        ],
        out_specs=[pl.BlockSpec((tm, 512), lambda i, j: (i, j))] * 2,
        out_shape=[jax.ShapeDtypeStruct((T, NA_WIDTH), BF16)] * 2,
        compiler_params=_cparams(("parallel", "parallel")),
    )(z, z, qw.reshape(1, LANE), kw.reshape(1, LANE), cos, sin)


def _toeplitz_kernel(r_ref, e_ref, o_ref):
    r = r_ref[0]
    acc = jnp.zeros(o_ref.shape[1:], F32)
    for d in range(2 * NA_KW - 1):
        acc = acc + r[:, d:d + 1] * e_ref[d:d + 1, :]
    o_ref[0] = acc


def rpb_bias_tables(rpb):
    L, H = rpb.shape[0], rpb.shape[1]
    ndr, ndc = 2 * NA_KH - 1, 2 * NA_KW - 1
    q = np.arange(GRID_W)
    e = np.zeros((32, GRID_W, GRID_W), np.float32)
    for d in range(ndc):
        e[d] = (q[None, :] - q[:, None] + (NA_KW - 1)) == d
    e = jnp.asarray(e.reshape(32, GRID_W * GRID_W))
    rp = jnp.pad(rpb.reshape(L * H, ndr, ndc), ((0, 0), (0, 16 - ndr), (0, 32 - ndc)))
    toep = pl.pallas_call(
        _toeplitz_kernel,
        grid=(L * H,),
        in_specs=[pl.BlockSpec((1, 16, 32), lambda i: (i, 0, 0)),
                  pl.BlockSpec((32, GRID_W * GRID_W), lambda i: (0, 0))],
        out_specs=pl.BlockSpec((1, 16, GRID_W * GRID_W), lambda i: (i, 0, 0)),
        out_shape=jax.ShapeDtypeStruct((L * H, 16, GRID_W * GRID_W), F32),
        compiler_params=_cparams(("parallel",)),
    )(rp, e)
    toep = toep.reshape(L, H, 16, GRID_W, GRID_W)
    col_start = np.clip(q - NA_KW // 2, 0, GRID_W - NA_KW)
    in_win = (q[None, :] >= col_start[:, None]) & (q[None, :] < col_start[:, None] + NA_KW)
    mask = jnp.asarray(np.where(in_win, 0.0, -1e30).astype(np.float32))
    tabs = []
    for o in range(NA_KH):
        band = toep[:, :, NA_KH - 1 - o:2 * NA_KH - 1 - o]
        band = band + mask[None, None, None]
        tabs.append(jnp.transpose(band, (0, 1, 3, 2, 4)).reshape(L, H, GRID_W, NA_KH * GRID_W))
    return jnp.stack(tabs, axis=2)


NA_HEADS_PER_STEP = 2


def _na_kernel(q_ref, kp_ref, kc_ref, kn_ref, vp_ref, vc_ref, vn_ref, kx_ref, vx_ref, bt_ref, o_ref,
               kbuf, vbuf, *, nrows):
    m = pl.program_id(1)
    blk = NA_KH * GRID_W
    kbuf[0:blk] = kp_ref[...]
    kbuf[blk:2 * blk] = kc_ref[...]
    kbuf[2 * blk:3 * blk] = kn_ref[...]
    vbuf[0:blk] = vp_ref[...].astype(BF16)
    vbuf[blk:2 * blk] = vc_ref[...].astype(BF16)
    vbuf[2 * blk:3 * blk] = vn_ref[...].astype(BF16)
    kctx = kx_ref[...]
    vctx = vx_ref[...].astype(BF16)
    work, scores = [], []
    for hh in range(NA_HEADS_PER_STEP):
        hs = slice(hh * LANE, (hh + 1) * LANE)
        for j in range(NA_KH):
            r = m * NA_KH + j
            rs = jnp.clip(r - NA_KH // 2, 0, nrows - NA_KH)
            start = pl.multiple_of((rs - (m - 1) * NA_KH) * GRID_W, GRID_W)
            q = q_ref[j * GRID_W:(j + 1) * GRID_W, hs]
            s = _dot_nt(q, kbuf[pl.ds(start, blk), hs]) + bt_ref[hh, r - rs]
            work.append((hs, j, start))
            scores.append((s, _dot_nt(q, kctx[:, hs])))
    probs = []
    for s, sc in scores:
        mx = jnp.maximum(jnp.max(s, axis=-1, keepdims=True), jnp.max(sc, axis=-1, keepdims=True))
        p = jnp.exp(s - mx)
        pc = jnp.exp(sc - mx)
        den = jnp.sum(p, axis=-1, keepdims=True) + jnp.sum(pc, axis=-1, keepdims=True)
        probs.append((p.astype(BF16), pc.astype(BF16), den))
    for (hs, j, start), (p, pc, den) in zip(work, probs):
        acc = _dot(p, vbuf[pl.ds(start, blk), hs]) + _dot(pc, vctx[:, hs])
        o_ref[j * GRID_W:(j + 1) * GRID_W, hs] = (acc / den).astype(o_ref.dtype)


def na_attention(qn, kn, z, kcn, zc, bias_tab):
    T = qn.shape[0]
    C = kcn.shape[0]
    nrows = T // GRID_W
    blk = NA_KH * GRID_W
    nblk = T // blk
    hw = NA_HEADS_PER_STEP * LANE
    vcol = COL_V // hw
    prev = lambda h, m: (jnp.maximum(m - 1, 0), h)
    cur = lambda h, m: (m, h)
    nxt = lambda h, m: (jnp.minimum(m + 1, nblk - 1), h)
    vprev = lambda h, m: (jnp.maximum(m - 1, 0), vcol + h)
    vcur = lambda h, m: (m, vcol + h)
    vnxt = lambda h, m: (jnp.minimum(m + 1, nblk - 1), vcol + h)
    return pl.pallas_call(
        functools.partial(_na_kernel, nrows=nrows),
        grid=(NA_HEADS // NA_HEADS_PER_STEP, nblk),
        in_specs=[
            pl.BlockSpec((blk, hw), cur),
            pl.BlockSpec((blk, hw), prev), pl.BlockSpec((blk, hw), cur), pl.BlockSpec((blk, hw), nxt),
            pl.BlockSpec((blk, hw), vprev), pl.BlockSpec((blk, hw), vcur), pl.BlockSpec((blk, hw), vnxt),
            pl.BlockSpec((C, hw), lambda h, m: (0, h)),
            pl.BlockSpec((C, hw), lambda h, m: (0, vcol + h)),
            pl.BlockSpec((NA_HEADS_PER_STEP, NA_KH, GRID_W, blk), lambda h, m: (h, 0, 0, 0)),
        ],
        out_specs=pl.BlockSpec((blk, hw), cur),
        out_shape=jax.ShapeDtypeStruct((T, NA_WIDTH), BF16),
        scratch_shapes=[pltpu.VMEM((3 * blk, hw), BF16), pltpu.VMEM((3 * blk, hw), BF16)],
        compiler_params=_cparams(("parallel", "parallel")),
    )(qn, kn, kn, kn, z, z, z, kcn, zc, bias_tab)


def _ctxattn_kernel(q_ref, k_ref, v_ref, o_ref):
    s = _dot_nt(q_ref[...], k_ref[...])
    p = jnp.exp(s - jnp.max(s, axis=-1, keepdims=True))
    den = jnp.sum(p, axis=-1, keepdims=True)
    o_ref[...] = (_dot(p.astype(BF16), v_ref[...].astype(BF16)) / den).astype(o_ref.dtype)


def ctx_attention(qcn, kcn, zc):
    C = qcn.shape[0]
    vcol = COL_V // LANE
    return pl.pallas_call(
        _ctxattn_kernel,
        grid=(NA_HEADS,),
        in_specs=[pl.BlockSpec((C, LANE), lambda h: (0, h)), pl.BlockSpec((C, LANE), lambda h: (0, h)),
                  pl.BlockSpec((C, LANE), lambda h: (0, vcol + h))],
        out_specs=pl.BlockSpec((C, LANE), lambda h: (0, h)),
        out_shape=jax.ShapeDtypeStruct((C, NA_WIDTH), BF16),
        compiler_params=_cparams(("parallel",)),
    )(qcn, kcn, zc)


def _seg_sum64(x):
    lane = lax.broadcasted_iota(jnp.int32, x.shape, 1)
    low = lane < RWKV_HEAD_DIM
    s_lo = jnp.sum(jnp.where(low, x, 0.0), axis=-1, keepdims=True)
    s_hi = jnp.sum(jnp.where(low, 0.0, x), axis=-1, keepdims=True)
    return jnp.where(low, s_lo, s_hi)


def _seg_sum(x):
    return jnp.concatenate([_seg_sum64(x[:, i * LANE:(i + 1) * LANE]) for i in range(x.shape[1] // LANE)], axis=1)


def _rwkvprep_kernel(r_ref, rp_ref, rn_ref, k_ref, kp_ref, kn_ref, v_ref, vp_ref, vn_ref, lo_ref, cw_ref,
                     w2_ref, a2_ref, g2_ref, w0_ref, a0_ref, kk_ref, ka_ref, rk_ref,
                     r_o, v_o, kk_o, g_o, bonus_o, logw_o, kd_o, ag_o):
    i = pl.program_id(0)
    n_i = pl.num_programs(0)
    W = RWKV_WIDTH
    cw = cw_ref[...]
    r = _conv3(r_ref[...], *_halo_rows(rp_ref, rn_ref, i, n_i), cw[:, 0:W])
    k = _conv3(k_ref[...], *_halo_rows(kp_ref, kn_ref, i, n_i), cw[:, W:2 * W])
    v = _conv3(v_ref[...], *_halo_rows(vp_ref, vn_ref, i, n_i), cw[:, 2 * W:3 * W])
    lora = lo_ref[...]
    wl = _dot(jnp.tanh(lora[:, 0:2 * W_LORA]).astype(BF16), w2_ref[...])
    al = _dot(lora[:, 2 * W_LORA:2 * (W_LORA + A_LORA)].astype(BF16), a2_ref[...])
    g = _dot(jax.nn.sigmoid(lora[:, 2 * (W_LORA + A_LORA):]).astype(BF16), g2_ref[...])
    kkr = k * kk_ref[...]
    kk = kkr / jnp.maximum(jnp.sqrt(_seg_sum(kkr * kkr)), 1e-12)
    kds = []
    for d in range(2):
        logw = -math.exp(-0.5) * jax.nn.sigmoid(w0_ref[d:d + 1, :] + wl[:, d * W:(d + 1) * W])
        a = jax.nn.sigmoid(a0_ref[d:d + 1, :] + al[:, d * W:(d + 1) * W])
        kd = k * (1.0 + (a - 1.0) * ka_ref[...])
        kds.append(kd)
        for gi in range(N_GROUPS):
            sl = slice(gi * GROUP, (gi + 1) * GROUP)
            logw_o[d, gi] = logw[:, sl]
            kd_o[d, gi] = kd[:, sl].astype(kd_o.dtype)
            ag_o[d, gi] = a[:, sl].astype(ag_o.dtype)
    bonus = _seg_sum(r * (kds[0] + kds[1]) * rk_ref[...]) * v
    for gi in range(N_GROUPS):
        sl = slice(gi * GROUP, (gi + 1) * GROUP)
        r_o[gi] = r[:, sl].astype(r_o.dtype)
        v_o[gi] = v[:, sl].astype(v_o.dtype)
        kk_o[gi] = kk[:, sl].astype(kk_o.dtype)
        g_o[gi] = g[:, sl].astype(g_o.dtype)
        bonus_o[gi] = bonus[:, sl].astype(bonus_o.dtype)


def rwkv_prep(z, p):
    T = z.shape[0]
    tm = min(128, T)
    W = RWKV_WIDTH
    cb = COL_R // W
    specs = []
    for c in range(3):
        specs += _halo_specs(tm, W, T, lambda j, c=c: cb + c)
    specs = [pl.BlockSpec(s.block_shape, lambda i, f=s.index_map: f(i, 0)) for s in specs]
    full = lambda shape: pl.BlockSpec(shape, lambda i: (0,) * len(shape))
    specs += [
        pl.BlockSpec((tm, LORA_PAD), lambda i: (i, COL_L // LORA_PAD)),
        full((3, 3 * W)), full((2 * W_LORA, 2 * W)), full((2 * A_LORA, 2 * W)), full((LORA_PAD - 256, W)),
        full((2, W)), full((2, W)), full((1, W)), full((1, W)), full((1, W)),
    ]
    g1 = pl.BlockSpec((N_GROUPS, tm, GROUP), lambda i: (0, i, 0))
    g2 = pl.BlockSpec((2, N_GROUPS, tm, GROUP), lambda i: (0, 0, i, 0))
    s1 = jax.ShapeDtypeStruct((N_GROUPS, T, GROUP), BF16)
    s2 = jax.ShapeDtypeStruct((2, N_GROUPS, T, GROUP), BF16)
    s2f = jax.ShapeDtypeStruct((2, N_GROUPS, T, GROUP), F32)
    return pl.pallas_call(
        _rwkvprep_kernel,
        grid=(T // tm,),
        in_specs=specs,
        out_specs=[g1] * 5 + [g2] * 3,
        out_shape=[s1] * 5 + [s2f, s2, s2],
        compiler_params=_cparams(("parallel",)),
    )(z, z, z, z, z, z, z, z, z, z, p['rwkv_conv'], p['w2bd'], p['a2bd'], p['g2p'], p['w0'], p['a0'],
      p['k_k'].reshape(1, W), p['k_a'].reshape(1, W), p['r_k'].reshape(1, W))


def _fold(x):
    c = CHUNK
    return x[0:c] + x[c:2 * c] + x[2 * c:3 * c] + x[3 * c:4 * c]


def _rwkv_kernel(r_ref, v_ref, kk_ref, logw_ref, kd_ref, ag_ref, tri_ref, ms_ref, mi_ref, bd_ref, bdf_ref, z0_ref,
                 y_ref, zf_ref, z_scr):
    forward = pl.program_id(0) == 0
    c = pl.program_id(1)

    @pl.when(c == 0)
    def _():
        z_scr[...] = z0_ref[0]

    tri = tri_ref[0]
    m_strict = ms_ref[0]
    m_incl = mi_ref[0]
    bd = bd_ref[...]
    bdf = bdf_ref[...]
    row = lax.broadcasted_iota(jnp.int32, (CHUNK, GROUP), 0)
    col = lax.broadcasted_iota(jnp.int32, (CHUNK, GROUP), 1)
    diag = row == (col % RWKV_HEAD_DIM)
    eye_f = jnp.where(diag, 1.0, 0.0)

    def expand(x):
        return jnp.concatenate([x.astype(BF16)] * 4, axis=0) * bd

    def stack(*xs):
        return jnp.concatenate([x.astype(BF16) for x in xs], axis=0)

    def prep(g):
        lw = logw_ref[0, g]
        hi, lo = _split(lw)
        cs = _dot(tri, jnp.concatenate([hi, lo], axis=1))
        linc = cs[:, :GROUP] + cs[:, GROUP:]
        ltot = jnp.where(forward, linc[CHUNK - 1:CHUNK, :], linc[0:1, :])
        e_inc = jnp.exp(linc)
        e_neg = jnp.exp(-linc)
        e_exc = jnp.exp(linc - lw)
        e_rem = jnp.exp(ltot - linc)
        kk = kk_ref[g].astype(F32)
        kd = kd_ref[0, g].astype(F32)
        b = kk * ag_ref[0, g].astype(F32)
        at = -kk * e_exc
        rt = r_ref[g].astype(F32) * e_inc
        gram = _dot_nt(stack(at, rt), jnp.concatenate([expand(b * e_neg), expand(kd * e_neg)], axis=0))
        fab = gram[:CHUNK, :GROUP] * m_strict
        return dict(g=g, rt=rt, at=at, e_tot=jnp.exp(ltot), bh=b * e_rem, kh=kd * e_rem, v=v_ref[g],
                    fp=fab, ft=eye_f + fab,
                    fak=gram[:CHUNK, GROUP:] * m_strict, frb=gram[CHUNK:, :GROUP] * m_incl,
                    frk=gram[CHUNK:, GROUP:] * m_incl)

    st = [prep(g) for g in range(N_GROUPS)]
    for s in st:
        s['fp'] = _dot(s['fp'].astype(BF16), expand(s['fp']))
    for _ in range(4):
        for s in st:
            res = _dot(stack(s['fp'], s['ft']), expand(s['fp']))
            s['fp'] = res[:CHUNK]
            s['ft'] = s['ft'] + res[CHUNK:]
    for s in st:
        s['ft'] = s['ft'] + _dot(s['ft'].astype(BF16), expand(s['fp']))
    for s in st:
        s['fg'] = _dot(s['frb'].astype(BF16), expand(s['ft']))
    for s in st:
        res = _dot(stack(s['ft'], s['fg']), jnp.concatenate([expand(s['fak']), expand(s['at'])], axis=1))
        s['fta'] = res[:CHUNK, :GROUP]
        s['ff'] = res[CHUNK:, :GROUP] + s['frk']
        s['fa1'] = res[:CHUNK, GROUP:]
        s['rp'] = s['rt'] + res[CHUNK:, GROUP:]
    for s in st:
        res = _dot(stack(s['fta'], s['ff']), expand(s['v']))
        s['fu0'] = res[:CHUNK]
        s['y0'] = res[CHUNK:]
    for s in st:
        mt = _dot_tn(s['bh'].astype(BF16), s['fa1'].astype(BF16))
        nt = _dot_tn(stack(s['bh'], s['kh']), stack(s['fu0'], s['v']))
        s['mf'] = _fold(mt * bdf) + jnp.where(diag, s['e_tot'], 0.0)
        s['nf'] = _fold(nt * bdf)
    for s in st:
        g = s['g']
        zhx = expand(z_scr[g])
        mh, ml = _split(s['mf'])
        res = _dot(jnp.concatenate([s['rp'].astype(BF16), mh, ml], axis=0), zhx)
        y_ref[0, g] = res[:CHUNK] + s['y0']
        z_scr[g] = res[CHUNK:2 * CHUNK] + res[2 * CHUNK:] + s['nf']

    @pl.when(c == pl.num_programs(1) - 1)
    def _():
        zf_ref[0] = z_scr[...]


def _chunk_masks():
    i = np.arange(CHUNK)
    tri = np.stack([i[None, :] <= i[:, None], i[None, :] >= i[:, None]]).astype(np.float32)
    j = np.arange(GROUP)
    same = ((j[:, None] // CHUNK) == (j[None, :] // CHUNK)).astype(np.float32)
    js = j[None, :] % CHUNK
    strict = np.stack([js < i[:, None], js > i[:, None]]).astype(np.float32)
    incl = np.stack([js <= i[:, None], js >= i[:, None]]).astype(np.float32)
    return jnp.asarray(tri, BF16), jnp.asarray(strict), jnp.asarray(incl), jnp.asarray(same, BF16), jnp.asarray(same)


def rwkv_scan(r, v, kk, logw, kd, ag, z0):
    T = r.shape[1]
    nc = T // CHUNK
    tri, strict, incl, bd, bdf = _chunk_masks()
    order = lambda d, c: c + d * (nc - 1 - 2 * c)
    b1 = pl.BlockSpec((N_GROUPS, CHUNK, GROUP), lambda d, c: (0, order(d, c), 0))
    b2 = pl.BlockSpec((1, N_GROUPS, CHUNK, GROUP), lambda d, c: (d, 0, order(d, c), 0))
    per_dir = lambda shape: pl.BlockSpec((1,) + shape, lambda d, c: (d,) + (0,) * len(shape))
    full = lambda shape: pl.BlockSpec(shape, lambda d, c: (0,) * len(shape))
    zspec = per_dir((N_GROUPS, CHUNK, GROUP))
    return pl.pallas_call(
        _rwkv_kernel,
        grid=(2, nc),
        in_specs=[b1, b1, b1, b2, b2, b2, per_dir((CHUNK, CHUNK)), per_dir((CHUNK, GROUP)), per_dir((CHUNK, GROUP)),
                  full((GROUP, GROUP)), full((GROUP, GROUP)), zspec],
        out_specs=[b2, zspec],
        out_shape=[jax.ShapeDtypeStruct((2, N_GROUPS, T, GROUP), F32),
                   jax.ShapeDtypeStruct((2, N_GROUPS, CHUNK, GROUP), F32)],
        scratch_shapes=[pltpu.VMEM((N_GROUPS, CHUNK, GROUP), F32)],
        compiler_params=_cparams(("arbitrary", "arbitrary")),
    )(r, v, kk, logw, kd, ag, tri, strict, incl, bd, bdf, z0)


def _rwkvpost_kernel(y_ref, g_ref, bonus_ref, gw_ref, gb_ref, o_ref):
    for gi in range(N_GROUPS):
        y = y_ref[0, gi] + y_ref[1, gi]
        mu = _seg_sum(y) * (1.0 / RWKV_HEAD_DIM)
        yc = y - mu
        var = _seg_sum(yc * yc) * (1.0 / RWKV_HEAD_DIM)
        sl = slice(gi * GROUP, (gi + 1) * GROUP)
        yn = yc * lax.rsqrt(var + GN_EPS) * gw_ref[:, sl] + gb_ref[:, sl]
        o_ref[:, sl] = ((yn + bonus_ref[gi].astype(F32)) * g_ref[gi].astype(F32)).astype(o_ref.dtype)


def rwkv_post(y, g, bonus, gn_w, gn_b):
    T = y.shape[2]
    tm = min(256, T)
    W = RWKV_WIDTH
    return pl.pallas_call(
        _rwkvpost_kernel,
        grid=(T // tm,),
        in_specs=[pl.BlockSpec((2, N_GROUPS, tm, GROUP), lambda i: (0, 0, i, 0)),
                  pl.BlockSpec((N_GROUPS, tm, GROUP), lambda i: (0, i, 0)),
                  pl.BlockSpec((N_GROUPS, tm, GROUP), lambda i: (0, i, 0)),
                  pl.BlockSpec((1, W), lambda i: (0, 0)), pl.BlockSpec((1, W), lambda i: (0, 0))],
        out_specs=pl.BlockSpec((tm, W), lambda i: (i, 0)),
        out_shape=jax.ShapeDtypeStruct((T, W), BF16),
        compiler_params=_cparams(("parallel",)),
    )(y, g, bonus, gn_w.reshape(1, W), gn_b.reshape(1, W))


def rwkv_mix(z, p, z0):
    r, v, kk, g, bonus, logw, kd, ag = rwkv_prep(z, p)
    y, zf = rwkv_scan(r, v, kk, logw, kd, ag, z0)
    return rwkv_post(y, g, bonus, p['gn_w'], p['gn_b']), zf


def _lora_heads(l, w2, a2, g2):
    W = RWKV_WIDTH
    zw = jnp.zeros((W_LORA, W), F32)
    w2bd = jnp.concatenate([jnp.concatenate([w2[l, 0], zw], axis=1), jnp.concatenate([zw, w2[l, 1]], axis=1)], axis=0)
    a2bd = jnp.concatenate([jnp.concatenate([a2[l, 0], zw], axis=1), jnp.concatenate([zw, a2[l, 1]], axis=1)], axis=0)
    g2p = jnp.pad(g2[l], ((0, LORA_PAD - 256 - G_LORA), (0, 0)))
    return dict(w2bd=w2bd.astype(BF16), a2bd=a2bd.astype(BF16), g2p=g2p.astype(BF16))


def _layer(x, ctx, mod_x, mod_c, p, l, bias_tab, ctx_out):
    Dm = D_MODEL
    sh1, sc1, gt1, sh2, sc2, gt2 = [mod_x[i * Dm:(i + 1) * Dm] for i in range(6)]
    csh1, csc1, cgt1, csh2, csc2, cgt2 = [mod_c[i * Dm:(i + 1) * Dm] for i in range(6)]
    zx = norm_mod_matmul(x, p['norm1_w'], sc1, sh1, p['w_in'], l)
    zc = norm_mod_matmul(ctx, p['norm1_w'], csc1, csh1, p['w_in'], l)
    fx = fourier_mix(zx, p['fourier_w'], p['fourier_b'])
    qx, kx = qk_prep(zx, p['q_norm_w'], p['k_norm_w'], rope=True)
    qc, kc = qk_prep(zc, p['q_norm_w'], p['k_norm_w'], rope=False)
    ax = na_attention(qx, kx, zx, kc, zc, bias_tab)
    z0 = jnp.zeros((2, N_GROUPS, CHUNK, GROUP), F32)
    rc, zf = rwkv_mix(zc, p, z0)
    rx, _ = rwkv_mix(zx, p, zf)
    x = matmul_residual([fx, ax, rx], p['w_out'], l, x, gt1)
    u = norm_mod_matmul(x, p['norm2_w'], sc2, sh2, p['w_ffn_in'], l, out_dtype=BF16)
    x = matmul_residual([conv_gate(u, p['ffn_conv'])], p['w_ffn_out'], l, x, gt2)
    if ctx_out:
        fc = fourier_mix_ctx(zc, p['fourier_w'], p['fourier_b'])
        ac = ctx_attention(qc, kc, zc)
        ctx = matmul_residual([fc, ac, rc], p['w_out'], l, ctx, cgt1)
        uc = norm_mod_matmul(ctx, p['norm2_w'], csc2, csh2, p['w_ffn_in'], l, out_dtype=BF16)
        ctx = matmul_residual([conv_gate(uc, p['ffn_conv'])], p['w_ffn_out'], l, ctx, cgt2)
    return x, ctx


def kernel(x, c, ctx, c_ctx, ada_w, ada_b, norm1_w, norm2_w, w_in, fourier_w, fourier_b, q_norm_w, k_norm_w, rpb,
           rwkv_conv, w0, w2, a0, a2, g2, k_k, k_a, r_k, gn_w, gn_b, w_out, ffn_conv, w_ffn_in, w_ffn_out):
    L = ada_w.shape[0]
    xs = x[0]
    cs = ctx[0]
    mods = ada_mod(jnp.concatenate([c, c_ctx[None, :]], axis=0), ada_w, ada_b)
    bias_tabs = rpb_bias_tables(rpb)
    big = dict(w_in=cast_w_in(w_in), w_out=cast_bf16(w_out), w_ffn_in=cast_bf16(w_ffn_in), w_ffn_out=cast_bf16(w_ffn_out))
    for l in range(L):
        p = dict(big)
        p.update(_lora_heads(l, w2, a2, g2))
        p.update(norm1_w=norm1_w[l], norm2_w=norm2_w[l], fourier_w=fourier_w[l], fourier_b=fourier_b[l],
                 q_norm_w=q_norm_w[l], k_norm_w=k_norm_w[l], rwkv_conv=rwkv_conv[l], w0=w0[l], a0=a0[l],
                 k_k=k_k[l], k_a=k_a[l], r_k=r_k[l], gn_w=gn_w[l], gn_b=gn_b[l], ffn_conv=ffn_conv[l])
        xs, cs = _layer(xs, cs, mods[l, 0], mods[l, 1], p, l, bias_tabs[l], l < L - 1)
    return xs[None]
```

```python
import functools
import math

import jax
import jax.numpy as jnp
import numpy as np
from jax import lax
from jax.experimental import pallas as pl
from jax.experimental.pallas import tpu as pltpu

F32 = jnp.float32
BF16 = jnp.bfloat16

D_MODEL = 4096
GRID_W = 64
FOURIER_WIDTH = 1024
FOURIER_HEADS = 4
FOURIER_DIM = 256
NA_WIDTH = 1536
NA_HEAD_DIM = 128
NA_HEADS = 12
NA_KH = 8
NA_KW = 16
RWKV_WIDTH = 1536
RWKV_HEAD_DIM = 64
W_LORA = 64
A_LORA = 64
G_LORA = 224
D_FF = 5120
ROPE_THETA = 10000.0
NORM_EPS = 1e-6
GN_EPS = 64e-5

COL_Q = 0
COL_K = NA_WIDTH
COL_V = 2 * NA_WIDTH
COL_R = 3 * NA_WIDTH
COL_F = 6 * NA_WIDTH
COL_L = COL_F + FOURIER_WIDTH
LORA_PAD = 512
IN_COLS_PAD = COL_L + LORA_PAD

CHUNK = 64
GROUP = 256
N_GROUPS = RWKV_WIDTH // GROUP
FFT_B = 128

LANE = 128
VMEM_LIMIT = 48 * 1024 * 1024


def _cparams(sem):
    return pltpu.CompilerParams(dimension_semantics=sem, vmem_limit_bytes=VMEM_LIMIT)


def _dot(a, b):
    return jnp.dot(a, b, preferred_element_type=F32)


def _dot_nt(a, b):
    return lax.dot_general(a, b, (((1,), (1,)), ((), ())), preferred_element_type=F32)


def _dot_tn(a, b):
    return lax.dot_general(a, b, (((0,), (0,)), ((), ())), preferred_element_type=F32)


def _split(x):
    hi = x.astype(BF16)
    lo = (x - hi.astype(F32)).astype(BF16)
    return hi, lo


def _ada_kernel(s_ref, w_ref, b_ref, o_ref, acc_ref):
    k = pl.program_id(2)
    tk, tn = w_ref.shape[1], w_ref.shape[2]
    rep = tn // LANE

    @pl.when(k == 0)
    def _():
        acc_ref[...] = jnp.zeros_like(acc_ref)

    def body(i, carry):
        a0, a1 = carry
        r = pl.multiple_of(i * 8, 8)
        w = w_ref[0, pl.ds(r, 8), :]
        s0 = s_ref[0, pl.ds(r, 8), :]
        s1 = s_ref[1, pl.ds(r, 8), :]
        s0 = s0 * jax.nn.sigmoid(s0)
        s1 = s1 * jax.nn.sigmoid(s1)
        a0 = a0 + w * jnp.concatenate([s0] * rep, axis=1)
        a1 = a1 + w * jnp.concatenate([s1] * rep, axis=1)
        return a0, a1

    a0, a1 = lax.fori_loop(0, tk // 8, body, (acc_ref[0], acc_ref[1]), unroll=4)
    acc_ref[0] = a0
    acc_ref[1] = a1

    @pl.when(k == pl.num_programs(2) - 1)
    def _():
        o_ref[0, 0:1, :] = jnp.sum(a0, axis=0, keepdims=True) + b_ref[0]
        o_ref[0, 1:2, :] = jnp.sum(a1, axis=0, keepdims=True) + b_ref[0]


def ada_mod(cc, ada_w, ada_b):
    L, K, N = ada_w.shape
    tk, tn = 2048, 1024
    s_b = jnp.broadcast_to(cc[:, :, None], (2, K, LANE))
    return pl.pallas_call(
        _ada_kernel,
        grid=(L, N // tn, K // tk),
        in_specs=[
            pl.BlockSpec((2, tk, LANE), lambda l, j, k: (0, k, 0)),
            pl.BlockSpec((1, tk, tn), lambda l, j, k: (l, k, j)),
            pl.BlockSpec((1, 1, tn), lambda l, j, k: (l, 0, j)),
        ],
        out_specs=pl.BlockSpec((1, 2, tn), lambda l, j, k: (l, 0, j)),
        out_shape=jax.ShapeDtypeStruct((L, 2, N), F32),
        scratch_shapes=[pltpu.VMEM((2, 8, tn), F32)],
        compiler_params=_cparams(("parallel", "parallel", "arbitrary")),
    )(s_b, ada_w, ada_b.reshape(L, 1, N))


def _normmod_kernel(x_ref, nw_ref, sc_ref, sh_ref, o_ref):
    x = x_ref[...]
    ms = jnp.mean(x * x, axis=-1, keepdims=True)
    y = x * lax.rsqrt(ms + NORM_EPS) * nw_ref[...]
    o_ref[...] = (y * (1.0 + sc_ref[...]) + sh_ref[...]).astype(o_ref.dtype)


def _mm_kernel(a_ref, w_ref, o_ref):
    o_ref[...] = _dot(a_ref[...], w_ref[0]).astype(o_ref.dtype)


def _cast_kernel(x_ref, o_ref):
    o_ref[...] = x_ref[...].astype(o_ref.dtype)


def cast_bf16(w):
    L, K, N = w.shape
    tk, tn = 512, 2048
    spec = pl.BlockSpec((1, tk, tn), lambda l, i, j: (l, i, j))
    return pl.pallas_call(
        _cast_kernel,
        grid=(L, K // tk, N // tn),
        in_specs=[spec],
        out_specs=spec,
        out_shape=jax.ShapeDtypeStruct((L, K, N), BF16),
        compiler_params=_cparams(("parallel", "parallel", "parallel")),
    )(w)


LORA_COLS = 2 * W_LORA + 2 * A_LORA + G_LORA
W_IN_BLOCKS = IN_COLS_PAD // 512


def _cast_win_kernel(x_ref, o_ref):
    row = lax.broadcasted_iota(jnp.int32, x_ref.shape[1:], 0)
    keep = (pl.program_id(2) < W_IN_BLOCKS - 1) | (row < LORA_COLS)
    o_ref[0] = jnp.where(keep, x_ref[0], 0.0).T.astype(BF16)


def cast_w_in(w_in):
    L, K, _ = w_in.shape
    tk = 1024
    n_front = (FOURIER_WIDTH) // 512
    n_mid = W_IN_BLOCKS - 1 - n_front

    def src(j):
        return jnp.where(j < n_mid, j + n_front, jnp.where(j < W_IN_BLOCKS - 1, j - n_mid, W_IN_BLOCKS - 1))

    return pl.pallas_call(
        _cast_win_kernel,
        grid=(L, K // tk, W_IN_BLOCKS),
        in_specs=[pl.BlockSpec((1, 512, tk), lambda l, i, j: (l, src(j), i))],
        out_specs=pl.BlockSpec((1, tk, 512), lambda l, i, j: (l, i, j)),
        out_shape=jax.ShapeDtypeStruct((L, K, IN_COLS_PAD), BF16),
        compiler_params=_cparams(("parallel", "parallel", "parallel")),
    )(jnp.swapaxes(w_in, 1, 2))


def norm_mod_matmul(x, nw, sc, sh, w, l, out_dtype=F32):
    M, K = x.shape
    N = w.shape[2]
    tr = min(512, M)
    vec = pl.BlockSpec((1, K), lambda i: (0, 0))
    h = pl.pallas_call(
        _normmod_kernel,
        grid=(M // tr,),
        in_specs=[pl.BlockSpec((tr, K), lambda i: (i, 0)), vec, vec, vec],
        out_specs=pl.BlockSpec((tr, K), lambda i: (i, 0)),
        out_shape=jax.ShapeDtypeStruct((M, K), BF16),
        compiler_params=_cparams(("parallel",)),
    )(x, nw.reshape(1, K), sc.reshape(1, K), sh.reshape(1, K))
    tm = min(1024, M)
    tn = next(t for t in (1024, 768, 512) if N % t == 0)
    return pl.pallas_call(
        _mm_kernel,
        grid=(M // tm, N // tn),
        in_specs=[
            pl.BlockSpec((tm, K), lambda i, j: (i, 0)),
            pl.BlockSpec((1, K, tn), lambda i, j: (l, 0, j)),
        ],
        out_specs=pl.BlockSpec((tm, tn), lambda i, j: (i, j)),
        out_shape=jax.ShapeDtypeStruct((M, N), out_dtype),
        compiler_params=_cparams(("parallel", "arbitrary")),
    )(h, w)


def _mmres_kernel(*refs, ksplits):
    n = len(ksplits)
    a_refs = refs[:n]
    w_ref, x_ref, g_ref, o_ref = refs[n:]
    acc = None
    off = 0
    for a_ref, kp in zip(a_refs, ksplits):
        part = _dot(a_ref[...].astype(BF16), w_ref[0, off:off + kp, :])
        acc = part if acc is None else acc + part
        off += kp
    o_ref[...] = x_ref[...] + g_ref[...] * acc


def matmul_residual(parts, w, l, x, gate):
    M, N = x.shape
    K = w.shape[1]
    ksplits = tuple(p.shape[1] for p in parts)
    assert sum(ksplits) == K
    tm = min(1024, M)
    tn = 512
    in_specs = [pl.BlockSpec((tm, kp), lambda i, j: (i, 0)) for kp in ksplits]
    in_specs += [
        pl.BlockSpec((1, K, tn), lambda i, j: (l, 0, j)),
        pl.BlockSpec((tm, tn), lambda i, j: (i, j)),
        pl.BlockSpec((1, tn), lambda i, j: (0, j)),
    ]
    return pl.pallas_call(
        functools.partial(_mmres_kernel, ksplits=ksplits),
        grid=(M // tm, N // tn),
        in_specs=in_specs,
        out_specs=pl.BlockSpec((tm, tn), lambda i, j: (i, j)),
        out_shape=jax.ShapeDtypeStruct((M, N), F32),
        compiler_params=_cparams(("parallel", "arbitrary")),
    )(*parts, w, x, gate.reshape(1, N))


def _conv3(main, prev_row, next_row, w):
    tm = main.shape[0]
    row = lax.broadcasted_iota(jnp.int32, main.shape, 0)
    dn = jnp.where(row == 0, prev_row, pltpu.roll(main, 1, axis=0))
    up = jnp.where(row == tm - 1, next_row, pltpu.roll(main, tm - 1, axis=0))
    return dn * w[0:1, :] + main * w[1:2, :] + up * w[2:3, :]


def _halo_rows(prev_ref, next_ref, i, n_i):
    prev = prev_ref[...].astype(F32)
    nxt = next_ref[...].astype(F32)
    hr = prev.shape[0]
    prev_row = jnp.where(i == 0, 0.0, prev[hr - 1:hr, :])
    next_row = jnp.where(i == n_i - 1, 0.0, nxt[0:1, :])
    return prev_row, next_row


def _halo_specs(tm, tn, n_rows, col_fn, hr=8):
    rb = tm // hr
    last = n_rows // hr - 1
    return [
        pl.BlockSpec((tm, tn), lambda i, j: (i, col_fn(j))),
        pl.BlockSpec((hr, tn), lambda i, j: (jnp.maximum(i * rb - 1, 0), col_fn(j))),
        pl.BlockSpec((hr, tn), lambda i, j: (jnp.minimum((i + 1) * rb, last), col_fn(j))),
    ]


def _convgate_kernel(a_ref, ap_ref, an_ref, b_ref, bp_ref, bn_ref, wa_ref, wb_ref, o_ref):
    i = pl.program_id(0)
    n_i = pl.num_programs(0)
    pa, na = _halo_rows(ap_ref, an_ref, i, n_i)
    pb, nb = _halo_rows(bp_ref, bn_ref, i, n_i)
    a = _conv3(a_ref[...].astype(F32), pa, na, wa_ref[...])
    b = _conv3(b_ref[...].astype(F32), pb, nb, wb_ref[...])
    o_ref[...] = (a * jax.nn.sigmoid(a) * b).astype(o_ref.dtype)


def conv_gate(u, conv_w):
    M, N2 = u.shape
    F = N2 // 2
    tm = min(512, M)
    tn = 1024
    nb = F // tn
    specs = _halo_specs(tm, tn, M, lambda j: j, 16) + _halo_specs(tm, tn, M, lambda j: j + nb, 16)
    specs += [pl.BlockSpec((3, tn), lambda i, j: (0, j)), pl.BlockSpec((3, tn), lambda i, j: (0, j + nb))]
    return pl.pallas_call(
        _convgate_kernel,
        grid=(M // tm, nb),
        in_specs=specs,
        out_specs=pl.BlockSpec((tm, tn), lambda i, j: (i, j)),
        out_shape=jax.ShapeDtypeStruct((M, F), BF16),
        compiler_params=_cparams(("parallel", "parallel")),
    )(u, u, u, u, u, u, conv_w, conv_w)


def _dft_tables(T):
    A, B = T // FFT_B, FFT_B
    ka = jnp.arange(A, dtype=jnp.int32)
    a = jnp.arange(A, dtype=jnp.int32)
    b = jnp.arange(B, dtype=jnp.int32)
    n = (ka[None, :, None] * (B * a[None, None, :] + b[:, None, None])) % T
    ang = n.astype(F32) * (2.0 * math.pi / T)
    sa = 1.0 / math.sqrt(A)
    m1 = jnp.concatenate([jnp.cos(ang) * sa, -jnp.sin(ang) * sa], axis=1).astype(BF16)
    kb = jnp.arange(B, dtype=jnp.int32)
    n2 = (kb[:, None] * b[None, :]) % B
    ang2 = n2.astype(F32) * (2.0 * math.pi / B)
    sb = 1.0 / math.sqrt(B)
    m2 = jnp.concatenate([jnp.cos(ang2) * sb, jnp.sin(ang2) * sb], axis=1).astype(BF16)
    return m1, m2


def _channel_dft():
    c = np.arange(FOURIER_DIM)
    ang = 2.0 * np.pi * ((c[:, None] * c[None, :]) % FOURIER_DIM) / FOURIER_DIM
    s = 1.0 / math.sqrt(FOURIER_DIM)
    return np.cos(ang) * s, np.sin(ang) * s


FFT_SUB = 8


def _f1_kernel(x_ref, m_ref, ch_ref, re_ref, im_ref):
    A = x_ref.shape[0]
    ch = ch_ref[...]
    for j in range(FFT_SUB):
        y = _dot(m_ref[j], x_ref[:, j, :].astype(BF16))
        for h in range(2):
            sl = slice(h * FOURIER_DIM, (h + 1) * FOURIER_DIM)
            lhs = jnp.concatenate([y[:A, sl], y[A:, sl]], axis=1).astype(BF16)
            yp = _dot(lhs, ch)
            re_ref[j, :, sl] = yp[:, :FOURIER_DIM]
            im_ref[j, :, sl] = yp[:, FOURIER_DIM:]


def _f2_kernel(re_ref, im_ref, m_ref, w_ref, b_ref, o_ref):
    for j in range(FFT_SUB):
        rhs = jnp.concatenate([re_ref[:, j, :], im_ref[:, j, :]], axis=0).astype(BF16)
        spec = _dot(m_ref[...], rhs)
        outs = []
        for h in range(FOURIER_HEADS):
            sl = slice(h * FOURIER_DIM, (h + 1) * FOURIER_DIM)
            outs.append(_dot(spec[:, sl].astype(BF16), w_ref[h].astype(BF16)) + b_ref[h])
        o_ref[:, j, :] = jnp.concatenate(outs, axis=1)


def fourier_mix(z, fw, fb):
    T, NC = z.shape
    A, B = T // FFT_B, FFT_B
    m1, m2 = _dft_tables(T)
    cc, ss = _channel_dft()
    ch = jnp.asarray(np.block([[cc, -ss], [ss, cc]]), BF16)
    cb0 = COL_F // 512
    yre, yim = pl.pallas_call(
        _f1_kernel,
        grid=(B // FFT_SUB, 2),
        in_specs=[
            pl.BlockSpec((A, FFT_SUB, 512), lambda b, c: (0, b, cb0 + c)),
            pl.BlockSpec((FFT_SUB, 2 * A, A), lambda b, c: (b, 0, 0)),
            pl.BlockSpec((512, 512), lambda b, c: (0, 0)),
        ],
        out_specs=[pl.BlockSpec((FFT_SUB, A, 512), lambda b, c: (b, 0, c))] * 2,
        out_shape=[jax.ShapeDtypeStruct((B, A, FOURIER_WIDTH), F32)] * 2,
        compiler_params=_cparams(("parallel", "parallel")),
    )(z.reshape(A, B, NC), m1, ch)
    blk = pl.BlockSpec((B, FFT_SUB, FOURIER_WIDTH), lambda i: (0, i, 0))
    out = pl.pallas_call(
        _f2_kernel,
        grid=(A // FFT_SUB,),
        in_specs=[
            blk, blk,
            pl.BlockSpec((B, 2 * B), lambda i: (0, 0)),
            pl.BlockSpec((FOURIER_HEADS, FOURIER_DIM, FOURIER_DIM), lambda i: (0, 0, 0)),
            pl.BlockSpec((FOURIER_HEADS, 1, FOURIER_DIM), lambda i: (0, 0, 0)),
        ],
        out_specs=blk,
        out_shape=jax.ShapeDtypeStruct((B, A, FOURIER_WIDTH), F32),
        compiler_params=_cparams(("parallel",)),
    )(yre, yim, m2, fw, fb.reshape(FOURIER_HEADS, 1, FOURIER_DIM))
    return out.reshape(T, FOURIER_WIDTH)


def _fctx_kernel(f_ref, cs_ref, ts_ref, w_ref, b_ref, o_ref):
    for h in range(FOURIER_HEADS):
        sl = slice(h * FOURIER_DIM, (h + 1) * FOURIER_DIM)
        g = _dot(f_ref[:, sl].astype(BF16), cs_ref[...])
        gg = jnp.concatenate([g[:, :FOURIER_DIM], g[:, FOURIER_DIM:]], axis=0).astype(BF16)
        spec = _dot(ts_ref[...], gg)
        out = _dot(spec.astype(BF16), w_ref[h].astype(BF16)) + b_ref[h]
        o_ref[:, sl] = out.astype(o_ref.dtype)


def fourier_mix_ctx(zc, fw, fb):
    T = zc.shape[0]
    cc, ss = _channel_dft()
    cs = jnp.asarray(np.concatenate([cc, ss], axis=1), BF16)
    t = np.arange(T)
    ang = 2.0 * np.pi * ((t[:, None] * t[None, :]) % T) / T
    st = 1.0 / math.sqrt(T)
    ts = jnp.asarray(np.concatenate([np.cos(ang) * st, -np.sin(ang) * st], axis=1), BF16)
    return pl.pallas_call(
        _fctx_kernel,
        grid=(1,),
        in_specs=[
            pl.BlockSpec((T, FOURIER_WIDTH), lambda i: (0, COL_F // FOURIER_WIDTH)),
            pl.BlockSpec((FOURIER_DIM, 2 * FOURIER_DIM), lambda i: (0, 0)),
            pl.BlockSpec((T, 2 * T), lambda i: (0, 0)),
            pl.BlockSpec((FOURIER_HEADS, FOURIER_DIM, FOURIER_DIM), lambda i: (0, 0, 0)),
            pl.BlockSpec((FOURIER_HEADS, 1, FOURIER_DIM), lambda i: (0, 0, 0)),
        ],
        out_specs=pl.BlockSpec((T, FOURIER_WIDTH), lambda i: (0, 0)),
        out_shape=jax.ShapeDtypeStruct((T, FOURIER_WIDTH), BF16),
        compiler_params=_cparams(("arbitrary",)),
    )(zc, cs, ts, fw, fb.reshape(FOURIER_HEADS, 1, FOURIER_DIM))


def _rope_tables(T):
    nf = NA_HEAD_DIM // 4
    t = jnp.arange(T)
    inv = 1.0 / (ROPE_THETA ** (jnp.arange(nf, dtype=F32) / nf))
    lane = jnp.arange(NA_HEAD_DIM)
    pos = jnp.where(lane[None, :] < NA_HEAD_DIM // 2, (t // GRID_W)[:, None], (t % GRID_W)[:, None]).astype(F32)
    ang = pos * inv[lane % nf][None, :]
    sign = jnp.where((lane % (2 * nf)) < nf, -1.0, 1.0)[None, :]
    return jnp.cos(ang), jnp.sin(ang) * sign


def _head_norm_rope(x, w, cos, sin):
    ms = jnp.mean(x * x, axis=-1, keepdims=True)
    y = x * lax.rsqrt(ms + NORM_EPS) * w
    if cos is None:
        return y
    lane = lax.broadcasted_iota(jnp.int32, y.shape, 1)
    swap = jnp.where((lane % 64) < 32, pltpu.roll(y, 96, axis=1), pltpu.roll(y, 32, axis=1))
    return y * cos + swap * sin


def _qkprep_kernel(q_ref, k_ref, qw_ref, kw_ref, cos_ref, sin_ref, qo_ref, ko_ref, *, rope):
    cos = cos_ref[...] if rope else None
    sin = sin_ref[...] if rope else None
    qw = qw_ref[...] * (NA_HEAD_DIM ** -0.5)
    kw = kw_ref[...]
    for h in range(4):
        sl = slice(h * LANE, (h + 1) * LANE)
        qo_ref[:, sl] = _head_norm_rope(q_ref[:, sl], qw, cos, sin).astype(qo_ref.dtype)
        ko_ref[:, sl] = _head_norm_rope(k_ref[:, sl], kw, cos, sin).astype(ko_ref.dtype)


def qk_prep(z, qw, kw, rope):
    T = z.shape[0]
    tm = min(512, T)
    if rope:
        cos, sin = _rope_tables(T)
    else:
        cos = sin = jnp.zeros((T, LANE), F32)
    nq = NA_WIDTH // 512
    return pl.pallas_call(
        functools.partial(_qkprep_kernel, rope=rope),
        grid=(T // tm, nq),
        in_specs=[
            pl.BlockSpec((tm, 512), lambda i, j: (i, COL_Q // 512 + j)),
            pl.BlockSpec((tm, 512), lambda i, j: (i, COL_K // 512 + j)),
            pl.BlockSpec((1, LANE), lambda i, j: (0, 0)),
            pl.BlockSpec((1, LANE), lambda i, j: (0, 0)),
            pl.BlockSpec((tm, LANE), lambda i, j: (i, 0)),
            pl.BlockSpec((tm, LANE), lambda i, j: (i, 0)),
        ],
        out_specs=[pl.BlockSpec((tm, 512), lambda i, j: (i, j))] * 2,
        out_shape=[jax.ShapeDtypeStruct((T, NA_WIDTH), BF16)] * 2,
        compiler_params=_cparams(("parallel", "parallel")),
    )(z, z, qw.reshape(1, LANE), kw.reshape(1, LANE), cos, sin)


def _toeplitz_kernel(r_ref, e_ref, o_ref):
    r = r_ref[0]
    acc = jnp.zeros(o_ref.shape[1:], F32)
    for d in range(2 * NA_KW - 1):
        acc = acc + r[:, d:d + 1] * e_ref[d:d + 1, :]
    o_ref[0] = acc


def rpb_bias_tables(rpb):
    L, H = rpb.shape[0], rpb.shape[1]
    ndr, ndc = 2 * NA_KH - 1, 2 * NA_KW - 1
    q = np.arange(GRID_W)
    e = np.zeros((32, GRID_W, GRID_W), np.float32)
    for d in range(ndc):
        e[d] = (q[None, :] - q[:, None] + (NA_KW - 1)) == d
    e = jnp.asarray(e.reshape(32, GRID_W * GRID_W))
    rp = jnp.pad(rpb.reshape(L * H, ndr, ndc), ((0, 0), (0, 16 - ndr), (0, 32 - ndc)))
    toep = pl.pallas_call(
        _toeplitz_kernel,
        grid=(L * H,),
        in_specs=[pl.BlockSpec((1, 16, 32), lambda i: (i, 0, 0)),
                  pl.BlockSpec((32, GRID_W * GRID_W), lambda i: (0, 0))],
        out_specs=pl.BlockSpec((1, 16, GRID_W * GRID_W), lambda i: (i, 0, 0)),
        out_shape=jax.ShapeDtypeStruct((L * H, 16, GRID_W * GRID_W), F32),
        compiler_params=_cparams(("parallel",)),
    )(rp, e)
    toep = toep.reshape(L, H, 16, GRID_W, GRID_W)
    col_start = np.clip(q - NA_KW // 2, 0, GRID_W - NA_KW)
    in_win = (q[None, :] >= col_start[:, None]) & (q[None, :] < col_start[:, None] + NA_KW)
    mask = jnp.asarray(np.where(in_win, 0.0, -1e30).astype(np.float32))
    tabs = []
    for o in range(NA_KH):
        band = toep[:, :, NA_KH - 1 - o:2 * NA_KH - 1 - o]
        band = band + mask[None, None, None]
        tabs.append(jnp.transpose(band, (0, 1, 3, 2, 4)).reshape(L, H, GRID_W, NA_KH * GRID_W))
    return jnp.stack(tabs, axis=2)


NA_HEADS_PER_STEP = 2


def _na_kernel(q_ref, kp_ref, kc_ref, kn_ref, vp_ref, vc_ref, vn_ref, kx_ref, vx_ref, bt_ref, o_ref,
               kbuf, vbuf, *, nrows):
    m = pl.program_id(1)
    blk = NA_KH * GRID_W
    kbuf[0:blk] = kp_ref[...]
    kbuf[blk:2 * blk] = kc_ref[...]
    kbuf[2 * blk:3 * blk] = kn_ref[...]
    vbuf[0:blk] = vp_ref[...].astype(BF16)
    vbuf[blk:2 * blk] = vc_ref[...].astype(BF16)
    vbuf[2 * blk:3 * blk] = vn_ref[...].astype(BF16)
    kctx = kx_ref[...]
    vctx = vx_ref[...].astype(BF16)
    work, scores = [], []
    for hh in range(NA_HEADS_PER_STEP):
        hs = slice(hh * LANE, (hh + 1) * LANE)
        for j in range(NA_KH):
            r = m * NA_KH + j
            rs = jnp.clip(r - NA_KH // 2, 0, nrows - NA_KH)
            start = pl.multiple_of((rs - (m - 1) * NA_KH) * GRID_W, GRID_W)
            q = q_ref[j * GRID_W:(j + 1) * GRID_W, hs]
            s = _dot_nt(q, kbuf[pl.ds(start, blk), hs]) + bt_ref[hh, r - rs]
            work.append((hs, j, start))
            scores.append((s, _dot_nt(q, kctx[:, hs])))
    probs = []
    for s, sc in scores:
        mx = jnp.maximum(jnp.max(s, axis=-1, keepdims=True), jnp.max(sc, axis=-1, keepdims=True))
        p = jnp.exp(s - mx)
        pc = jnp.exp(sc - mx)
        den = jnp.sum(p, axis=-1, keepdims=True) + jnp.sum(pc, axis=-1, keepdims=True)
        probs.append((p.astype(BF16), pc.astype(BF16), den))
    for (hs, j, start), (p, pc, den) in zip(work, probs):
        acc = _dot(p, vbuf[pl.ds(start, blk), hs]) + _dot(pc, vctx[:, hs])
        o_ref[j * GRID_W:(j + 1) * GRID_W, hs] = (acc / den).astype(o_ref.dtype)


def na_attention(qn, kn, z, kcn, zc, bias_tab):
    T = qn.shape[0]
    C = kcn.shape[0]
    nrows = T // GRID_W
    blk = NA_KH * GRID_W
    nblk = T // blk
    hw = NA_HEADS_PER_STEP * LANE
    vcol = COL_V // hw
    prev = lambda h, m: (jnp.maximum(m - 1, 0), h)
    cur = lambda h, m: (m, h)
    nxt = lambda h, m: (jnp.minimum(m + 1, nblk - 1), h)
    vprev = lambda h, m: (jnp.maximum(m - 1, 0), vcol + h)
    vcur = lambda h, m: (m, vcol + h)
    vnxt = lambda h, m: (jnp.minimum(m + 1, nblk - 1), vcol + h)
    return pl.pallas_call(
        functools.partial(_na_kernel, nrows=nrows),
        grid=(NA_HEADS // NA_HEADS_PER_STEP, nblk),
        in_specs=[
            pl.BlockSpec((blk, hw), cur),
            pl.BlockSpec((blk, hw), prev), pl.BlockSpec((blk, hw), cur), pl.BlockSpec((blk, hw), nxt),
            pl.BlockSpec((blk, hw), vprev), pl.BlockSpec((blk, hw), vcur), pl.BlockSpec((blk, hw), vnxt),
            pl.BlockSpec((C, hw), lambda h, m: (0, h)),
            pl.BlockSpec((C, hw), lambda h, m: (0, vcol + h)),
            pl.BlockSpec((NA_HEADS_PER_STEP, NA_KH, GRID_W, blk), lambda h, m: (h, 0, 0, 0)),
        ],
        out_specs=pl.BlockSpec((blk, hw), cur),
        out_shape=jax.ShapeDtypeStruct((T, NA_WIDTH), BF16),
        scratch_shapes=[pltpu.VMEM((3 * blk, hw), BF16), pltpu.VMEM((3 * blk, hw), BF16)],
        compiler_params=_cparams(("parallel", "parallel")),
    )(qn, kn, kn, kn, z, z, z, kcn, zc, bias_tab)


def _ctxattn_kernel(q_ref, k_ref, v_ref, o_ref):
    s = _dot_nt(q_ref[...], k_ref[...])
    p = jnp.exp(s - jnp.max(s, axis=-1, keepdims=True))
    den = jnp.sum(p, axis=-1, keepdims=True)
    o_ref[...] = (_dot(p.astype(BF16), v_ref[...].astype(BF16)) / den).astype(o_ref.dtype)


def ctx_attention(qcn, kcn, zc):
    C = qcn.shape[0]
    vcol = COL_V // LANE
    return pl.pallas_call(
        _ctxattn_kernel,
        grid=(NA_HEADS,),
        in_specs=[pl.BlockSpec((C, LANE), lambda h: (0, h)), pl.BlockSpec((C, LANE), lambda h: (0, h)),
                  pl.BlockSpec((C, LANE), lambda h: (0, vcol + h))],
        out_specs=pl.BlockSpec((C, LANE), lambda h: (0, h)),
        out_shape=jax.ShapeDtypeStruct((C, NA_WIDTH), BF16),
        compiler_params=_cparams(("parallel",)),
    )(qcn, kcn, zc)


def _seg_sum64(x):
    lane = lax.broadcasted_iota(jnp.int32, x.shape, 1)
    low = lane < RWKV_HEAD_DIM
    s_lo = jnp.sum(jnp.where(low, x, 0.0), axis=-1, keepdims=True)
    s_hi = jnp.sum(jnp.where(low, 0.0, x), axis=-1, keepdims=True)
    return jnp.where(low, s_lo, s_hi)


def _seg_sum(x):
    return jnp.concatenate([_seg_sum64(x[:, i * LANE:(i + 1) * LANE]) for i in range(x.shape[1] // LANE)], axis=1)


def _rwkvprep_kernel(r_ref, rp_ref, rn_ref, k_ref, kp_ref, kn_ref, v_ref, vp_ref, vn_ref, lo_ref, cw_ref,
                     w2_ref, a2_ref, g2_ref, w0_ref, a0_ref, kk_ref, ka_ref, rk_ref,
                     r_o, v_o, kk_o, g_o, bonus_o, logw_o, kd_o, ag_o):
    i = pl.program_id(0)
    n_i = pl.num_programs(0)
    W = RWKV_WIDTH
    cw = cw_ref[...]
    r = _conv3(r_ref[...], *_halo_rows(rp_ref, rn_ref, i, n_i), cw[:, 0:W])
    k = _conv3(k_ref[...], *_halo_rows(kp_ref, kn_ref, i, n_i), cw[:, W:2 * W])
    v = _conv3(v_ref[...], *_halo_rows(vp_ref, vn_ref, i, n_i), cw[:, 2 * W:3 * W])
    lora = lo_ref[...]
    wl = _dot(jnp.tanh(lora[:, 0:2 * W_LORA]).astype(BF16), w2_ref[...])
    al = _dot(lora[:, 2 * W_LORA:2 * (W_LORA + A_LORA)].astype(BF16), a2_ref[...])
    g = _dot(jax.nn.sigmoid(lora[:, 2 * (W_LORA + A_LORA):]).astype(BF16), g2_ref[...])
    kkr = k * kk_ref[...]
    kk = kkr / jnp.maximum(jnp.sqrt(_seg_sum(kkr * kkr)), 1e-12)
    kds = []
    for d in range(2):
        logw = -math.exp(-0.5) * jax.nn.sigmoid(w0_ref[d:d + 1, :] + wl[:, d * W:(d + 1) * W])
        a = jax.nn.sigmoid(a0_ref[d:d + 1, :] + al[:, d * W:(d + 1) * W])
        kd = k * (1.0 + (a - 1.0) * ka_ref[...])
        kds.append(kd)
        for gi in range(N_GROUPS):
            sl = slice(gi * GROUP, (gi + 1) * GROUP)
            logw_o[d, gi] = logw[:, sl]
            kd_o[d, gi] = kd[:, sl].astype(kd_o.dtype)
            ag_o[d, gi] = a[:, sl].astype(ag_o.dtype)
    bonus = _seg_sum(r * (kds[0] + kds[1]) * rk_ref[...]) * v
    for gi in range(N_GROUPS):
        sl = slice(gi * GROUP, (gi + 1) * GROUP)
        r_o[gi] = r[:, sl].astype(r_o.dtype)
        v_o[gi] = v[:, sl].astype(v_o.dtype)
        kk_o[gi] = kk[:, sl].astype(kk_o.dtype)
        g_o[gi] = g[:, sl].astype(g_o.dtype)
        bonus_o[gi] = bonus[:, sl].astype(bonus_o.dtype)


def rwkv_prep(z, p):
    T = z.shape[0]
    tm = min(128, T)
    W = RWKV_WIDTH
    cb = COL_R // W
    specs = []
    for c in range(3):
        specs += _halo_specs(tm, W, T, lambda j, c=c: cb + c)
    specs = [pl.BlockSpec(s.block_shape, lambda i, f=s.index_map: f(i, 0)) for s in specs]
    full = lambda shape: pl.BlockSpec(shape, lambda i: (0,) * len(shape))
    specs += [
        pl.BlockSpec((tm, LORA_PAD), lambda i: (i, COL_L // LORA_PAD)),
        full((3, 3 * W)), full((2 * W_LORA, 2 * W)), full((2 * A_LORA, 2 * W)), full((LORA_PAD - 256, W)),
        full((2, W)), full((2, W)), full((1, W)), full((1, W)), full((1, W)),
    ]
    g1 = pl.BlockSpec((N_GROUPS, tm, GROUP), lambda i: (0, i, 0))
    g2 = pl.BlockSpec((2, N_GROUPS, tm, GROUP), lambda i: (0, 0, i, 0))
    s1 = jax.ShapeDtypeStruct((N_GROUPS, T, GROUP), BF16)
    s2 = jax.ShapeDtypeStruct((2, N_GROUPS, T, GROUP), BF16)
    s2f = jax.ShapeDtypeStruct((2, N_GROUPS, T, GROUP), F32)
    return pl.pallas_call(
        _rwkvprep_kernel,
        grid=(T // tm,),
        in_specs=specs,
        out_specs=[g1] * 5 + [g2] * 3,
        out_shape=[s1] * 5 + [s2f, s2, s2],
        compiler_params=_cparams(("parallel",)),
    )(z, z, z, z, z, z, z, z, z, z, p['rwkv_conv'], p['w2bd'], p['a2bd'], p['g2p'], p['w0'], p['a0'],
      p['k_k'].reshape(1, W), p['k_a'].reshape(1, W), p['r_k'].reshape(1, W))


def _fold(x):
    c = CHUNK
    return x[0:c] + x[c:2 * c] + x[2 * c:3 * c] + x[3 * c:4 * c]


def _rwkv_kernel(r_ref, v_ref, kk_ref, logw_ref, kd_ref, ag_ref, tri_ref, ms_ref, mi_ref, bd_ref, bdf_ref, z0_ref,
                 y_ref, zf_ref, z_scr):
    forward = pl.program_id(0) == 0
    c = pl.program_id(1)

    @pl.when(c == 0)
    def _():
        z_scr[...] = z0_ref[0]

    tri = tri_ref[0]
    m_strict = ms_ref[0]
    m_incl = mi_ref[0]
    bd = bd_ref[...]
    bdf = bdf_ref[...]
    row = lax.broadcasted_iota(jnp.int32, (CHUNK, GROUP), 0)
    col = lax.broadcasted_iota(jnp.int32, (CHUNK, GROUP), 1)
    diag = row == (col % RWKV_HEAD_DIM)
    eye_f = jnp.where(diag, 1.0, 0.0)

    def expand(x):
        return jnp.concatenate([x.astype(BF16)] * 4, axis=0) * bd

    def stack(*xs):
        return jnp.concatenate([x.astype(BF16) for x in xs], axis=0)

    cs_all = _dot(tri, jnp.concatenate([h for g in range(N_GROUPS) for h in _split(logw_ref[0, g])], axis=1))

    def prep(g):
        lw = logw_ref[0, g]
        cs = cs_all[:, 2 * g * GROUP:2 * (g + 1) * GROUP]
        linc = cs[:, :GROUP] + cs[:, GROUP:]
        ltot = jnp.where(forward, linc[CHUNK - 1:CHUNK, :], linc[0:1, :])
        e_inc = jnp.exp(linc)
        e_neg = jnp.exp(-linc)
        e_exc = jnp.exp(linc - lw)
        e_rem = jnp.exp(ltot - linc)
        kk = kk_ref[g].astype(F32)
        kd = kd_ref[0, g].astype(F32)
        b = kk * ag_ref[0, g].astype(F32)
        at = -kk * e_exc
        rt = r_ref[g].astype(F32) * e_inc
        gram = _dot_nt(stack(at, rt), jnp.concatenate([expand(b * e_neg), expand(kd * e_neg)], axis=0))
        fab = gram[:CHUNK, :GROUP] * m_strict
        return dict(g=g, rt=rt, at=at, e_tot=jnp.exp(ltot), bh=b * e_rem, kh=kd * e_rem, v=v_ref[g],
                    fp=fab, ft=eye_f + fab,
                    fak=gram[:CHUNK, GROUP:] * m_strict, frb=gram[CHUNK:, :GROUP] * m_incl,
                    frk=gram[CHUNK:, GROUP:] * m_incl)

    st = [prep(g) for g in range(N_GROUPS)]
    for s in st:
        res = _dot(stack(s['fp'], s['frb']), expand(s['fp']))
        s['fp'] = res[:CHUNK]
        s['fg'] = s['frb'] + res[CHUNK:]
    for _ in range(4):
        for s in st:
            res = _dot(stack(s['fp'], s['ft'], s['fg']), expand(s['fp']))
            s['fp'] = res[:CHUNK]
            s['ft'] = s['ft'] + res[CHUNK:2 * CHUNK]
            s['fg'] = s['fg'] + res[2 * CHUNK:]
    for s in st:
        res = _dot(stack(s['ft'], s['fg']), expand(s['fp']))
        s['ft'] = s['ft'] + res[:CHUNK]
        s['fg'] = s['fg'] + res[CHUNK:]
    for s in st:
        res = _dot(stack(s['ft'], s['fg']), jnp.concatenate([expand(s['fak']), expand(s['at'])], axis=1))
        s['fta'] = res[:CHUNK, :GROUP]
        s['ff'] = res[CHUNK:, :GROUP] + s['frk']
        s['fa1'] = res[:CHUNK, GROUP:]
        s['rp'] = s['rt'] + res[CHUNK:, GROUP:]
    for s in st:
        res = _dot(stack(s['fta'], s['ff']), expand(s['v']))
        s['fu0'] = res[:CHUNK]
        s['y0'] = res[CHUNK:]
    for s in st:
        zero = jnp.zeros((CHUNK, GROUP), BF16)
        rhs = jnp.concatenate([jnp.concatenate([s['fa1'].astype(BF16), zero], axis=1),
                               jnp.concatenate([zero, s['fu0'].astype(BF16)], axis=1),
                               jnp.concatenate([zero, s['v']], axis=1)], axis=0)
        mn = _dot_tn(stack(s['bh'], s['bh'], s['kh']), rhs)
        s['mf'] = _fold(mn[:, :GROUP] * bdf) + jnp.where(diag, s['e_tot'], 0.0)
        s['nf'] = _fold(mn[:, GROUP:] * bdf)
    for s in st:
        g = s['g']
        zhx = expand(z_scr[g])
        mh, ml = _split(s['mf'])
        res = _dot(jnp.concatenate([s['rp'].astype(BF16), mh, ml], axis=0), zhx)
        y_ref[0, g] = res[:CHUNK] + s['y0']
        z_scr[g] = res[CHUNK:2 * CHUNK] + res[2 * CHUNK:] + s['nf']

    @pl.when(c == pl.num_programs(1) - 1)
    def _():
        zf_ref[0] = z_scr[...]


def _chunk_masks():
    i = np.arange(CHUNK)
    tri = np.stack([i[None, :] <= i[:, None], i[None, :] >= i[:, None]]).astype(np.float32)
    j = np.arange(GROUP)
    same = ((j[:, None] // CHUNK) == (j[None, :] // CHUNK)).astype(np.float32)
    js = j[None, :] % CHUNK
    strict = np.stack([js < i[:, None], js > i[:, None]]).astype(np.float32)
    incl = np.stack([js <= i[:, None], js >= i[:, None]]).astype(np.float32)
    return jnp.asarray(tri, BF16), jnp.asarray(strict), jnp.asarray(incl), jnp.asarray(same, BF16), jnp.asarray(same)


def rwkv_scan(r, v, kk, logw, kd, ag, z0):
    T = r.shape[1]
    nc = T // CHUNK
    tri, strict, incl, bd, bdf = _chunk_masks()
    order = lambda d, c: c + d * (nc - 1 - 2 * c)
    b1 = pl.BlockSpec((N_GROUPS, CHUNK, GROUP), lambda d, c: (0, order(d, c), 0))
    b2 = pl.BlockSpec((1, N_GROUPS, CHUNK, GROUP), lambda d, c: (d, 0, order(d, c), 0))
    per_dir = lambda shape: pl.BlockSpec((1,) + shape, lambda d, c: (d,) + (0,) * len(shape))
    full = lambda shape: pl.BlockSpec(shape, lambda d, c: (0,) * len(shape))
    zspec = per_dir((N_GROUPS, CHUNK, GROUP))
    return pl.pallas_call(
        _rwkv_kernel,
        grid=(2, nc),
        in_specs=[b1, b1, b1, b2, b2, b2, per_dir((CHUNK, CHUNK)), per_dir((CHUNK, GROUP)), per_dir((CHUNK, GROUP)),
                  full((GROUP, GROUP)), full((GROUP, GROUP)), zspec],
        out_specs=[b2, zspec],
        out_shape=[jax.ShapeDtypeStruct((2, N_GROUPS, T, GROUP), F32),
                   jax.ShapeDtypeStruct((2, N_GROUPS, CHUNK, GROUP), F32)],
        scratch_shapes=[pltpu.VMEM((N_GROUPS, CHUNK, GROUP), F32)],
        compiler_params=_cparams(("arbitrary", "arbitrary")),
    )(r, v, kk, logw, kd, ag, tri, strict, incl, bd, bdf, z0)


def _rwkvpost_kernel(y_ref, g_ref, bonus_ref, gw_ref, gb_ref, o_ref):
    for gi in range(N_GROUPS):
        y = y_ref[0, gi] + y_ref[1, gi]
        mu = _seg_sum(y) * (1.0 / RWKV_HEAD_DIM)
        yc = y - mu
        var = _seg_sum(yc * yc) * (1.0 / RWKV_HEAD_DIM)
        sl = slice(gi * GROUP, (gi + 1) * GROUP)
        yn = yc * lax.rsqrt(var + GN_EPS) * gw_ref[:, sl] + gb_ref[:, sl]
        o_ref[:, sl] = ((yn + bonus_ref[gi].astype(F32)) * g_ref[gi].astype(F32)).astype(o_ref.dtype)


def rwkv_post(y, g, bonus, gn_w, gn_b):
    T = y.shape[2]
    tm = min(256, T)
    W = RWKV_WIDTH
    return pl.pallas_call(
        _rwkvpost_kernel,
        grid=(T // tm,),
        in_specs=[pl.BlockSpec((2, N_GROUPS, tm, GROUP), lambda i: (0, 0, i, 0)),
                  pl.BlockSpec((N_GROUPS, tm, GROUP), lambda i: (0, i, 0)),
                  pl.BlockSpec((N_GROUPS, tm, GROUP), lambda i: (0, i, 0)),
                  pl.BlockSpec((1, W), lambda i: (0, 0)), pl.BlockSpec((1, W), lambda i: (0, 0))],
        out_specs=pl.BlockSpec((tm, W), lambda i: (i, 0)),
        out_shape=jax.ShapeDtypeStruct((T, W), BF16),
        compiler_params=_cparams(("parallel",)),
    )(y, g, bonus, gn_w.reshape(1, W), gn_b.reshape(1, W))


def rwkv_mix(z, p, z0):
    r, v, kk, g, bonus, logw, kd, ag = rwkv_prep(z, p)
    y, zf = rwkv_scan(r, v, kk, logw, kd, ag, z0)
    return rwkv_post(y, g, bonus, p['gn_w'], p['gn_b']), zf


def _lora_heads(l, w2, a2, g2):
    W = RWKV_WIDTH
    zw = jnp.zeros((W_LORA, W), F32)
    w2bd = jnp.concatenate([jnp.concatenate([w2[l, 0], zw], axis=1), jnp.concatenate([zw, w2[l, 1]], axis=1)], axis=0)
    a2bd = jnp.concatenate([jnp.concatenate([a2[l, 0], zw], axis=1), jnp.concatenate([zw, a2[l, 1]], axis=1)], axis=0)
    g2p = jnp.pad(g2[l], ((0, LORA_PAD - 256 - G_LORA), (0, 0)))
    return dict(w2bd=w2bd.astype(BF16), a2bd=a2bd.astype(BF16), g2p=g2p.astype(BF16))


def _layer(x, ctx, mod_x, mod_c, p, l, bias_tab, ctx_out):
    Dm = D_MODEL
    sh1, sc1, gt1, sh2, sc2, gt2 = [mod_x[i * Dm:(i + 1) * Dm] for i in range(6)]
    csh1, csc1, cgt1, csh2, csc2, cgt2 = [mod_c[i * Dm:(i + 1) * Dm] for i in range(6)]
    zx = norm_mod_matmul(x, p['norm1_w'], sc1, sh1, p['w_in'], l)
    zc = norm_mod_matmul(ctx, p['norm1_w'], csc1, csh1, p['w_in'], l)
    fx = fourier_mix(zx, p['fourier_w'], p['fourier_b'])
    qx, kx = qk_prep(zx, p['q_norm_w'], p['k_norm_w'], rope=True)
    qc, kc = qk_prep(zc, p['q_norm_w'], p['k_norm_w'], rope=False)
    ax = na_attention(qx, kx, zx, kc, zc, bias_tab)
    z0 = jnp.zeros((2, N_GROUPS, CHUNK, GROUP), F32)
    rc, zf = rwkv_mix(zc, p, z0)
    rx, _ = rwkv_mix(zx, p, zf)
    x = matmul_residual([fx, ax, rx], p['w_out'], l, x, gt1)
    u = norm_mod_matmul(x, p['norm2_w'], sc2, sh2, p['w_ffn_in'], l, out_dtype=BF16)
    x = matmul_residual([conv_gate(u, p['ffn_conv'])], p['w_ffn_out'], l, x, gt2)
    if ctx_out:
        fc = fourier_mix_ctx(zc, p['fourier_w'], p['fourier_b'])
        ac = ctx_attention(qc, kc, zc)
        ctx = matmul_residual([fc, ac, rc], p['w_out'], l, ctx, cgt1)
        uc = norm_mod_matmul(ctx, p['norm2_w'], csc2, csh2, p['w_ffn_in'], l, out_dtype=BF16)
        ctx = matmul_residual([conv_gate(uc, p['ffn_conv'])], p['w_ffn_out'], l, ctx, cgt2)
    return x, ctx


def kernel(x, c, ctx, c_ctx, ada_w, ada_b, norm1_w, norm2_w, w_in, fourier_w, fourier_b, q_norm_w, k_norm_w, rpb,
           rwkv_conv, w0, w2, a0, a2, g2, k_k, k_a, r_k, gn_w, gn_b, w_out, ffn_conv, w_ffn_in, w_ffn_out):
    L = ada_w.shape[0]
    xs = x[0]
    cs = ctx[0]
    mods = ada_mod(jnp.concatenate([c, c_ctx[None, :]], axis=0), ada_w, ada_b)
    bias_tabs = rpb_bias_tables(rpb)
    big = dict(w_in=cast_w_in(w_in), w_out=cast_bf16(w_out), w_ffn_in=cast_bf16(w_ffn_in), w_ffn_out=cast_bf16(w_ffn_out))
    for l in range(L):
        p = dict(big)
        p.update(_lora_heads(l, w2, a2, g2))
        p.update(norm1_w=norm1_w[l], norm2_w=norm2_w[l], fourier_w=fourier_w[l], fourier_b=fourier_b[l],
                 q_norm_w=q_norm_w[l], k_norm_w=k_norm_w[l], rwkv_conv=rwkv_conv[l], w0=w0[l], a0=a0[l],
                 k_k=k_k[l], k_a=k_a[l], r_k=r_k[l], gn_w=gn_w[l], gn_b=gn_b[l], ffn_conv=ffn_conv[l])
        xs, cs = _layer(xs, cs, mods[l, 0], mods[l, 1], p, l, bias_tabs[l], l < L - 1)
    return xs[None]
```

```python
import functools
import math

import jax
import jax.numpy as jnp
import numpy as np
from jax import lax
from jax.experimental import pallas as pl
from jax.experimental.pallas import tpu as pltpu

F32 = jnp.float32
BF16 = jnp.bfloat16

D_MODEL = 4096
GRID_W = 64
FOURIER_WIDTH = 1024
FOURIER_HEADS = 4
FOURIER_DIM = 256
NA_WIDTH = 1536
NA_HEAD_DIM = 128
NA_HEADS = 12
NA_KH = 8
NA_KW = 16
RWKV_WIDTH = 1536
RWKV_HEAD_DIM = 64
W_LORA = 64
A_LORA = 64
G_LORA = 224
D_FF = 5120
ROPE_THETA = 10000.0
NORM_EPS = 1e-6
GN_EPS = 64e-5

COL_Q = 0
COL_K = NA_WIDTH
COL_V = 2 * NA_WIDTH
COL_R = 3 * NA_WIDTH
COL_F = 6 * NA_WIDTH
COL_L = COL_F + FOURIER_WIDTH
LORA_PAD = 512
IN_COLS_PAD = COL_L + LORA_PAD

CHUNK = 64
GROUP = 256
N_GROUPS = RWKV_WIDTH // GROUP
FFT_B = 128

LANE = 128
VMEM_LIMIT = 48 * 1024 * 1024


def _cparams(sem):
    return pltpu.CompilerParams(dimension_semantics=sem, vmem_limit_bytes=VMEM_LIMIT)


def _dot(a, b):
    return jnp.dot(a, b, preferred_element_type=F32)


def _dot_nt(a, b):
    return lax.dot_general(a, b, (((1,), (1,)), ((), ())), preferred_element_type=F32)


def _dot_tn(a, b):
    return lax.dot_general(a, b, (((0,), (0,)), ((), ())), preferred_element_type=F32)


def _split(x):
    hi = x.astype(BF16)
    lo = (x - hi.astype(F32)).astype(BF16)
    return hi, lo


def _ada_kernel(s_ref, w_ref, b_ref, o_ref, acc_ref):
    k = pl.program_id(2)
    tk, tn = w_ref.shape[1], w_ref.shape[2]
    rep = tn // LANE

    @pl.when(k == 0)
    def _():
        acc_ref[...] = jnp.zeros_like(acc_ref)

    def body(i, carry):
        a0, a1 = carry
        r = pl.multiple_of(i * 8, 8)
        w = w_ref[0, pl.ds(r, 8), :]
        s0 = s_ref[0, pl.ds(r, 8), :]
        s1 = s_ref[1, pl.ds(r, 8), :]
        s0 = s0 * jax.nn.sigmoid(s0)
        s1 = s1 * jax.nn.sigmoid(s1)
        a0 = a0 + w * jnp.concatenate([s0] * rep, axis=1)
        a1 = a1 + w * jnp.concatenate([s1] * rep, axis=1)
        return a0, a1

    a0, a1 = lax.fori_loop(0, tk // 8, body, (acc_ref[0], acc_ref[1]), unroll=4)
    acc_ref[0] = a0
    acc_ref[1] = a1

    @pl.when(k == pl.num_programs(2) - 1)
    def _():
        o_ref[0, 0:1, :] = jnp.sum(a0, axis=0, keepdims=True) + b_ref[0]
        o_ref[0, 1:2, :] = jnp.sum(a1, axis=0, keepdims=True) + b_ref[0]


def ada_mod(cc, ada_w, ada_b):
    L, K, N = ada_w.shape
    tk, tn = 2048, 1024
    s_b = jnp.broadcast_to(cc[:, :, None], (2, K, LANE))
    return pl.pallas_call(
        _ada_kernel,
        grid=(L, N // tn, K // tk),
        in_specs=[
            pl.BlockSpec((2, tk, LANE), lambda l, j, k: (0, k, 0)),
            pl.BlockSpec((1, tk, tn), lambda l, j, k: (l, k, j)),
            pl.BlockSpec((1, 1, tn), lambda l, j, k: (l, 0, j)),
        ],
        out_specs=pl.BlockSpec((1, 2, tn), lambda l, j, k: (l, 0, j)),
        out_shape=jax.ShapeDtypeStruct((L, 2, N), F32),
        scratch_shapes=[pltpu.VMEM((2, 8, tn), F32)],
        compiler_params=_cparams(("parallel", "parallel", "arbitrary")),
    )(s_b, ada_w, ada_b.reshape(L, 1, N))


def _normmod_kernel(x_ref, nw_ref, sc_ref, sh_ref, o_ref):
    x = x_ref[...]
    ms = jnp.mean(x * x, axis=-1, keepdims=True)
    y = x * lax.rsqrt(ms + NORM_EPS) * nw_ref[...]
    o_ref[...] = (y * (1.0 + sc_ref[...]) + sh_ref[...]).astype(o_ref.dtype)


def _mm_kernel(a_ref, w_ref, o_ref):
    o_ref[...] = _dot(a_ref[...], w_ref[0]).astype(o_ref.dtype)


def _cast_kernel(x_ref, o_ref):
    o_ref[...] = x_ref[...].astype(o_ref.dtype)


def cast_bf16(w):
    L, K, N = w.shape
    tk, tn = 512, 2048
    spec = pl.BlockSpec((1, tk, tn), lambda l, i, j: (l, i, j))
    return pl.pallas_call(
        _cast_kernel,
        grid=(L, K // tk, N // tn),
        in_specs=[spec],
        out_specs=spec,
        out_shape=jax.ShapeDtypeStruct((L, K, N), BF16),
        compiler_params=_cparams(("parallel", "parallel", "parallel")),
    )(w)


LORA_COLS = 2 * W_LORA + 2 * A_LORA + G_LORA
W_IN_BLOCKS = IN_COLS_PAD // 512


def _cast_win_kernel(x_ref, o_ref):
    row = lax.broadcasted_iota(jnp.int32, x_ref.shape[1:], 0)
    keep = (pl.program_id(2) < W_IN_BLOCKS - 1) | (row < LORA_COLS)
    o_ref[0] = jnp.where(keep, x_ref[0], 0.0).T.astype(BF16)


def cast_w_in(w_in):
    L, K, _ = w_in.shape
    tk = 1024
    n_front = (FOURIER_WIDTH) // 512
    n_mid = W_IN_BLOCKS - 1 - n_front

    def src(j):
        return jnp.where(j < n_mid, j + n_front, jnp.where(j < W_IN_BLOCKS - 1, j - n_mid, W_IN_BLOCKS - 1))

    return pl.pallas_call(
        _cast_win_kernel,
        grid=(L, K // tk, W_IN_BLOCKS),
        in_specs=[pl.BlockSpec((1, 512, tk), lambda l, i, j: (l, src(j), i))],
        out_specs=pl.BlockSpec((1, tk, 512), lambda l, i, j: (l, i, j)),
        out_shape=jax.ShapeDtypeStruct((L, K, IN_COLS_PAD), BF16),
        compiler_params=_cparams(("parallel", "parallel", "parallel")),
    )(jnp.swapaxes(w_in, 1, 2))


def norm_mod_matmul(x, nw, sc, sh, w, l, out_dtype=F32):
    M, K = x.shape
    N = w.shape[2]
    tr = min(512, M)
    vec = pl.BlockSpec((1, K), lambda i: (0, 0))
    h = pl.pallas_call(
        _normmod_kernel,
        grid=(M // tr,),
        in_specs=[pl.BlockSpec((tr, K), lambda i: (i, 0)), vec, vec, vec],
        out_specs=pl.BlockSpec((tr, K), lambda i: (i, 0)),
        out_shape=jax.ShapeDtypeStruct((M, K), BF16),
        compiler_params=_cparams(("parallel",)),
    )(x, nw.reshape(1, K), sc.reshape(1, K), sh.reshape(1, K))
    tm = min(1024, M)
    tn = next(t for t in (1024, 768, 512) if N % t == 0)
    return pl.pallas_call(
        _mm_kernel,
        grid=(M // tm, N // tn),
        in_specs=[
            pl.BlockSpec((tm, K), lambda i, j: (i, 0)),
            pl.BlockSpec((1, K, tn), lambda i, j: (l, 0, j)),
        ],
        out_specs=pl.BlockSpec((tm, tn), lambda i, j: (i, j)),
        out_shape=jax.ShapeDtypeStruct((M, N), out_dtype),
        compiler_params=_cparams(("parallel", "arbitrary")),
    )(h, w)


def _mmres_kernel(*refs, ksplits):
    n = len(ksplits)
    a_refs = refs[:n]
    w_ref, x_ref, g_ref, o_ref = refs[n:]
    acc = None
    off = 0
    for a_ref, kp in zip(a_refs, ksplits):
        part = _dot(a_ref[...].astype(BF16), w_ref[0, off:off + kp, :])
        acc = part if acc is None else acc + part
        off += kp
    o_ref[...] = x_ref[...] + g_ref[...] * acc


def matmul_residual(parts, w, l, x, gate):
    M, N = x.shape
    K = w.shape[1]
    ksplits = tuple(p.shape[1] for p in parts)
    assert sum(ksplits) == K
    tm = min(1024, M)
    tn = 512
    in_specs = [pl.BlockSpec((tm, kp), lambda i, j: (i, 0)) for kp in ksplits]
    in_specs += [
        pl.BlockSpec((1, K, tn), lambda i, j: (l, 0, j)),
        pl.BlockSpec((tm, tn), lambda i, j: (i, j)),
        pl.BlockSpec((1, tn), lambda i, j: (0, j)),
    ]
    return pl.pallas_call(
        functools.partial(_mmres_kernel, ksplits=ksplits),
        grid=(M // tm, N // tn),
        in_specs=in_specs,
        out_specs=pl.BlockSpec((tm, tn), lambda i, j: (i, j)),
        out_shape=jax.ShapeDtypeStruct((M, N), F32),
        compiler_params=_cparams(("parallel", "arbitrary")),
    )(*parts, w, x, gate.reshape(1, N))


def _conv3(main, prev_row, next_row, w):
    tm = main.shape[0]
    row = lax.broadcasted_iota(jnp.int32, main.shape, 0)
    dn = jnp.where(row == 0, prev_row, pltpu.roll(main, 1, axis=0))
    up = jnp.where(row == tm - 1, next_row, pltpu.roll(main, tm - 1, axis=0))
    return dn * w[0:1, :] + main * w[1:2, :] + up * w[2:3, :]


def _halo_rows(prev_ref, next_ref, i, n_i):
    prev = prev_ref[...].astype(F32)
    nxt = next_ref[...].astype(F32)
    hr = prev.shape[0]
    prev_row = jnp.where(i == 0, 0.0, prev[hr - 1:hr, :])
    next_row = jnp.where(i == n_i - 1, 0.0, nxt[0:1, :])
    return prev_row, next_row


def _halo_specs(tm, tn, n_rows, col_fn, hr=8):
    rb = tm // hr
    last = n_rows // hr - 1
    return [
        pl.BlockSpec((tm, tn), lambda i, j: (i, col_fn(j))),
        pl.BlockSpec((hr, tn), lambda i, j: (jnp.maximum(i * rb - 1, 0), col_fn(j))),
        pl.BlockSpec((hr, tn), lambda i, j: (jnp.minimum((i + 1) * rb, last), col_fn(j))),
    ]


def _convgate_kernel(a_ref, ap_ref, an_ref, b_ref, bp_ref, bn_ref, wa_ref, wb_ref, o_ref):
    i = pl.program_id(0)
    n_i = pl.num_programs(0)
    pa, na = _halo_rows(ap_ref, an_ref, i, n_i)
    pb, nb = _halo_rows(bp_ref, bn_ref, i, n_i)
    a = _conv3(a_ref[...].astype(F32), pa, na, wa_ref[...])
    b = _conv3(b_ref[...].astype(F32), pb, nb, wb_ref[...])
    o_ref[...] = (a * jax.nn.sigmoid(a) * b).astype(o_ref.dtype)


def conv_gate(u, conv_w):
    M, N2 = u.shape
    F = N2 // 2
    tm = min(512, M)
    tn = 1024
    nb = F // tn
    specs = _halo_specs(tm, tn, M, lambda j: j, 16) + _halo_specs(tm, tn, M, lambda j: j + nb, 16)
    specs += [pl.BlockSpec((3, tn), lambda i, j: (0, j)), pl.BlockSpec((3, tn), lambda i, j: (0, j + nb))]
    return pl.pallas_call(
        _convgate_kernel,
        grid=(M // tm, nb),
        in_specs=specs,
        out_specs=pl.BlockSpec((tm, tn), lambda i, j: (i, j)),
        out_shape=jax.ShapeDtypeStruct((M, F), BF16),
        compiler_params=_cparams(("parallel", "parallel")),
    )(u, u, u, u, u, u, conv_w, conv_w)


def _dft_tables(T):
    A, B = T // FFT_B, FFT_B
    ka = jnp.arange(A, dtype=jnp.int32)
    a = jnp.arange(A, dtype=jnp.int32)
    b = jnp.arange(B, dtype=jnp.int32)
    n = (ka[None, :, None] * (B * a[None, None, :] + b[:, None, None])) % T
    ang = n.astype(F32) * (2.0 * math.pi / T)
    sa = 1.0 / math.sqrt(A)
    m1 = jnp.concatenate([jnp.cos(ang) * sa, -jnp.sin(ang) * sa], axis=1).astype(BF16)
    kb = jnp.arange(B, dtype=jnp.int32)
    n2 = (kb[:, None] * b[None, :]) % B
    ang2 = n2.astype(F32) * (2.0 * math.pi / B)
    sb = 1.0 / math.sqrt(B)
    m2 = jnp.concatenate([jnp.cos(ang2) * sb, jnp.sin(ang2) * sb], axis=1).astype(BF16)
    return m1, m2


def _channel_dft():
    c = np.arange(FOURIER_DIM)
    ang = 2.0 * np.pi * ((c[:, None] * c[None, :]) % FOURIER_DIM) / FOURIER_DIM
    s = 1.0 / math.sqrt(FOURIER_DIM)
    return np.cos(ang) * s, np.sin(ang) * s


FFT_SUB = 8


def _f1_kernel(x_ref, m_ref, ch_ref, pm_ref, re_ref, im_ref):
    A = x_ref.shape[0]
    ch = ch_ref[...]
    x_all = x_ref[...].reshape(A * FFT_SUB, x_ref.shape[2]).astype(BF16)
    xp = _dot(pm_ref[...], x_all).astype(BF16)
    for j in range(FFT_SUB):
        y = _dot(m_ref[j], xp[j * A:(j + 1) * A])
        for h in range(2):
            sl = slice(h * FOURIER_DIM, (h + 1) * FOURIER_DIM)
            lhs = jnp.concatenate([y[:A, sl], y[A:, sl]], axis=1).astype(BF16)
            yp = _dot(lhs, ch)
            re_ref[j, :, sl] = yp[:, :FOURIER_DIM]
            im_ref[j, :, sl] = yp[:, FOURIER_DIM:]


def _f2_kernel(re_ref, im_ref, m_ref, w_ref, b_ref, o_ref):
    for j in range(FFT_SUB):
        rhs = jnp.concatenate([re_ref[:, j, :], im_ref[:, j, :]], axis=0).astype(BF16)
        spec = _dot(m_ref[...], rhs)
        outs = []
        for h in range(FOURIER_HEADS):
            sl = slice(h * FOURIER_DIM, (h + 1) * FOURIER_DIM)
            outs.append(_dot(spec[:, sl].astype(BF16), w_ref[h].astype(BF16)) + b_ref[h])
        o_ref[:, j, :] = jnp.concatenate(outs, axis=1)


def fourier_mix(z, fw, fb):
    T, NC = z.shape
    A, B = T // FFT_B, FFT_B
    m1, m2 = _dft_tables(T)
    cc, ss = _channel_dft()
    ch = jnp.asarray(np.block([[cc, -ss], [ss, cc]]), BF16)
    cb0 = COL_F // 512
    n = A * FFT_SUB
    dst = np.arange(n)
    pm = jnp.asarray((((dst % A) * FFT_SUB + dst // A)[:, None] == np.arange(n)[None, :]).astype(np.float32), BF16)
    yre, yim = pl.pallas_call(
        _f1_kernel,
        grid=(B // FFT_SUB, 2),
        in_specs=[
            pl.BlockSpec((A, FFT_SUB, 512), lambda b, c: (0, b, cb0 + c)),
            pl.BlockSpec((FFT_SUB, 2 * A, A), lambda b, c: (b, 0, 0)),
            pl.BlockSpec((512, 512), lambda b, c: (0, 0)),
            pl.BlockSpec((n, n), lambda b, c: (0, 0)),
        ],
        out_specs=[pl.BlockSpec((FFT_SUB, A, 512), lambda b, c: (b, 0, c))] * 2,
        out_shape=[jax.ShapeDtypeStruct((B, A, FOURIER_WIDTH), F32)] * 2,
        compiler_params=_cparams(("parallel", "parallel")),
    )(z.reshape(A, B, NC), m1, ch, pm)
    blk = pl.BlockSpec((B, FFT_SUB, FOURIER_WIDTH), lambda i: (0, i, 0))
    out = pl.pallas_call(
        _f2_kernel,
        grid=(A // FFT_SUB,),
        in_specs=[
            blk, blk,
            pl.BlockSpec((B, 2 * B), lambda i: (0, 0)),
            pl.BlockSpec((FOURIER_HEADS, FOURIER_DIM, FOURIER_DIM), lambda i: (0, 0, 0)),
            pl.BlockSpec((FOURIER_HEADS, 1, FOURIER_DIM), lambda i: (0, 0, 0)),
        ],
        out_specs=blk,
        out_shape=jax.ShapeDtypeStruct((B, A, FOURIER_WIDTH), F32),
        compiler_params=_cparams(("parallel",)),
    )(yre, yim, m2, fw, fb.reshape(FOURIER_HEADS, 1, FOURIER_DIM))
    return out.reshape(T, FOURIER_WIDTH)


def _fctx_kernel(f_ref, cs_ref, ts_ref, w_ref, b_ref, o_ref):
    for h in range(FOURIER_HEADS):
        sl = slice(h * FOURIER_DIM, (h + 1) * FOURIER_DIM)
        g = _dot(f_ref[:, sl].astype(BF16), cs_ref[...])
        gg = jnp.concatenate([g[:, :FOURIER_DIM], g[:, FOURIER_DIM:]], axis=0).astype(BF16)
        spec = _dot(ts_ref[...], gg)
        out = _dot(spec.astype(BF16), w_ref[h].astype(BF16)) + b_ref[h]
        o_ref[:, sl] = out.astype(o_ref.dtype)


def fourier_mix_ctx(zc, fw, fb):
    T = zc.shape[0]
    cc, ss = _channel_dft()
    cs = jnp.asarray(np.concatenate([cc, ss], axis=1), BF16)
    t = np.arange(T)
    ang = 2.0 * np.pi * ((t[:, None] * t[None, :]) % T) / T
    st = 1.0 / math.sqrt(T)
    ts = jnp.asarray(np.concatenate([np.cos(ang) * st, -np.sin(ang) * st], axis=1), BF16)
    return pl.pallas_call(
        _fctx_kernel,
        grid=(1,),
        in_specs=[
            pl.BlockSpec((T, FOURIER_WIDTH), lambda i: (0, COL_F // FOURIER_WIDTH)),
            pl.BlockSpec((FOURIER_DIM, 2 * FOURIER_DIM), lambda i: (0, 0)),
            pl.BlockSpec((T, 2 * T), lambda i: (0, 0)),
            pl.BlockSpec((FOURIER_HEADS, FOURIER_DIM, FOURIER_DIM), lambda i: (0, 0, 0)),
            pl.BlockSpec((FOURIER_HEADS, 1, FOURIER_DIM), lambda i: (0, 0, 0)),
        ],
        out_specs=pl.BlockSpec((T, FOURIER_WIDTH), lambda i: (0, 0)),
        out_shape=jax.ShapeDtypeStruct((T, FOURIER_WIDTH), BF16),
        compiler_params=_cparams(("arbitrary",)),
    )(zc, cs, ts, fw, fb.reshape(FOURIER_HEADS, 1, FOURIER_DIM))


def _rope_tables(T):
    nf = NA_HEAD_DIM // 4
    t = jnp.arange(T)
    inv = 1.0 / (ROPE_THETA ** (jnp.arange(nf, dtype=F32) / nf))
    lane = jnp.arange(NA_HEAD_DIM)
    pos = jnp.where(lane[None, :] < NA_HEAD_DIM // 2, (t // GRID_W)[:, None], (t % GRID_W)[:, None]).astype(F32)
    ang = pos * inv[lane % nf][None, :]
    sign = jnp.where((lane % (2 * nf)) < nf, -1.0, 1.0)[None, :]
    return jnp.cos(ang), jnp.sin(ang) * sign


def _head_norm_rope(x, w, cos, sin, perm):
    ms = jnp.mean(x * x, axis=-1, keepdims=True)
    y = x * lax.rsqrt(ms + NORM_EPS) * w
    if cos is None:
        return y
    return y * cos + _dot(y.astype(BF16), perm) * sin


def _qkprep_kernel(q_ref, k_ref, qw_ref, kw_ref, cos_ref, sin_ref, perm_ref, qo_ref, ko_ref, *, rope):
    cos = cos_ref[...] if rope else None
    sin = sin_ref[...] if rope else None
    perm = perm_ref[...]
    qw = qw_ref[...] * (NA_HEAD_DIM ** -0.5)
    kw = kw_ref[...]
    for h in range(4):
        sl = slice(h * LANE, (h + 1) * LANE)
        qo_ref[:, sl] = _head_norm_rope(q_ref[:, sl], qw, cos, sin, perm).astype(qo_ref.dtype)
        ko_ref[:, sl] = _head_norm_rope(k_ref[:, sl], kw, cos, sin, perm).astype(ko_ref.dtype)


def qk_prep(z, qw, kw, rope):
    T = z.shape[0]
    tm = min(512, T)
    if rope:
        cos, sin = _rope_tables(T)
    else:
        cos = sin = jnp.zeros((T, LANE), F32)
    nq = NA_WIDTH // 512
    lane = np.arange(LANE)
    src = np.where(lane % 64 < 32, lane + 32, lane - 32)
    perm = jnp.asarray((lane[:, None] == src[None, :]).astype(np.float32), BF16)
    return pl.pallas_call(
        functools.partial(_qkprep_kernel, rope=rope),
        grid=(T // tm, nq),
        in_specs=[
            pl.BlockSpec((tm, 512), lambda i, j: (i, COL_Q // 512 + j)),
            pl.BlockSpec((tm, 512), lambda i, j: (i, COL_K // 512 + j)),
            pl.BlockSpec((1, LANE), lambda i, j: (0, 0)),
            pl.BlockSpec((1, LANE), lambda i, j: (0, 0)),
            pl.BlockSpec((tm, LANE), lambda i, j: (i, 0)),
            pl.BlockSpec((tm, LANE), lambda i, j: (i, 0)),
            pl.BlockSpec((LANE, LANE), lambda i, j: (0, 0)),
        ],
        out_specs=[pl.BlockSpec((tm, 512), lambda i, j: (i, j))] * 2,
        out_shape=[jax.ShapeDtypeStruct((T, NA_WIDTH), BF16)] * 2,
        compiler_params=_cparams(("parallel", "parallel")),
    )(z, z, qw.reshape(1, LANE), kw.reshape(1, LANE), cos, sin, perm)


def _toeplitz_kernel(r_ref, e_ref, o_ref):
    r = r_ref[0]
    acc = jnp.zeros(o_ref.shape[1:], F32)
    for d in range(2 * NA_KW - 1):
        acc = acc + r[:, d:d + 1] * e_ref[d:d + 1, :]
    o_ref[0] = acc


def rpb_bias_tables(rpb):
    L, H = rpb.shape[0], rpb.shape[1]
    ndr, ndc = 2 * NA_KH - 1, 2 * NA_KW - 1
    q = np.arange(GRID_W)
    e = np.zeros((32, GRID_W, GRID_W), np.float32)
    for d in range(ndc):
        e[d] = (q[None, :] - q[:, None] + (NA_KW - 1)) == d
    e = jnp.asarray(e.reshape(32, GRID_W * GRID_W))
    rp = jnp.pad(rpb.reshape(L * H, ndr, ndc), ((0, 0), (0, 16 - ndr), (0, 32 - ndc)))
    toep = pl.pallas_call(
        _toeplitz_kernel,
        grid=(L * H,),
        in_specs=[pl.BlockSpec((1, 16, 32), lambda i: (i, 0, 0)),
                  pl.BlockSpec((32, GRID_W * GRID_W), lambda i: (0, 0))],
        out_specs=pl.BlockSpec((1, 16, GRID_W * GRID_W), lambda i: (i, 0, 0)),
        out_shape=jax.ShapeDtypeStruct((L * H, 16, GRID_W * GRID_W), F32),
        compiler_params=_cparams(("parallel",)),
    )(rp, e)
    toep = toep.reshape(L, H, 16, GRID_W, GRID_W)
    col_start = np.clip(q - NA_KW // 2, 0, GRID_W - NA_KW)
    in_win = (q[None, :] >= col_start[:, None]) & (q[None, :] < col_start[:, None] + NA_KW)
    mask = jnp.asarray(np.where(in_win, 0.0, -1e30).astype(np.float32))
    tabs = []
    for o in range(NA_KH):
        band = toep[:, :, NA_KH - 1 - o:2 * NA_KH - 1 - o]
        band = band + mask[None, None, None]
        tabs.append(jnp.transpose(band, (0, 1, 3, 2, 4)).reshape(L, H, GRID_W, NA_KH * GRID_W))
    return jnp.stack(tabs, axis=2)


NA_HEADS_PER_STEP = 2


def _na_kernel(q_ref, kp_ref, kc_ref, kn_ref, vp_ref, vc_ref, vn_ref, kx_ref, vx_ref, bt_ref, o_ref,
               kbuf, vbuf, *, nrows):
    m = pl.program_id(1)
    blk = NA_KH * GRID_W
    kbuf[0:blk] = kp_ref[...]
    kbuf[blk:2 * blk] = kc_ref[...]
    kbuf[2 * blk:3 * blk] = kn_ref[...]
    vbuf[0:blk] = vp_ref[...].astype(BF16)
    vbuf[blk:2 * blk] = vc_ref[...].astype(BF16)
    vbuf[2 * blk:3 * blk] = vn_ref[...].astype(BF16)
    kctx = kx_ref[...]
    vctx = vx_ref[...].astype(BF16)
    work, scores = [], []
    for hh in range(NA_HEADS_PER_STEP):
        hs = slice(hh * LANE, (hh + 1) * LANE)
        for j in range(NA_KH):
            r = m * NA_KH + j
            rs = jnp.clip(r - NA_KH // 2, 0, nrows - NA_KH)
            start = pl.multiple_of((rs - (m - 1) * NA_KH) * GRID_W, GRID_W)
            q = q_ref[j * GRID_W:(j + 1) * GRID_W, hs]
            s = _dot_nt(q, kbuf[pl.ds(start, blk), hs]) + bt_ref[hh, r - rs]
            work.append((hs, j, start))
            scores.append((s, _dot_nt(q, kctx[:, hs])))
    probs = []
    for s, sc in scores:
        mx = jnp.maximum(jnp.max(s, axis=-1, keepdims=True), jnp.max(sc, axis=-1, keepdims=True))
        p = jnp.exp(s - mx)
        pc = jnp.exp(sc - mx)
        den = jnp.sum(p, axis=-1, keepdims=True) + jnp.sum(pc, axis=-1, keepdims=True)
        probs.append((p.astype(BF16), pc.astype(BF16), den))
    for (hs, j, start), (p, pc, den) in zip(work, probs):
        acc = _dot(p, vbuf[pl.ds(start, blk), hs]) + _dot(pc, vctx[:, hs])
        o_ref[j * GRID_W:(j + 1) * GRID_W, hs] = (acc / den).astype(o_ref.dtype)


def na_attention(qn, kn, z, kcn, zc, bias_tab):
    T = qn.shape[0]
    C = kcn.shape[0]
    nrows = T // GRID_W
    blk = NA_KH * GRID_W
    nblk = T // blk
    hw = NA_HEADS_PER_STEP * LANE
    vcol = COL_V // hw
    prev = lambda h, m: (jnp.maximum(m - 1, 0), h)
    cur = lambda h, m: (m, h)
    nxt = lambda h, m: (jnp.minimum(m + 1, nblk - 1), h)
    vprev = lambda h, m: (jnp.maximum(m - 1, 0), vcol + h)
    vcur = lambda h, m: (m, vcol + h)
    vnxt = lambda h, m: (jnp.minimum(m + 1, nblk - 1), vcol + h)
    return pl.pallas_call(
        functools.partial(_na_kernel, nrows=nrows),
        grid=(NA_HEADS // NA_HEADS_PER_STEP, nblk),
        in_specs=[
            pl.BlockSpec((blk, hw), cur),
            pl.BlockSpec((blk, hw), prev), pl.BlockSpec((blk, hw), cur), pl.BlockSpec((blk, hw), nxt),
            pl.BlockSpec((blk, hw), vprev), pl.BlockSpec((blk, hw), vcur), pl.BlockSpec((blk, hw), vnxt),
            pl.BlockSpec((C, hw), lambda h, m: (0, h)),
            pl.BlockSpec((C, hw), lambda h, m: (0, vcol + h)),
            pl.BlockSpec((NA_HEADS_PER_STEP, NA_KH, GRID_W, blk), lambda h, m: (h, 0, 0, 0)),
        ],
        out_specs=pl.BlockSpec((blk, hw), cur),
        out_shape=jax.ShapeDtypeStruct((T, NA_WIDTH), BF16),
        scratch_shapes=[pltpu.VMEM((3 * blk, hw), BF16), pltpu.VMEM((3 * blk, hw), BF16)],
        compiler_params=_cparams(("parallel", "parallel")),
    )(qn, kn, kn, kn, z, z, z, kcn, zc, bias_tab)


def _ctxattn_kernel(q_ref, k_ref, v_ref, o_ref):
    s = _dot_nt(q_ref[...], k_ref[...])
    p = jnp.exp(s - jnp.max(s, axis=-1, keepdims=True))
    den = jnp.sum(p, axis=-1, keepdims=True)
    o_ref[...] = (_dot(p.astype(BF16), v_ref[...].astype(BF16)) / den).astype(o_ref.dtype)


def ctx_attention(qcn, kcn, zc):
    C = qcn.shape[0]
    vcol = COL_V // LANE
    return pl.pallas_call(
        _ctxattn_kernel,
        grid=(NA_HEADS,),
        in_specs=[pl.BlockSpec((C, LANE), lambda h: (0, h)), pl.BlockSpec((C, LANE), lambda h: (0, h)),
                  pl.BlockSpec((C, LANE), lambda h: (0, vcol + h))],
        out_specs=pl.BlockSpec((C, LANE), lambda h: (0, h)),
        out_shape=jax.ShapeDtypeStruct((C, NA_WIDTH), BF16),
        compiler_params=_cparams(("parallel",)),
    )(qcn, kcn, zc)


def _seg_sum64(x):
    lane = lax.broadcasted_iota(jnp.int32, x.shape, 1)
    low = lane < RWKV_HEAD_DIM
    s_lo = jnp.sum(jnp.where(low, x, 0.0), axis=-1, keepdims=True)
    s_hi = jnp.sum(jnp.where(low, 0.0, x), axis=-1, keepdims=True)
    return jnp.where(low, s_lo, s_hi)


def _seg_sum(x):
    return jnp.concatenate([_seg_sum64(x[:, i * LANE:(i + 1) * LANE]) for i in range(x.shape[1] // LANE)], axis=1)


def _rwkvprep_kernel(r_ref, rp_ref, rn_ref, k_ref, kp_ref, kn_ref, v_ref, vp_ref, vn_ref, lo_ref, cw_ref,
                     w2_ref, a2_ref, g2_ref, w0_ref, a0_ref, kk_ref, ka_ref, rk_ref,
                     r_o, v_o, kk_o, g_o, bonus_o, logw_o, kd_o, ag_o):
    i = pl.program_id(0)
    n_i = pl.num_programs(0)
    W = RWKV_WIDTH
    cw = cw_ref[...]
    r = _conv3(r_ref[...], *_halo_rows(rp_ref, rn_ref, i, n_i), cw[:, 0:W])
    k = _conv3(k_ref[...], *_halo_rows(kp_ref, kn_ref, i, n_i), cw[:, W:2 * W])
    v = _conv3(v_ref[...], *_halo_rows(vp_ref, vn_ref, i, n_i), cw[:, 2 * W:3 * W])
    lora = lo_ref[...]
    wl = _dot(jnp.tanh(lora[:, 0:2 * W_LORA]).astype(BF16), w2_ref[...])
    al = _dot(lora[:, 2 * W_LORA:2 * (W_LORA + A_LORA)].astype(BF16), a2_ref[...])
    g = _dot(jax.nn.sigmoid(lora[:, 2 * (W_LORA + A_LORA):]).astype(BF16), g2_ref[...])
    kkr = k * kk_ref[...]
    kk = kkr * lax.rsqrt(jnp.maximum(_seg_sum(kkr * kkr), 1e-24))
    kds = []
    for d in range(2):
        logw = -math.exp(-0.5) * jax.nn.sigmoid(w0_ref[d:d + 1, :] + wl[:, d * W:(d + 1) * W])
        a = jax.nn.sigmoid(a0_ref[d:d + 1, :] + al[:, d * W:(d + 1) * W])
        kd = k * (1.0 + (a - 1.0) * ka_ref[...])
        kds.append(kd)
        for gi in range(N_GROUPS):
            sl = slice(gi * GROUP, (gi + 1) * GROUP)
            logw_o[d, gi] = logw[:, sl]
            kd_o[d, gi] = kd[:, sl].astype(kd_o.dtype)
            ag_o[d, gi] = a[:, sl].astype(ag_o.dtype)
    bonus = _seg_sum(r * (kds[0] + kds[1]) * rk_ref[...]) * v
    for gi in range(N_GROUPS):
        sl = slice(gi * GROUP, (gi + 1) * GROUP)
        r_o[gi] = r[:, sl].astype(r_o.dtype)
        v_o[gi] = v[:, sl].astype(v_o.dtype)
        kk_o[gi] = kk[:, sl].astype(kk_o.dtype)
        g_o[gi] = g[:, sl].astype(g_o.dtype)
        bonus_o[gi] = bonus[:, sl].astype(bonus_o.dtype)


def rwkv_prep(z, p):
    T = z.shape[0]
    tm = min(128, T)
    W = RWKV_WIDTH
    cb = COL_R // W
    specs = []
    for c in range(3):
        specs += _halo_specs(tm, W, T, lambda j, c=c: cb + c)
    specs = [pl.BlockSpec(s.block_shape, lambda i, f=s.index_map: f(i, 0)) for s in specs]
    full = lambda shape: pl.BlockSpec(shape, lambda i: (0,) * len(shape))
    specs += [
        pl.BlockSpec((tm, LORA_PAD), lambda i: (i, COL_L // LORA_PAD)),
        full((3, 3 * W)), full((2 * W_LORA, 2 * W)), full((2 * A_LORA, 2 * W)), full((LORA_PAD - 256, W)),
        full((2, W)), full((2, W)), full((1, W)), full((1, W)), full((1, W)),
    ]
    g1 = pl.BlockSpec((N_GROUPS, tm, GROUP), lambda i: (0, i, 0))
    g2 = pl.BlockSpec((2, N_GROUPS, tm, GROUP), lambda i: (0, 0, i, 0))
    s1 = jax.ShapeDtypeStruct((N_GROUPS, T, GROUP), BF16)
    s2 = jax.ShapeDtypeStruct((2, N_GROUPS, T, GROUP), BF16)
    s2f = jax.ShapeDtypeStruct((2, N_GROUPS, T, GROUP), F32)
    return pl.pallas_call(
        _rwkvprep_kernel,
        grid=(T // tm,),
        in_specs=specs,
        out_specs=[g1] * 5 + [g2] * 3,
        out_shape=[s1] * 5 + [s2f, s2, s2],
        compiler_params=_cparams(("parallel",)),
    )(z, z, z, z, z, z, z, z, z, z, p['rwkv_conv'], p['w2bd'], p['a2bd'], p['g2p'], p['w0'], p['a0'],
      p['k_k'].reshape(1, W), p['k_a'].reshape(1, W), p['r_k'].reshape(1, W))


def _fold(x):
    c = CHUNK
    return x[0:c] + x[c:2 * c] + x[2 * c:3 * c] + x[3 * c:4 * c]


def _rwkv_kernel(r_ref, v_ref, kk_ref, logw_ref, kd_ref, ag_ref, tri_ref, ms_ref, mi_ref, bd_ref, bdf_ref, z0_ref,
                 y_ref, zf_ref, z_scr):
    forward = pl.program_id(0) == 0
    c = pl.program_id(1)

    @pl.when(c == 0)
    def _():
        z_scr[...] = z0_ref[0]

    tri = tri_ref[0]
    m_strict = ms_ref[0]
    m_incl = mi_ref[0]
    bd = bd_ref[...]
    bdf = bdf_ref[...]
    row = lax.broadcasted_iota(jnp.int32, (CHUNK, GROUP), 0)
    col = lax.broadcasted_iota(jnp.int32, (CHUNK, GROUP), 1)
    diag = row == (col % RWKV_HEAD_DIM)
    eye_f = jnp.where(diag, 1.0, 0.0)

    def expand(x):
        return jnp.concatenate([x.astype(BF16)] * 4, axis=0) * bd

    def stack(*xs):
        return jnp.concatenate([x.astype(BF16) for x in xs], axis=0)

    cs_all = _dot(tri, jnp.concatenate([h for g in range(N_GROUPS) for h in _split(logw_ref[0, g])], axis=1))

    def prep(g):
        lw = logw_ref[0, g]
        cs = cs_all[:, 2 * g * GROUP:2 * (g + 1) * GROUP]
        linc = cs[:, :GROUP] + cs[:, GROUP:]
        ltot = jnp.where(forward, linc[CHUNK - 1:CHUNK, :], linc[0:1, :])
        e_inc = jnp.exp(linc)
        e_neg = jnp.exp(-linc).astype(BF16)
        e_exc = jnp.exp(linc - lw).astype(BF16)
        e_rem = jnp.exp(ltot - linc).astype(BF16)
        kk = kk_ref[g]
        kd = kd_ref[0, g]
        b = kk * ag_ref[0, g]
        at = -(kk * e_exc)
        rt = r_ref[g].astype(F32) * e_inc
        gram = _dot_nt(stack(at, rt), jnp.concatenate([expand(b * e_neg), expand(kd * e_neg)], axis=0))
        fab = gram[:CHUNK, :GROUP] * m_strict
        return dict(g=g, rt=rt, at=at, e_tot=jnp.exp(ltot), bh=b * e_rem, kh=kd * e_rem, v=v_ref[g],
                    fp=fab, ft=eye_f + fab,
                    fak=gram[:CHUNK, GROUP:] * m_strict, frb=gram[CHUNK:, :GROUP] * m_incl,
                    frk=gram[CHUNK:, GROUP:] * m_incl)

    st = [prep(g) for g in range(N_GROUPS)]
    for s in st:
        res = _dot(stack(s['fp'], s['frb']), expand(s['fp']))
        s['fp'] = res[:CHUNK]
        s['fg'] = s['frb'] + res[CHUNK:]
    for _ in range(4):
        for s in st:
            res = _dot(stack(s['fp'], s['ft'], s['fg']), expand(s['fp']))
            s['fp'] = res[:CHUNK]
            s['ft'] = s['ft'] + res[CHUNK:2 * CHUNK]
            s['fg'] = s['fg'] + res[2 * CHUNK:]
    for s in st:
        res = _dot(stack(s['ft'], s['fg']), expand(s['fp']))
        s['ft'] = s['ft'] + res[:CHUNK]
        s['fg'] = s['fg'] + res[CHUNK:]
    for s in st:
        res = _dot(stack(s['ft'], s['fg']), jnp.concatenate([expand(s['fak']), expand(s['at'])], axis=1))
        s['fta'] = res[:CHUNK, :GROUP]
        s['ff'] = res[CHUNK:, :GROUP] + s['frk']
        s['fa1'] = res[:CHUNK, GROUP:]
        s['rp'] = s['rt'] + res[CHUNK:, GROUP:]
    for s in st:
        res = _dot(stack(s['fta'], s['ff']), expand(s['v']))
        s['fu0'] = res[:CHUNK]
        s['y0'] = res[CHUNK:]
    for s in st:
        zero = jnp.zeros((CHUNK, GROUP), BF16)
        rhs = jnp.concatenate([jnp.concatenate([s['fa1'].astype(BF16), zero], axis=1),
                               jnp.concatenate([zero, s['fu0'].astype(BF16)], axis=1),
                               jnp.concatenate([zero, s['v']], axis=1)], axis=0)
        mn = _dot_tn(stack(s['bh'], s['bh'], s['kh']), rhs)
        s['mf'] = _fold(mn[:, :GROUP] * bdf) + jnp.where(diag, s['e_tot'], 0.0)
        s['nf'] = _fold(mn[:, GROUP:] * bdf)
    for s in st:
        g = s['g']
        zhx = expand(z_scr[g])
        mh, ml = _split(s['mf'])
        res = _dot(jnp.concatenate([s['rp'].astype(BF16), mh, ml], axis=0), zhx)
        y_ref[0, g] = res[:CHUNK] + s['y0']
        z_scr[g] = res[CHUNK:2 * CHUNK] + res[2 * CHUNK:] + s['nf']

    @pl.when(c == pl.num_programs(1) - 1)
    def _():
        zf_ref[0] = z_scr[...]


def _chunk_masks():
    i = np.arange(CHUNK)
    tri = np.stack([i[None, :] <= i[:, None], i[None, :] >= i[:, None]]).astype(np.float32)
    j = np.arange(GROUP)
    same = ((j[:, None] // CHUNK) == (j[None, :] // CHUNK)).astype(np.float32)
    js = j[None, :] % CHUNK
    strict = np.stack([js < i[:, None], js > i[:, None]]).astype(np.float32)
    incl = np.stack([js <= i[:, None], js >= i[:, None]]).astype(np.float32)
    return jnp.asarray(tri, BF16), jnp.asarray(strict), jnp.asarray(incl), jnp.asarray(same, BF16), jnp.asarray(same)


def rwkv_scan(r, v, kk, logw, kd, ag, z0):
    T = r.shape[1]
    nc = T // CHUNK
    tri, strict, incl, bd, bdf = _chunk_masks()
    order = lambda d, c: c + d * (nc - 1 - 2 * c)
    b1 = pl.BlockSpec((N_GROUPS, CHUNK, GROUP), lambda d, c: (0, order(d, c), 0))
    b2 = pl.BlockSpec((1, N_GROUPS, CHUNK, GROUP), lambda d, c: (d, 0, order(d, c), 0))
    per_dir = lambda shape: pl.BlockSpec((1,) + shape, lambda d, c: (d,) + (0,) * len(shape))
    full = lambda shape: pl.BlockSpec(shape, lambda d, c: (0,) * len(shape))
    zspec = per_dir((N_GROUPS, CHUNK, GROUP))
    return pl.pallas_call(
        _rwkv_kernel,
        grid=(2, nc),
        in_specs=[b1, b1, b1, b2, b2, b2, per_dir((CHUNK, CHUNK)), per_dir((CHUNK, GROUP)), per_dir((CHUNK, GROUP)),
                  full((GROUP, GROUP)), full((GROUP, GROUP)), zspec],
        out_specs=[b2, zspec],
        out_shape=[jax.ShapeDtypeStruct((2, N_GROUPS, T, GROUP), F32),
                   jax.ShapeDtypeStruct((2, N_GROUPS, CHUNK, GROUP), F32)],
        scratch_shapes=[pltpu.VMEM((N_GROUPS, CHUNK, GROUP), F32)],
        compiler_params=_cparams(("arbitrary", "arbitrary")),
    )(r, v, kk, logw, kd, ag, tri, strict, incl, bd, bdf, z0)


def _rwkvpost_kernel(y_ref, g_ref, bonus_ref, gw_ref, gb_ref, o_ref):
    for gi in range(N_GROUPS):
        y = y_ref[0, gi] + y_ref[1, gi]
        mu = _seg_sum(y) * (1.0 / RWKV_HEAD_DIM)
        yc = y - mu
        var = _seg_sum(yc * yc) * (1.0 / RWKV_HEAD_DIM)
        sl = slice(gi * GROUP, (gi + 1) * GROUP)
        yn = yc * lax.rsqrt(var + GN_EPS) * gw_ref[:, sl] + gb_ref[:, sl]
        o_ref[:, sl] = ((yn + bonus_ref[gi].astype(F32)) * g_ref[gi].astype(F32)).astype(o_ref.dtype)


def rwkv_post(y, g, bonus, gn_w, gn_b):
    T = y.shape[2]
    tm = min(256, T)
    W = RWKV_WIDTH
    return pl.pallas_call(
        _rwkvpost_kernel,
        grid=(T // tm,),
        in_specs=[pl.BlockSpec((2, N_GROUPS, tm, GROUP), lambda i: (0, 0, i, 0)),
                  pl.BlockSpec((N_GROUPS, tm, GROUP), lambda i: (0, i, 0)),
                  pl.BlockSpec((N_GROUPS, tm, GROUP), lambda i: (0, i, 0)),
                  pl.BlockSpec((1, W), lambda i: (0, 0)), pl.BlockSpec((1, W), lambda i: (0, 0))],
        out_specs=pl.BlockSpec((tm, W), lambda i: (i, 0)),
        out_shape=jax.ShapeDtypeStruct((T, W), BF16),
        compiler_params=_cparams(("parallel",)),
    )(y, g, bonus, gn_w.reshape(1, W), gn_b.reshape(1, W))


def rwkv_mix(z, p, z0):
    r, v, kk, g, bonus, logw, kd, ag = rwkv_prep(z, p)
    y, zf = rwkv_scan(r, v, kk, logw, kd, ag, z0)
    return rwkv_post(y, g, bonus, p['gn_w'], p['gn_b']), zf


def _lora_heads(l, w2, a2, g2):
    W = RWKV_WIDTH
    zw = jnp.zeros((W_LORA, W), F32)
    w2bd = jnp.concatenate([jnp.concatenate([w2[l, 0], zw], axis=1), jnp.concatenate([zw, w2[l, 1]], axis=1)], axis=0)
    a2bd = jnp.concatenate([jnp.concatenate([a2[l, 0], zw], axis=1), jnp.concatenate([zw, a2[l, 1]], axis=1)], axis=0)
    g2p = jnp.pad(g2[l], ((0, LORA_PAD - 256 - G_LORA), (0, 0)))
    return dict(w2bd=w2bd.astype(BF16), a2bd=a2bd.astype(BF16), g2p=g2p.astype(BF16))


def _layer(x, ctx, mod_x, mod_c, p, l, bias_tab, ctx_out):
    Dm = D_MODEL
    sh1, sc1, gt1, sh2, sc2, gt2 = [mod_x[i * Dm:(i + 1) * Dm] for i in range(6)]
    csh1, csc1, cgt1, csh2, csc2, cgt2 = [mod_c[i * Dm:(i + 1) * Dm] for i in range(6)]
    zx = norm_mod_matmul(x, p['norm1_w'], sc1, sh1, p['w_in'], l)
    zc = norm_mod_matmul(ctx, p['norm1_w'], csc1, csh1, p['w_in'], l)
    fx = fourier_mix(zx, p['fourier_w'], p['fourier_b'])
    qx, kx = qk_prep(zx, p['q_norm_w'], p['k_norm_w'], rope=True)
    qc, kc = qk_prep(zc, p['q_norm_w'], p['k_norm_w'], rope=False)
    ax = na_attention(qx, kx, zx, kc, zc, bias_tab)
    z0 = jnp.zeros((2, N_GROUPS, CHUNK, GROUP), F32)
    rc, zf = rwkv_mix(zc, p, z0)
    rx, _ = rwkv_mix(zx, p, zf)
    x = matmul_residual([fx, ax, rx], p['w_out'], l, x, gt1)
    u = norm_mod_matmul(x, p['norm2_w'], sc2, sh2, p['w_ffn_in'], l, out_dtype=BF16)
    x = matmul_residual([conv_gate(u, p['ffn_conv'])], p['w_ffn_out'], l, x, gt2)
    if ctx_out:
        fc = fourier_mix_ctx(zc, p['fourier_w'], p['fourier_b'])
        ac = ctx_attention(qc, kc, zc)
        ctx = matmul_residual([fc, ac, rc], p['w_out'], l, ctx, cgt1)
        uc = norm_mod_matmul(ctx, p['norm2_w'], csc2, csh2, p['w_ffn_in'], l, out_dtype=BF16)
        ctx = matmul_residual([conv_gate(uc, p['ffn_conv'])], p['w_ffn_out'], l, ctx, cgt2)
    return x, ctx


def kernel(x, c, ctx, c_ctx, ada_w, ada_b, norm1_w, norm2_w, w_in, fourier_w, fourier_b, q_norm_w, k_norm_w, rpb,
           rwkv_conv, w0, w2, a0, a2, g2, k_k, k_a, r_k, gn_w, gn_b, w_out, ffn_conv, w_ffn_in, w_ffn_out):
    L = ada_w.shape[0]
    xs = x[0]
    cs = ctx[0]
    mods = ada_mod(jnp.concatenate([c, c_ctx[None, :]], axis=0), ada_w, ada_b)
    bias_tabs = rpb_bias_tables(rpb)
    big = dict(w_in=cast_w_in(w_in), w_out=cast_bf16(w_out), w_ffn_in=cast_bf16(w_ffn_in), w_ffn_out=cast_bf16(w_ffn_out))
    for l in range(L):
        p = dict(big)
        p.update(_lora_heads(l, w2, a2, g2))
        p.update(norm1_w=norm1_w[l], norm2_w=norm2_w[l], fourier_w=fourier_w[l], fourier_b=fourier_b[l],
                 q_norm_w=q_norm_w[l], k_norm_w=k_norm_w[l], rwkv_conv=rwkv_conv[l], w0=w0[l], a0=a0[l],
                 k_k=k_k[l], k_a=k_a[l], r_k=r_k[l], gn_w=gn_w[l], gn_b=gn_b[l], ffn_conv=ffn_conv[l])
        xs, cs = _layer(xs, cs, mods[l, 0], mods[l, 1], p, l, bias_tabs[l], l < L - 1)
    return xs[None]
```

```python
import functools
import math

import jax
import jax.numpy as jnp
import numpy as np
from jax import lax
from jax.experimental import pallas as pl
from jax.experimental.pallas import tpu as pltpu

F32 = jnp.float32
BF16 = jnp.bfloat16

D_MODEL = 4096
GRID_W = 64
FOURIER_WIDTH = 1024
FOURIER_HEADS = 4
FOURIER_DIM = 256
NA_WIDTH = 1536
NA_HEAD_DIM = 128
NA_HEADS = 12
NA_KH = 8
NA_KW = 16
RWKV_WIDTH = 1536
RWKV_HEAD_DIM = 64
W_LORA = 64
A_LORA = 64
G_LORA = 224
D_FF = 5120
ROPE_THETA = 10000.0
NORM_EPS = 1e-6
GN_EPS = 64e-5

COL_Q = 0
COL_K = NA_WIDTH
COL_V = 2 * NA_WIDTH
COL_R = 3 * NA_WIDTH
COL_F = 6 * NA_WIDTH
COL_L = COL_F + FOURIER_WIDTH
LORA_PAD = 512
IN_COLS_PAD = COL_L + LORA_PAD

CHUNK = 64
GROUP = 256
N_GROUPS = RWKV_WIDTH // GROUP
FFT_B = 128

LANE = 128
VMEM_LIMIT = 48 * 1024 * 1024


def _cparams(sem):
    return pltpu.CompilerParams(dimension_semantics=sem, vmem_limit_bytes=VMEM_LIMIT)


def _dot(a, b):
    return jnp.dot(a, b, preferred_element_type=F32)


def _dot_nt(a, b):
    return lax.dot_general(a, b, (((1,), (1,)), ((), ())), preferred_element_type=F32)


def _dot_tn(a, b):
    return lax.dot_general(a, b, (((0,), (0,)), ((), ())), preferred_element_type=F32)


def _split(x):
    hi = x.astype(BF16)
    lo = (x - hi.astype(F32)).astype(BF16)
    return hi, lo


def _ada_kernel(s_ref, w_ref, b_ref, o_ref, acc_ref):
    k = pl.program_id(2)
    tk, tn = w_ref.shape[1], w_ref.shape[2]
    rep = tn // LANE

    @pl.when(k == 0)
    def _():
        acc_ref[...] = jnp.zeros_like(acc_ref)

    def body(i, carry):
        a0, a1 = carry
        r = pl.multiple_of(i * 8, 8)
        w = w_ref[0, pl.ds(r, 8), :]
        s0 = s_ref[0, pl.ds(r, 8), :]
        s1 = s_ref[1, pl.ds(r, 8), :]
        s0 = s0 * jax.nn.sigmoid(s0)
        s1 = s1 * jax.nn.sigmoid(s1)
        a0 = a0 + w * jnp.concatenate([s0] * rep, axis=1)
        a1 = a1 + w * jnp.concatenate([s1] * rep, axis=1)
        return a0, a1

    a0, a1 = lax.fori_loop(0, tk // 8, body, (acc_ref[0], acc_ref[1]), unroll=4)
    acc_ref[0] = a0
    acc_ref[1] = a1

    @pl.when(k == pl.num_programs(2) - 1)
    def _():
        o_ref[0, 0:1, :] = jnp.sum(a0, axis=0, keepdims=True) + b_ref[0]
        o_ref[0, 1:2, :] = jnp.sum(a1, axis=0, keepdims=True) + b_ref[0]


def ada_mod(cc, ada_w, ada_b):
    L, K, N = ada_w.shape
    tk, tn = 2048, 1024
    s_b = jnp.broadcast_to(cc[:, :, None], (2, K, LANE))
    return pl.pallas_call(
        _ada_kernel,
        grid=(L, N // tn, K // tk),
        in_specs=[
            pl.BlockSpec((2, tk, LANE), lambda l, j, k: (0, k, 0)),
            pl.BlockSpec((1, tk, tn), lambda l, j, k: (l, k, j)),
            pl.BlockSpec((1, 1, tn), lambda l, j, k: (l, 0, j)),
        ],
        out_specs=pl.BlockSpec((1, 2, tn), lambda l, j, k: (l, 0, j)),
        out_shape=jax.ShapeDtypeStruct((L, 2, N), F32),
        scratch_shapes=[pltpu.VMEM((2, 8, tn), F32)],
        compiler_params=_cparams(("parallel", "parallel", "arbitrary")),
    )(s_b, ada_w, ada_b.reshape(L, 1, N))


def _normmod_kernel(x_ref, nw_ref, sc_ref, sh_ref, o_ref):
    x = x_ref[...]
    ms = jnp.mean(x * x, axis=-1, keepdims=True)
    y = x * lax.rsqrt(ms + NORM_EPS) * nw_ref[...]
    o_ref[...] = (y * (1.0 + sc_ref[...]) + sh_ref[...]).astype(o_ref.dtype)


LORA_COLS = 2 * W_LORA + 2 * A_LORA + G_LORA
W_IN_BLOCKS = IN_COLS_PAD // 512
W_IN_FRONT = FOURIER_WIDTH // 512
W_IN_MID = W_IN_BLOCKS - 1 - W_IN_FRONT
CAST_ROWS, CAST_COLS = 256, 1024
WIN_TK = 1024


def _w_in_src_block(j):
    return jnp.where(j < W_IN_MID, j + W_IN_FRONT, jnp.where(j < W_IN_BLOCKS - 1, j - W_IN_MID, W_IN_BLOCKS - 1))


def _cast_block(s_ref, so_ref, kind, block):
    if kind == 'plain':
        so_ref[...] = s_ref[...].astype(BF16)
    else:
        row = lax.broadcasted_iota(jnp.int32, s_ref.shape[1:], 0)
        keep = (block % W_IN_BLOCKS < W_IN_BLOCKS - 1) | (row < LORA_COLS)
        so_ref[0] = jnp.where(keep, s_ref[0], 0.0).T.astype(BF16)


def _cast_specs(kind, src, l, block_of):
    if kind == 'plain':
        _, K, N = src.shape
        ncb = N // CAST_COLS
        nblocks = (K // CAST_ROWS) * ncb
        blk = (1, CAST_ROWS, CAST_COLS)
        in_spec = pl.BlockSpec(blk, lambda *g: (l, block_of(*g) // ncb, block_of(*g) % ncb))
        out_spec = pl.BlockSpec(blk, lambda *g: (0, block_of(*g) // ncb, block_of(*g) % ncb))
        return src, in_spec, out_spec, jax.ShapeDtypeStruct((1, K, N), BF16), nblocks
    _, K, _ = src.shape
    nblocks = (K // WIN_TK) * W_IN_BLOCKS
    in_spec = pl.BlockSpec((1, 512, WIN_TK), lambda *g: (l, _w_in_src_block(block_of(*g) % W_IN_BLOCKS),
                                                         block_of(*g) // W_IN_BLOCKS))
    out_spec = pl.BlockSpec((1, WIN_TK, 512), lambda *g: (0, block_of(*g) // W_IN_BLOCKS, block_of(*g) % W_IN_BLOCKS))
    return jnp.swapaxes(src, 1, 2), in_spec, out_spec, jax.ShapeDtypeStruct((1, K, IN_COLS_PAD), BF16), nblocks


def _castonly_kernel(s_ref, so_ref, *, kind):
    _cast_block(s_ref, so_ref, kind, pl.program_id(0))


def cast_layer(kind, w, l):
    src, in_spec, out_spec, out_shape, nblocks = _cast_specs(kind, w, l, lambda b: b)
    return pl.pallas_call(
        functools.partial(_castonly_kernel, kind=kind),
        grid=(nblocks,),
        in_specs=[in_spec],
        out_specs=out_spec,
        out_shape=out_shape,
        compiler_params=_cparams(("parallel",)),
    )(src)


def _side_cast_specs(side, grid):
    kind, w, l = side
    nb = _cast_specs(kind, w, l, lambda b: b)[4]
    assert grid[0] * grid[1] >= nb, "the matmul grid has too few steps to cast every block of the weight"
    return _cast_specs(kind, w, l, lambda i, j: jnp.minimum(i * grid[1] + j, nb - 1)) + (kind,)


def _mm_kernel(a_ref, w_ref, *rest, side_kind, side_blocks):
    if side_kind is None:
        (o_ref,) = rest
    else:
        s_ref, o_ref, so_ref = rest
        step = pl.program_id(0) * pl.num_programs(1) + pl.program_id(1)
        _cast_block(s_ref, so_ref, side_kind, jnp.minimum(step, side_blocks - 1))
    o_ref[...] = _dot(a_ref[...], w_ref[0]).astype(o_ref.dtype)


def norm_mod_matmul(x, nw, sc, sh, w, l, out_dtype=F32, side=None):
    M, K = x.shape
    N = w.shape[2]
    tr = min(512, M)
    vec = pl.BlockSpec((1, K), lambda i: (0, 0))
    h = pl.pallas_call(
        _normmod_kernel,
        grid=(M // tr,),
        in_specs=[pl.BlockSpec((tr, K), lambda i: (i, 0)), vec, vec, vec],
        out_specs=pl.BlockSpec((tr, K), lambda i: (i, 0)),
        out_shape=jax.ShapeDtypeStruct((M, K), BF16),
        compiler_params=_cparams(("parallel",)),
    )(x, nw.reshape(1, K), sc.reshape(1, K), sh.reshape(1, K))
    tm = min(1024, M)
    tn = next(t for t in (1024, 768, 512) if N % t == 0)
    grid = (M // tm, N // tn)
    args = [h, w]
    in_specs = [
        pl.BlockSpec((tm, K), lambda i, j: (i, 0)),
        pl.BlockSpec((1, K, tn), lambda i, j: (l, 0, j)),
    ]
    out_specs = [pl.BlockSpec((tm, tn), lambda i, j: (i, j))]
    out_shape = [jax.ShapeDtypeStruct((M, N), out_dtype)]
    side_kind, side_blocks = None, 0
    if side is not None:
        src, s_in, s_out, s_shape, side_blocks, side_kind = _side_cast_specs(side, grid)
        args.append(src)
        in_specs.append(s_in)
        out_specs.append(s_out)
        out_shape.append(s_shape)
    outs = pl.pallas_call(
        functools.partial(_mm_kernel, side_kind=side_kind, side_blocks=side_blocks),
        grid=grid,
        in_specs=in_specs,
        out_specs=out_specs,
        out_shape=out_shape,
        compiler_params=_cparams(("arbitrary", "arbitrary") if side is not None else ("parallel", "arbitrary")),
    )(*args)
    return outs[0] if side is None else tuple(outs)


def _mmres_kernel(*refs, ksplits, side_kind, side_blocks):
    n = len(ksplits)
    a_refs = refs[:n]
    if side_kind is None:
        w_ref, x_ref, g_ref, o_ref = refs[n:]
    else:
        w_ref, x_ref, g_ref, s_ref, o_ref, so_ref = refs[n:]
        step = pl.program_id(0) * pl.num_programs(1) + pl.program_id(1)
        _cast_block(s_ref, so_ref, side_kind, jnp.minimum(step, side_blocks - 1))
    acc = None
    off = 0
    for a_ref, kp in zip(a_refs, ksplits):
        part = _dot(a_ref[...].astype(BF16), w_ref[0, off:off + kp, :])
        acc = part if acc is None else acc + part
        off += kp
    o_ref[...] = x_ref[...] + g_ref[...] * acc


def matmul_residual(parts, w, l, x, gate, side=None):
    M, N = x.shape
    K = w.shape[1]
    ksplits = tuple(p.shape[1] for p in parts)
    assert sum(ksplits) == K
    tm = min(1024, M)
    tn = 512
    grid = (M // tm, N // tn)
    args = list(parts) + [w, x, gate.reshape(1, N)]
    in_specs = [pl.BlockSpec((tm, kp), lambda i, j: (i, 0)) for kp in ksplits]
    in_specs += [
        pl.BlockSpec((1, K, tn), lambda i, j: (l, 0, j)),
        pl.BlockSpec((tm, tn), lambda i, j: (i, j)),
        pl.BlockSpec((1, tn), lambda i, j: (0, j)),
    ]
    out_specs = [pl.BlockSpec((tm, tn), lambda i, j: (i, j))]
    out_shape = [jax.ShapeDtypeStruct((M, N), F32)]
    side_kind, side_blocks = None, 0
    if side is not None:
        src, s_in, s_out, s_shape, side_blocks, side_kind = _side_cast_specs(side, grid)
        args.append(src)
        in_specs.append(s_in)
        out_specs.append(s_out)
        out_shape.append(s_shape)
    outs = pl.pallas_call(
        functools.partial(_mmres_kernel, ksplits=ksplits, side_kind=side_kind, side_blocks=side_blocks),
        grid=grid,
        in_specs=in_specs,
        out_specs=out_specs,
        out_shape=out_shape,
        compiler_params=_cparams(("arbitrary", "arbitrary") if side is not None else ("parallel", "arbitrary")),
    )(*args)
    return outs[0] if side is None else tuple(outs)


def _conv3(main, prev_row, next_row, w):
    tm = main.shape[0]
    row = lax.broadcasted_iota(jnp.int32, main.shape, 0)
    dn = jnp.where(row == 0, prev_row, pltpu.roll(main, 1, axis=0))
    up = jnp.where(row == tm - 1, next_row, pltpu.roll(main, tm - 1, axis=0))
    return dn * w[0:1, :] + main * w[1:2, :] + up * w[2:3, :]


def _halo_rows(prev_ref, next_ref, i, n_i):
    prev = prev_ref[...].astype(F32)
    nxt = next_ref[...].astype(F32)
    hr = prev.shape[0]
    prev_row = jnp.where(i == 0, 0.0, prev[hr - 1:hr, :])
    next_row = jnp.where(i == n_i - 1, 0.0, nxt[0:1, :])
    return prev_row, next_row


def _halo_specs(tm, tn, n_rows, col_fn, hr=8):
    rb = tm // hr
    last = n_rows // hr - 1
    return [
        pl.BlockSpec((tm, tn), lambda i, j: (i, col_fn(j))),
        pl.BlockSpec((hr, tn), lambda i, j: (jnp.maximum(i * rb - 1, 0), col_fn(j))),
        pl.BlockSpec((hr, tn), lambda i, j: (jnp.minimum((i + 1) * rb, last), col_fn(j))),
    ]


def _convgate_kernel(a_ref, ap_ref, an_ref, b_ref, bp_ref, bn_ref, wa_ref, wb_ref, o_ref):
    i = pl.program_id(0)
    n_i = pl.num_programs(0)
    pa, na = _halo_rows(ap_ref, an_ref, i, n_i)
    pb, nb = _halo_rows(bp_ref, bn_ref, i, n_i)
    a = _conv3(a_ref[...].astype(F32), pa, na, wa_ref[...])
    b = _conv3(b_ref[...].astype(F32), pb, nb, wb_ref[...])
    o_ref[...] = (a * jax.nn.sigmoid(a) * b).astype(o_ref.dtype)


def conv_gate(u, conv_w):
    M, N2 = u.shape
    F = N2 // 2
    tm = min(512, M)
    tn = 1024
    nb = F // tn
    specs = _halo_specs(tm, tn, M, lambda j: j, 16) + _halo_specs(tm, tn, M, lambda j: j + nb, 16)
    specs += [pl.BlockSpec((3, tn), lambda i, j: (0, j)), pl.BlockSpec((3, tn), lambda i, j: (0, j + nb))]
    return pl.pallas_call(
        _convgate_kernel,
        grid=(M // tm, nb),
        in_specs=specs,
        out_specs=pl.BlockSpec((tm, tn), lambda i, j: (i, j)),
        out_shape=jax.ShapeDtypeStruct((M, F), BF16),
        compiler_params=_cparams(("parallel", "parallel")),
    )(u, u, u, u, u, u, conv_w, conv_w)


def _dft_tables(T):
    A, B = T // FFT_B, FFT_B
    ka = jnp.arange(A, dtype=jnp.int32)
    a = jnp.arange(A, dtype=jnp.int32)
    b = jnp.arange(B, dtype=jnp.int32)
    n = (ka[None, :, None] * (B * a[None, None, :] + b[:, None, None])) % T
    ang = n.astype(F32) * (2.0 * math.pi / T)
    sa = 1.0 / math.sqrt(A)
    m1 = jnp.concatenate([jnp.cos(ang) * sa, -jnp.sin(ang) * sa], axis=1).astype(BF16)
    kb = jnp.arange(B, dtype=jnp.int32)
    n2 = (kb[:, None] * b[None, :]) % B
    ang2 = n2.astype(F32) * (2.0 * math.pi / B)
    sb = 1.0 / math.sqrt(B)
    m2 = jnp.concatenate([jnp.cos(ang2) * sb, jnp.sin(ang2) * sb], axis=1).astype(BF16)
    return m1, m2


def _channel_dft():
    c = np.arange(FOURIER_DIM)
    ang = 2.0 * np.pi * ((c[:, None] * c[None, :]) % FOURIER_DIM) / FOURIER_DIM
    s = 1.0 / math.sqrt(FOURIER_DIM)
    return np.cos(ang) * s, np.sin(ang) * s


FFT_SUB = 8


def _f1_kernel(x_ref, m_ref, ch_ref, pm_ref, re_ref, im_ref):
    A = x_ref.shape[0]
    ch = ch_ref[...]
    x_all = x_ref[...].reshape(A * FFT_SUB, x_ref.shape[2]).astype(BF16)
    xp = _dot(pm_ref[...], x_all).astype(BF16)
    for j in range(FFT_SUB):
        y = _dot(m_ref[j], xp[j * A:(j + 1) * A])
        for h in range(2):
            sl = slice(h * FOURIER_DIM, (h + 1) * FOURIER_DIM)
            lhs = jnp.concatenate([y[:A, sl], y[A:, sl]], axis=1).astype(BF16)
            yp = _dot(lhs, ch)
            re_ref[j, :, sl] = yp[:, :FOURIER_DIM]
            im_ref[j, :, sl] = yp[:, FOURIER_DIM:]


def _f2_kernel(re_ref, im_ref, m_ref, w_ref, b_ref, o_ref):
    for j in range(FFT_SUB):
        rhs = jnp.concatenate([re_ref[:, j, :], im_ref[:, j, :]], axis=0).astype(BF16)
        spec = _dot(m_ref[...], rhs)
        outs = []
        for h in range(FOURIER_HEADS):
            sl = slice(h * FOURIER_DIM, (h + 1) * FOURIER_DIM)
            outs.append(_dot(spec[:, sl].astype(BF16), w_ref[h].astype(BF16)) + b_ref[h])
        o_ref[:, j, :] = jnp.concatenate(outs, axis=1)


def fourier_mix(z, fw, fb):
    T, NC = z.shape
    A, B = T // FFT_B, FFT_B
    m1, m2 = _dft_tables(T)
    cc, ss = _channel_dft()
    ch = jnp.asarray(np.block([[cc, -ss], [ss, cc]]), BF16)
    cb0 = COL_F // 512
    n = A * FFT_SUB
    dst = np.arange(n)
    pm = jnp.asarray((((dst % A) * FFT_SUB + dst // A)[:, None] == np.arange(n)[None, :]).astype(np.float32), BF16)
    yre, yim = pl.pallas_call(
        _f1_kernel,
        grid=(B // FFT_SUB, 2),
        in_specs=[
            pl.BlockSpec((A, FFT_SUB, 512), lambda b, c: (0, b, cb0 + c)),
            pl.BlockSpec((FFT_SUB, 2 * A, A), lambda b, c: (b, 0, 0)),
            pl.BlockSpec((512, 512), lambda b, c: (0, 0)),
            pl.BlockSpec((n, n), lambda b, c: (0, 0)),
        ],
        out_specs=[pl.BlockSpec((FFT_SUB, A, 512), lambda b, c: (b, 0, c))] * 2,
        out_shape=[jax.ShapeDtypeStruct((B, A, FOURIER_WIDTH), F32)] * 2,
        compiler_params=_cparams(("parallel", "parallel")),
    )(z.reshape(A, B, NC), m1, ch, pm)
    blk = pl.BlockSpec((B, FFT_SUB, FOURIER_WIDTH), lambda i: (0, i, 0))
    out = pl.pallas_call(
        _f2_kernel,
        grid=(A // FFT_SUB,),
        in_specs=[
            blk, blk,
            pl.BlockSpec((B, 2 * B), lambda i: (0, 0)),
            pl.BlockSpec((FOURIER_HEADS, FOURIER_DIM, FOURIER_DIM), lambda i: (0, 0, 0)),
            pl.BlockSpec((FOURIER_HEADS, 1, FOURIER_DIM), lambda i: (0, 0, 0)),
        ],
        out_specs=blk,
        out_shape=jax.ShapeDtypeStruct((B, A, FOURIER_WIDTH), F32),
        compiler_params=_cparams(("parallel",)),
    )(yre, yim, m2, fw, fb.reshape(FOURIER_HEADS, 1, FOURIER_DIM))
    return out.reshape(T, FOURIER_WIDTH)


def _fctx_kernel(f_ref, cs_ref, ts_ref, w_ref, b_ref, o_ref):
    for h in range(FOURIER_HEADS):
        sl = slice(h * FOURIER_DIM, (h + 1) * FOURIER_DIM)
        g = _dot(f_ref[:, sl].astype(BF16), cs_ref[...])
        gg = jnp.concatenate([g[:, :FOURIER_DIM], g[:, FOURIER_DIM:]], axis=0).astype(BF16)
        spec = _dot(ts_ref[...], gg)
        out = _dot(spec.astype(BF16), w_ref[h].astype(BF16)) + b_ref[h]
        o_ref[:, sl] = out.astype(o_ref.dtype)


def fourier_mix_ctx(zc, fw, fb):
    T = zc.shape[0]
    cc, ss = _channel_dft()
    cs = jnp.asarray(np.concatenate([cc, ss], axis=1), BF16)
    t = np.arange(T)
    ang = 2.0 * np.pi * ((t[:, None] * t[None, :]) % T) / T
    st = 1.0 / math.sqrt(T)
    ts = jnp.asarray(np.concatenate([np.cos(ang) * st, -np.sin(ang) * st], axis=1), BF16)
    return pl.pallas_call(
        _fctx_kernel,
        grid=(1,),
        in_specs=[
            pl.BlockSpec((T, FOURIER_WIDTH), lambda i: (0, COL_F // FOURIER_WIDTH)),
            pl.BlockSpec((FOURIER_DIM, 2 * FOURIER_DIM), lambda i: (0, 0)),
            pl.BlockSpec((T, 2 * T), lambda i: (0, 0)),
            pl.BlockSpec((FOURIER_HEADS, FOURIER_DIM, FOURIER_DIM), lambda i: (0, 0, 0)),
            pl.BlockSpec((FOURIER_HEADS, 1, FOURIER_DIM), lambda i: (0, 0, 0)),
        ],
        out_specs=pl.BlockSpec((T, FOURIER_WIDTH), lambda i: (0, 0)),
        out_shape=jax.ShapeDtypeStruct((T, FOURIER_WIDTH), BF16),
        compiler_params=_cparams(("arbitrary",)),
    )(zc, cs, ts, fw, fb.reshape(FOURIER_HEADS, 1, FOURIER_DIM))


def _rope_tables(T):
    nf = NA_HEAD_DIM // 4
    t = jnp.arange(T)
    inv = 1.0 / (ROPE_THETA ** (jnp.arange(nf, dtype=F32) / nf))
    lane = jnp.arange(NA_HEAD_DIM)
    pos = jnp.where(lane[None, :] < NA_HEAD_DIM // 2, (t // GRID_W)[:, None], (t % GRID_W)[:, None]).astype(F32)
    ang = pos * inv[lane % nf][None, :]
    sign = jnp.where((lane % (2 * nf)) < nf, -1.0, 1.0)[None, :]
    return jnp.cos(ang), jnp.sin(ang) * sign


def _head_norm_rope(x, w, cos, sin, perm):
    ms = jnp.mean(x * x, axis=-1, keepdims=True)
    y = x * lax.rsqrt(ms + NORM_EPS) * w
    if cos is None:
        return y
    return y * cos + _dot(y.astype(BF16), perm) * sin


def _qkprep_kernel(q_ref, k_ref, qw_ref, kw_ref, cos_ref, sin_ref, perm_ref, qo_ref, ko_ref, *, rope):
    cos = cos_ref[...] if rope else None
    sin = sin_ref[...] if rope else None
    perm = perm_ref[...]
    qw = qw_ref[...] * (NA_HEAD_DIM ** -0.5)
    kw = kw_ref[...]
    for h in range(4):
        sl = slice(h * LANE, (h + 1) * LANE)
        qo_ref[:, sl] = _head_norm_rope(q_ref[:, sl], qw, cos, sin, perm).astype(qo_ref.dtype)
        ko_ref[:, sl] = _head_norm_rope(k_ref[:, sl], kw, cos, sin, perm).astype(ko_ref.dtype)


def qk_prep(z, qw, kw, rope):
    T = z.shape[0]
    tm = min(512, T)
    if rope:
        cos, sin = _rope_tables(T)
    else:
        cos = sin = jnp.zeros((T, LANE), F32)
    nq = NA_WIDTH // 512
    lane = np.arange(LANE)
    src = np.where(lane % 64 < 32, lane + 32, lane - 32)
    perm = jnp.asarray((lane[:, None] == src[None, :]).astype(np.float32), BF16)
    return pl.pallas_call(
        functools.partial(_qkprep_kernel, rope=rope),
        grid=(T // tm, nq),
        in_specs=[
            pl.BlockSpec((tm, 512), lambda i, j: (i, COL_Q // 512 + j)),
            pl.BlockSpec((tm, 512), lambda i, j: (i, COL_K // 512 + j)),
            pl.BlockSpec((1, LANE), lambda i, j: (0, 0)),
            pl.BlockSpec((1, LANE), lambda i, j: (0, 0)),
            pl.BlockSpec((tm, LANE), lambda i, j: (i, 0)),
            pl.BlockSpec((tm, LANE), lambda i, j: (i, 0)),
            pl.BlockSpec((LANE, LANE), lambda i, j: (0, 0)),
        ],
        out_specs=[pl.BlockSpec((tm, 512), lambda i, j: (i, j))] * 2,
        out_shape=[jax.ShapeDtypeStruct((T, NA_WIDTH), BF16)] * 2,
        compiler_params=_cparams(("parallel", "parallel")),
    )(z, z, qw.reshape(1, LANE), kw.reshape(1, LANE), cos, sin, perm)


def _toeplitz_kernel(r_ref, e_ref, o_ref):
    r = r_ref[0]
    acc = jnp.zeros(o_ref.shape[1:], F32)
    for d in range(2 * NA_KW - 1):
        acc = acc + r[:, d:d + 1] * e_ref[d:d + 1, :]
    o_ref[0] = acc


def rpb_bias_tables(rpb):
    L, H = rpb.shape[0], rpb.shape[1]
    ndr, ndc = 2 * NA_KH - 1, 2 * NA_KW - 1
    q = np.arange(GRID_W)
    e = np.zeros((32, GRID_W, GRID_W), np.float32)
    for d in range(ndc):
        e[d] = (q[None, :] - q[:, None] + (NA_KW - 1)) == d
    e = jnp.asarray(e.reshape(32, GRID_W * GRID_W))
    rp = jnp.pad(rpb.reshape(L * H, ndr, ndc), ((0, 0), (0, 16 - ndr), (0, 32 - ndc)))
    toep = pl.pallas_call(
        _toeplitz_kernel,
        grid=(L * H,),
        in_specs=[pl.BlockSpec((1, 16, 32), lambda i: (i, 0, 0)),
                  pl.BlockSpec((32, GRID_W * GRID_W), lambda i: (0, 0))],
        out_specs=pl.BlockSpec((1, 16, GRID_W * GRID_W), lambda i: (i, 0, 0)),
        out_shape=jax.ShapeDtypeStruct((L * H, 16, GRID_W * GRID_W), F32),
        compiler_params=_cparams(("parallel",)),
    )(rp, e)
    toep = toep.reshape(L, H, 16, GRID_W, GRID_W)
    col_start = np.clip(q - NA_KW // 2, 0, GRID_W - NA_KW)
    in_win = (q[None, :] >= col_start[:, None]) & (q[None, :] < col_start[:, None] + NA_KW)
    mask = jnp.asarray(np.where(in_win, 0.0, -1e30).astype(np.float32))
    tabs = []
    for o in range(NA_KH):
        band = toep[:, :, NA_KH - 1 - o:2 * NA_KH - 1 - o]
        band = band + mask[None, None, None]
        tabs.append(jnp.transpose(band, (0, 1, 3, 2, 4)).reshape(L, H, GRID_W, NA_KH * GRID_W))
    return jnp.stack(tabs, axis=2)


NA_HEADS_PER_STEP = 2


def _na_kernel(q_ref, kp_ref, kc_ref, kn_ref, vp_ref, vc_ref, vn_ref, kx_ref, vx_ref, bt_ref, o_ref,
               kbuf, vbuf, *, nrows):
    m = pl.program_id(1)
    blk = NA_KH * GRID_W
    kbuf[0:blk] = kp_ref[...]
    kbuf[blk:2 * blk] = kc_ref[...]
    kbuf[2 * blk:3 * blk] = kn_ref[...]
    vbuf[0:blk] = vp_ref[...].astype(BF16)
    vbuf[blk:2 * blk] = vc_ref[...].astype(BF16)
    vbuf[2 * blk:3 * blk] = vn_ref[...].astype(BF16)
    kctx = kx_ref[...]
    vctx = vx_ref[...].astype(BF16)
    work, scores = [], []
    for hh in range(NA_HEADS_PER_STEP):
        hs = slice(hh * LANE, (hh + 1) * LANE)
        for j in range(NA_KH):
            r = m * NA_KH + j
            rs = jnp.clip(r - NA_KH // 2, 0, nrows - NA_KH)
            start = pl.multiple_of((rs - (m - 1) * NA_KH) * GRID_W, GRID_W)
            q = q_ref[j * GRID_W:(j + 1) * GRID_W, hs]
            s = _dot_nt(q, kbuf[pl.ds(start, blk), hs]) + bt_ref[hh, r - rs]
            work.append((hs, j, start))
            scores.append((s, _dot_nt(q, kctx[:, hs])))
    probs = []
    for s, sc in scores:
        mx = jnp.maximum(jnp.max(s, axis=-1, keepdims=True), jnp.max(sc, axis=-1, keepdims=True))
        p = jnp.exp(s - mx)
        pc = jnp.exp(sc - mx)
        den = jnp.sum(p, axis=-1, keepdims=True) + jnp.sum(pc, axis=-1, keepdims=True)
        probs.append((p.astype(BF16), pc.astype(BF16), den))
    for (hs, j, start), (p, pc, den) in zip(work, probs):
        acc = _dot(p, vbuf[pl.ds(start, blk), hs]) + _dot(pc, vctx[:, hs])
        o_ref[j * GRID_W:(j + 1) * GRID_W, hs] = (acc / den).astype(o_ref.dtype)


def na_attention(qn, kn, z, kcn, zc, bias_tab):
    T = qn.shape[0]
    C = kcn.shape[0]
    nrows = T // GRID_W
    blk = NA_KH * GRID_W
    nblk = T // blk
    hw = NA_HEADS_PER_STEP * LANE
    vcol = COL_V // hw
    prev = lambda h, m: (jnp.maximum(m - 1, 0), h)
    cur = lambda h, m: (m, h)
    nxt = lambda h, m: (jnp.minimum(m + 1, nblk - 1), h)
    vprev = lambda h, m: (jnp.maximum(m - 1, 0), vcol + h)
    vcur = lambda h, m: (m, vcol + h)
    vnxt = lambda h, m: (jnp.minimum(m + 1, nblk - 1), vcol + h)
    return pl.pallas_call(
        functools.partial(_na_kernel, nrows=nrows),
        grid=(NA_HEADS // NA_HEADS_PER_STEP, nblk),
        in_specs=[
            pl.BlockSpec((blk, hw), cur),
            pl.BlockSpec((blk, hw), prev), pl.BlockSpec((blk, hw), cur), pl.BlockSpec((blk, hw), nxt),
            pl.BlockSpec((blk, hw), vprev), pl.BlockSpec((blk, hw), vcur), pl.BlockSpec((blk, hw), vnxt),
            pl.BlockSpec((C, hw), lambda h, m: (0, h)),
            pl.BlockSpec((C, hw), lambda h, m: (0, vcol + h)),
            pl.BlockSpec((NA_HEADS_PER_STEP, NA_KH, GRID_W, blk), lambda h, m: (h, 0, 0, 0)),
        ],
        out_specs=pl.BlockSpec((blk, hw), cur),
        out_shape=jax.ShapeDtypeStruct((T, NA_WIDTH), BF16),
        scratch_shapes=[pltpu.VMEM((3 * blk, hw), BF16), pltpu.VMEM((3 * blk, hw), BF16)],
        compiler_params=_cparams(("parallel", "parallel")),
    )(qn, kn, kn, kn, z, z, z, kcn, zc, bias_tab)


def _ctxattn_kernel(q_ref, k_ref, v_ref, o_ref):
    s = _dot_nt(q_ref[...], k_ref[...])
    p = jnp.exp(s - jnp.max(s, axis=-1, keepdims=True))
    den = jnp.sum(p, axis=-1, keepdims=True)
    o_ref[...] = (_dot(p.astype(BF16), v_ref[...].astype(BF16)) / den).astype(o_ref.dtype)


def ctx_attention(qcn, kcn, zc):
    C = qcn.shape[0]
    vcol = COL_V // LANE
    return pl.pallas_call(
        _ctxattn_kernel,
        grid=(NA_HEADS,),
        in_specs=[pl.BlockSpec((C, LANE), lambda h: (0, h)), pl.BlockSpec((C, LANE), lambda h: (0, h)),
                  pl.BlockSpec((C, LANE), lambda h: (0, vcol + h))],
        out_specs=pl.BlockSpec((C, LANE), lambda h: (0, h)),
        out_shape=jax.ShapeDtypeStruct((C, NA_WIDTH), BF16),
        compiler_params=_cparams(("parallel",)),
    )(qcn, kcn, zc)


def _seg_sum64(x):
    lane = lax.broadcasted_iota(jnp.int32, x.shape, 1)
    low = lane < RWKV_HEAD_DIM
    s_lo = jnp.sum(jnp.where(low, x, 0.0), axis=-1, keepdims=True)
    s_hi = jnp.sum(jnp.where(low, 0.0, x), axis=-1, keepdims=True)
    return jnp.where(low, s_lo, s_hi)


def _seg_sum(x):
    return jnp.concatenate([_seg_sum64(x[:, i * LANE:(i + 1) * LANE]) for i in range(x.shape[1] // LANE)], axis=1)


def _rwkvprep_kernel(r_ref, rp_ref, rn_ref, k_ref, kp_ref, kn_ref, v_ref, vp_ref, vn_ref, lo_ref, cw_ref,
                     w2_ref, a2_ref, g2_ref, w0_ref, a0_ref, kk_ref, ka_ref, rk_ref,
                     r_o, v_o, kk_o, g_o, bonus_o, logw_o, kd_o, ag_o):
    i = pl.program_id(0)
    n_i = pl.num_programs(0)
    W = RWKV_WIDTH
    cw = cw_ref[...]
    r = _conv3(r_ref[...], *_halo_rows(rp_ref, rn_ref, i, n_i), cw[:, 0:W])
    k = _conv3(k_ref[...], *_halo_rows(kp_ref, kn_ref, i, n_i), cw[:, W:2 * W])
    v = _conv3(v_ref[...], *_halo_rows(vp_ref, vn_ref, i, n_i), cw[:, 2 * W:3 * W])
    lora = lo_ref[...]
    wl = _dot(jnp.tanh(lora[:, 0:2 * W_LORA]).astype(BF16), w2_ref[...])
    al = _dot(lora[:, 2 * W_LORA:2 * (W_LORA + A_LORA)].astype(BF16), a2_ref[...])
    g = _dot(jax.nn.sigmoid(lora[:, 2 * (W_LORA + A_LORA):]).astype(BF16), g2_ref[...])
    kkr = k * kk_ref[...]
    kk = kkr * lax.rsqrt(jnp.maximum(_seg_sum(kkr * kkr), 1e-24))
    kds = []
    for d in range(2):
        logw = -math.exp(-0.5) * jax.nn.sigmoid(w0_ref[d:d + 1, :] + wl[:, d * W:(d + 1) * W])
        a = jax.nn.sigmoid(a0_ref[d:d + 1, :] + al[:, d * W:(d + 1) * W])
        kd = k * (1.0 + (a - 1.0) * ka_ref[...])
        kds.append(kd)
        for gi in range(N_GROUPS):
            sl = slice(gi * GROUP, (gi + 1) * GROUP)
            logw_o[d, gi] = logw[:, sl]
            kd_o[d, gi] = kd[:, sl].astype(kd_o.dtype)
            ag_o[d, gi] = a[:, sl].astype(ag_o.dtype)
    bonus = _seg_sum(r * (kds[0] + kds[1]) * rk_ref[...]) * v
    for gi in range(N_GROUPS):
        sl = slice(gi * GROUP, (gi + 1) * GROUP)
        r_o[gi] = r[:, sl].astype(r_o.dtype)
        v_o[gi] = v[:, sl].astype(v_o.dtype)
        kk_o[gi] = kk[:, sl].astype(kk_o.dtype)
        g_o[gi] = g[:, sl].astype(g_o.dtype)
        bonus_o[gi] = bonus[:, sl].astype(bonus_o.dtype)


def rwkv_prep(z, p):
    T = z.shape[0]
    tm = min(128, T)
    W = RWKV_WIDTH
    cb = COL_R // W
    specs = []
    for c in range(3):
        specs += _halo_specs(tm, W, T, lambda j, c=c: cb + c)
    specs = [pl.BlockSpec(s.block_shape, lambda i, f=s.index_map: f(i, 0)) for s in specs]
    full = lambda shape: pl.BlockSpec(shape, lambda i: (0,) * len(shape))
    specs += [
        pl.BlockSpec((tm, LORA_PAD), lambda i: (i, COL_L // LORA_PAD)),
        full((3, 3 * W)), full((2 * W_LORA, 2 * W)), full((2 * A_LORA, 2 * W)), full((LORA_PAD - 256, W)),
        full((2, W)), full((2, W)), full((1, W)), full((1, W)), full((1, W)),
    ]
    g1 = pl.BlockSpec((N_GROUPS, tm, GROUP), lambda i: (0, i, 0))
    g2 = pl.BlockSpec((2, N_GROUPS, tm, GROUP), lambda i: (0, 0, i, 0))
    s1 = jax.ShapeDtypeStruct((N_GROUPS, T, GROUP), BF16)
    s2 = jax.ShapeDtypeStruct((2, N_GROUPS, T, GROUP), BF16)
    s2f = jax.ShapeDtypeStruct((2, N_GROUPS, T, GROUP), F32)
    return pl.pallas_call(
        _rwkvprep_kernel,
        grid=(T // tm,),
        in_specs=specs,
        out_specs=[g1] * 5 + [g2] * 3,
        out_shape=[s1] * 5 + [s2f, s2, s2],
        compiler_params=_cparams(("parallel",)),
    )(z, z, z, z, z, z, z, z, z, z, p['rwkv_conv'], p['w2bd'], p['a2bd'], p['g2p'], p['w0'], p['a0'],
      p['k_k'].reshape(1, W), p['k_a'].reshape(1, W), p['r_k'].reshape(1, W))


def _fold(x):
    c = CHUNK
    return x[0:c] + x[c:2 * c] + x[2 * c:3 * c] + x[3 * c:4 * c]


def _rwkv_kernel(r_ref, v_ref, kk_ref, logw_ref, kd_ref, ag_ref, tri_ref, ms_ref, mi_ref, bd_ref, bdf_ref, z0_ref,
                 y_ref, zf_ref, z_scr):
    forward = pl.program_id(0) == 0
    c = pl.program_id(1)

    @pl.when(c == 0)
    def _():
        z_scr[...] = z0_ref[0]

    tri = tri_ref[0]
    m_strict = ms_ref[0]
    m_incl = mi_ref[0]
    bd = bd_ref[...]
    bdf = bdf_ref[...]
    row = lax.broadcasted_iota(jnp.int32, (CHUNK, GROUP), 0)
    col = lax.broadcasted_iota(jnp.int32, (CHUNK, GROUP), 1)
    diag = row == (col % RWKV_HEAD_DIM)
    eye_f = jnp.where(diag, 1.0, 0.0)

    def expand(x):
        return jnp.concatenate([x.astype(BF16)] * 4, axis=0) * bd

    def stack(*xs):
        return jnp.concatenate([x.astype(BF16) for x in xs], axis=0)

    cs_all = _dot(tri, jnp.concatenate([h for g in range(N_GROUPS) for h in _split(logw_ref[0, g])], axis=1))

    def prep(g):
        lw = logw_ref[0, g]
        cs = cs_all[:, 2 * g * GROUP:2 * (g + 1) * GROUP]
        linc = cs[:, :GROUP] + cs[:, GROUP:]
        ltot = jnp.where(forward, linc[CHUNK - 1:CHUNK, :], linc[0:1, :])
        e_inc = jnp.exp(linc)
        e_neg = jnp.exp(-linc).astype(BF16)
        e_exc = jnp.exp(linc - lw).astype(BF16)
        e_rem = jnp.exp(ltot - linc).astype(BF16)
        kk = kk_ref[g]
        kd = kd_ref[0, g]
        b = kk * ag_ref[0, g]
        at = -(kk * e_exc)
        rt = r_ref[g].astype(F32) * e_inc
        gram = _dot_nt(stack(at, rt), jnp.concatenate([expand(b * e_neg), expand(kd * e_neg)], axis=0))
        fab = gram[:CHUNK, :GROUP] * m_strict
        return dict(g=g, rt=rt, at=at, e_tot=jnp.exp(ltot), bh=b * e_rem, kh=kd * e_rem, v=v_ref[g],
                    fp=fab, ft=eye_f + fab,
                    fak=gram[:CHUNK, GROUP:] * m_strict, frb=gram[CHUNK:, :GROUP] * m_incl,
                    frk=gram[CHUNK:, GROUP:] * m_incl)

    st = [prep(g) for g in range(N_GROUPS)]
    for s in st:
        res = _dot(stack(s['fp'], s['frb']), expand(s['fp']))
        s['fp'] = res[:CHUNK]
        s['fg'] = s['frb'] + res[CHUNK:]
    for _ in range(4):
        for s in st:
            res = _dot(stack(s['fp'], s['ft'], s['fg']), expand(s['fp']))
            s['fp'] = res[:CHUNK]
            s['ft'] = s['ft'] + res[CHUNK:2 * CHUNK]
            s['fg'] = s['fg'] + res[2 * CHUNK:]
    for s in st:
        res = _dot(stack(s['ft'], s['fg']), expand(s['fp']))
        s['ft'] = s['ft'] + res[:CHUNK]
        s['fg'] = s['fg'] + res[CHUNK:]
    for s in st:
        res = _dot(stack(s['ft'], s['fg']), jnp.concatenate([expand(s['fak']), expand(s['at'])], axis=1))
        s['fta'] = res[:CHUNK, :GROUP]
        s['ff'] = res[CHUNK:, :GROUP] + s['frk']
        s['fa1'] = res[:CHUNK, GROUP:]
        s['rp'] = s['rt'] + res[CHUNK:, GROUP:]
    for s in st:
        res = _dot(stack(s['fta'], s['ff']), expand(s['v']))
        s['fu0'] = res[:CHUNK]
        s['y0'] = res[CHUNK:]
    for s in st:
        zero = jnp.zeros((CHUNK, GROUP), BF16)
        rhs = jnp.concatenate([jnp.concatenate([s['fa1'].astype(BF16), zero], axis=1),
                               jnp.concatenate([zero, s['fu0'].astype(BF16)], axis=1),
                               jnp.concatenate([zero, s['v']], axis=1)], axis=0)
        mn = _dot_tn(stack(s['bh'], s['bh'], s['kh']), rhs)
        s['mf'] = _fold(mn[:, :GROUP] * bdf) + jnp.where(diag, s['e_tot'], 0.0)
        s['nf'] = _fold(mn[:, GROUP:] * bdf)
    for s in st:
        g = s['g']
        zhx = expand(z_scr[g])
        mh, ml = _split(s['mf'])
        res = _dot(jnp.concatenate([s['rp'].astype(BF16), mh, ml], axis=0), zhx)
        y_ref[0, g] = res[:CHUNK] + s['y0']
        z_scr[g] = res[CHUNK:2 * CHUNK] + res[2 * CHUNK:] + s['nf']

    @pl.when(c == pl.num_programs(1) - 1)
    def _():
        zf_ref[0] = z_scr[...]


def _chunk_masks():
    i = np.arange(CHUNK)
    tri = np.stack([i[None, :] <= i[:, None], i[None, :] >= i[:, None]]).astype(np.float32)
    j = np.arange(GROUP)
    same = ((j[:, None] // CHUNK) == (j[None, :] // CHUNK)).astype(np.float32)
    js = j[None, :] % CHUNK
    strict = np.stack([js < i[:, None], js > i[:, None]]).astype(np.float32)
    incl = np.stack([js <= i[:, None], js >= i[:, None]]).astype(np.float32)
    return jnp.asarray(tri, BF16), jnp.asarray(strict), jnp.asarray(incl), jnp.asarray(same, BF16), jnp.asarray(same)


def rwkv_scan(r, v, kk, logw, kd, ag, z0):
    T = r.shape[1]
    nc = T // CHUNK
    tri, strict, incl, bd, bdf = _chunk_masks()
    order = lambda d, c: c + d * (nc - 1 - 2 * c)
    b1 = pl.BlockSpec((N_GROUPS, CHUNK, GROUP), lambda d, c: (0, order(d, c), 0))
    b2 = pl.BlockSpec((1, N_GROUPS, CHUNK, GROUP), lambda d, c: (d, 0, order(d, c), 0))
    per_dir = lambda shape: pl.BlockSpec((1,) + shape, lambda d, c: (d,) + (0,) * len(shape))
    full = lambda shape: pl.BlockSpec(shape, lambda d, c: (0,) * len(shape))
    zspec = per_dir((N_GROUPS, CHUNK, GROUP))
    return pl.pallas_call(
        _rwkv_kernel,
        grid=(2, nc),
        in_specs=[b1, b1, b1, b2, b2, b2, per_dir((CHUNK, CHUNK)), per_dir((CHUNK, GROUP)), per_dir((CHUNK, GROUP)),
                  full((GROUP, GROUP)), full((GROUP, GROUP)), zspec],
        out_specs=[b2, zspec],
        out_shape=[jax.ShapeDtypeStruct((2, N_GROUPS, T, GROUP), F32),
                   jax.ShapeDtypeStruct((2, N_GROUPS, CHUNK, GROUP), F32)],
        scratch_shapes=[pltpu.VMEM((N_GROUPS, CHUNK, GROUP), F32)],
        compiler_params=_cparams(("arbitrary", "arbitrary")),
    )(r, v, kk, logw, kd, ag, tri, strict, incl, bd, bdf, z0)


def _rwkvpost_kernel(y_ref, g_ref, bonus_ref, gw_ref, gb_ref, o_ref):
    for gi in range(N_GROUPS):
        y = y_ref[0, gi] + y_ref[1, gi]
        mu = _seg_sum(y) * (1.0 / RWKV_HEAD_DIM)
        yc = y - mu
        var = _seg_sum(yc * yc) * (1.0 / RWKV_HEAD_DIM)
        sl = slice(gi * GROUP, (gi + 1) * GROUP)
        yn = yc * lax.rsqrt(var + GN_EPS) * gw_ref[:, sl] + gb_ref[:, sl]
        o_ref[:, sl] = ((yn + bonus_ref[gi].astype(F32)) * g_ref[gi].astype(F32)).astype(o_ref.dtype)


def rwkv_post(y, g, bonus, gn_w, gn_b):
    T = y.shape[2]
    tm = min(256, T)
    W = RWKV_WIDTH
    return pl.pallas_call(
        _rwkvpost_kernel,
        grid=(T // tm,),
        in_specs=[pl.BlockSpec((2, N_GROUPS, tm, GROUP), lambda i: (0, 0, i, 0)),
                  pl.BlockSpec((N_GROUPS, tm, GROUP), lambda i: (0, i, 0)),
                  pl.BlockSpec((N_GROUPS, tm, GROUP), lambda i: (0, i, 0)),
                  pl.BlockSpec((1, W), lambda i: (0, 0)), pl.BlockSpec((1, W), lambda i: (0, 0))],
        out_specs=pl.BlockSpec((tm, W), lambda i: (i, 0)),
        out_shape=jax.ShapeDtypeStruct((T, W), BF16),
        compiler_params=_cparams(("parallel",)),
    )(y, g, bonus, gn_w.reshape(1, W), gn_b.reshape(1, W))


def rwkv_mix(z, p, z0):
    r, v, kk, g, bonus, logw, kd, ag = rwkv_prep(z, p)
    y, zf = rwkv_scan(r, v, kk, logw, kd, ag, z0)
    return rwkv_post(y, g, bonus, p['gn_w'], p['gn_b']), zf


def _lora_heads(l, w2, a2, g2):
    W = RWKV_WIDTH
    zw = jnp.zeros((W_LORA, W), F32)
    w2bd = jnp.concatenate([jnp.concatenate([w2[l, 0], zw], axis=1), jnp.concatenate([zw, w2[l, 1]], axis=1)], axis=0)
    a2bd = jnp.concatenate([jnp.concatenate([a2[l, 0], zw], axis=1), jnp.concatenate([zw, a2[l, 1]], axis=1)], axis=0)
    g2p = jnp.pad(g2[l], ((0, LORA_PAD - 256 - G_LORA), (0, 0)))
    return dict(w2bd=w2bd.astype(BF16), a2bd=a2bd.astype(BF16), g2p=g2p.astype(BF16))


BIG_WEIGHTS = (('w_in', 'win'), ('w_out', 'plain'), ('w_ffn_in', 'plain'), ('w_ffn_out', 'plain'))


def _layer(x, ctx, mod_x, mod_c, p, bias_tab, ctx_out, nxt):
    Dm = D_MODEL
    l = 0
    cast = {}

    def run(fn, name, *args, **kw):
        if nxt is None:
            return fn(*args, **kw)
        out, cast[name] = fn(*args, side=nxt[name], **kw)
        return out

    sh1, sc1, gt1, sh2, sc2, gt2 = [mod_x[i * Dm:(i + 1) * Dm] for i in range(6)]
    csh1, csc1, cgt1, csh2, csc2, cgt2 = [mod_c[i * Dm:(i + 1) * Dm] for i in range(6)]
    zx = run(norm_mod_matmul, 'w_in', x, p['norm1_w'], sc1, sh1, p['w_in'], l)
    zc = norm_mod_matmul(ctx, p['norm1_w'], csc1, csh1, p['w_in'], l)
    fx = fourier_mix(zx, p['fourier_w'], p['fourier_b'])
    qx, kx = qk_prep(zx, p['q_norm_w'], p['k_norm_w'], rope=True)
    qc, kc = qk_prep(zc, p['q_norm_w'], p['k_norm_w'], rope=False)
    ax = na_attention(qx, kx, zx, kc, zc, bias_tab)
    z0 = jnp.zeros((2, N_GROUPS, CHUNK, GROUP), F32)
    rc, zf = rwkv_mix(zc, p, z0)
    rx, _ = rwkv_mix(zx, p, zf)
    x = run(matmul_residual, 'w_out', [fx, ax, rx], p['w_out'], l, x, gt1)
    u = run(norm_mod_matmul, 'w_ffn_in', x, p['norm2_w'], sc2, sh2, p['w_ffn_in'], l, out_dtype=BF16)
    x = run(matmul_residual, 'w_ffn_out', [conv_gate(u, p['ffn_conv'])], p['w_ffn_out'], l, x, gt2)
    if ctx_out:
        fc = fourier_mix_ctx(zc, p['fourier_w'], p['fourier_b'])
        ac = ctx_attention(qc, kc, zc)
        ctx = matmul_residual([fc, ac, rc], p['w_out'], l, ctx, cgt1)
        uc = norm_mod_matmul(ctx, p['norm2_w'], csc2, csh2, p['w_ffn_in'], l, out_dtype=BF16)
        ctx = matmul_residual([conv_gate(uc, p['ffn_conv'])], p['w_ffn_out'], l, ctx, cgt2)
    return x, ctx, cast


def kernel(x, c, ctx, c_ctx, ada_w, ada_b, norm1_w, norm2_w, w_in, fourier_w, fourier_b, q_norm_w, k_norm_w, rpb,
           rwkv_conv, w0, w2, a0, a2, g2, k_k, k_a, r_k, gn_w, gn_b, w_out, ffn_conv, w_ffn_in, w_ffn_out):
    L = ada_w.shape[0]
    xs = x[0]
    cs = ctx[0]
    mods = ada_mod(jnp.concatenate([c, c_ctx[None, :]], axis=0), ada_w, ada_b)
    bias_tabs = rpb_bias_tables(rpb)
    stacked = dict(w_in=w_in, w_out=w_out, w_ffn_in=w_ffn_in, w_ffn_out=w_ffn_out)
    big = {name: cast_layer(kind, stacked[name], 0) for name, kind in BIG_WEIGHTS}
    for l in range(L):
        p = dict(big)
        p.update(_lora_heads(l, w2, a2, g2))
        p.update(norm1_w=norm1_w[l], norm2_w=norm2_w[l], fourier_w=fourier_w[l], fourier_b=fourier_b[l],
                 q_norm_w=q_norm_w[l], k_norm_w=k_norm_w[l], rwkv_conv=rwkv_conv[l], w0=w0[l], a0=a0[l],
                 k_k=k_k[l], k_a=k_a[l], r_k=r_k[l], gn_w=gn_w[l], gn_b=gn_b[l], ffn_conv=ffn_conv[l])
        nxt = {name: (kind, stacked[name], l + 1) for name, kind in BIG_WEIGHTS} if l < L - 1 else None
        xs, cs, big = _layer(xs, cs, mods[l, 0], mods[l, 1], p, bias_tabs[l], l < L - 1, nxt)
    return xs[None]
```

```python
import functools
import math

import jax
import jax.numpy as jnp
import numpy as np
from jax import lax
from jax.experimental import pallas as pl
from jax.experimental.pallas import tpu as pltpu

F32 = jnp.float32
BF16 = jnp.bfloat16

D_MODEL = 4096
GRID_W = 64
FOURIER_WIDTH = 1024
FOURIER_HEADS = 4
FOURIER_DIM = 256
NA_WIDTH = 1536
NA_HEAD_DIM = 128
NA_HEADS = 12
NA_KH = 8
NA_KW = 16
RWKV_WIDTH = 1536
RWKV_HEAD_DIM = 64
W_LORA = 64
A_LORA = 64
G_LORA = 224
D_FF = 5120
ROPE_THETA = 10000.0
NORM_EPS = 1e-6
GN_EPS = 64e-5

COL_Q = 0
COL_K = NA_WIDTH
COL_V = 2 * NA_WIDTH
COL_R = 3 * NA_WIDTH
COL_F = 6 * NA_WIDTH
COL_L = COL_F + FOURIER_WIDTH
LORA_PAD = 512
IN_COLS_PAD = COL_L + LORA_PAD

CHUNK = 64
GROUP = 256
N_GROUPS = RWKV_WIDTH // GROUP
FFT_B = 128

LANE = 128
VMEM_LIMIT = 48 * 1024 * 1024


def _cparams(sem):
    return pltpu.CompilerParams(dimension_semantics=sem, vmem_limit_bytes=VMEM_LIMIT)


def _dot(a, b):
    return jnp.dot(a, b, preferred_element_type=F32)


def _dot_nt(a, b):
    return lax.dot_general(a, b, (((1,), (1,)), ((), ())), preferred_element_type=F32)


def _dot_tn(a, b):
    return lax.dot_general(a, b, (((0,), (0,)), ((), ())), preferred_element_type=F32)


def _split(x):
    hi = x.astype(BF16)
    lo = (x - hi.astype(F32)).astype(BF16)
    return hi, lo


def _ada_kernel(s_ref, w_ref, b_ref, o_ref, acc_ref):
    k = pl.program_id(2)
    tk, tn = w_ref.shape[1], w_ref.shape[2]
    rep = tn // LANE

    @pl.when(k == 0)
    def _():
        acc_ref[...] = jnp.zeros_like(acc_ref)

    def body(i, carry):
        a0, a1 = carry
        r = pl.multiple_of(i * 8, 8)
        rs = pl.multiple_of(k * tk + i * 8, 8)
        w = w_ref[0, pl.ds(r, 8), :]
        s0 = s_ref[0, pl.ds(rs, 8), :]
        s1 = s_ref[1, pl.ds(rs, 8), :]
        s0 = s0 * jax.nn.sigmoid(s0)
        s1 = s1 * jax.nn.sigmoid(s1)
        a0 = a0 + w * jnp.concatenate([s0] * rep, axis=1)
        a1 = a1 + w * jnp.concatenate([s1] * rep, axis=1)
        return a0, a1

    a0, a1 = lax.fori_loop(0, tk // 8, body, (acc_ref[0], acc_ref[1]), unroll=4)
    acc_ref[0] = a0
    acc_ref[1] = a1

    @pl.when(k == pl.num_programs(2) - 1)
    def _():
        o_ref[0, 0:1, :] = jnp.sum(a0, axis=0, keepdims=True) + b_ref[0]
        o_ref[0, 1:2, :] = jnp.sum(a1, axis=0, keepdims=True) + b_ref[0]


def ada_mod(cc, ada_w, ada_b):
    L, K, N = ada_w.shape
    tk, tn = 2048, 1024
    s_b = jnp.broadcast_to(cc[:, :, None], (2, K, LANE))
    return pl.pallas_call(
        _ada_kernel,
        grid=(L, N // tn, K // tk),
        in_specs=[
            pl.BlockSpec((2, K, LANE), lambda l, j, k: (0, 0, 0)),
            pl.BlockSpec((1, tk, tn), lambda l, j, k: (l, k, j)),
            pl.BlockSpec((1, 1, tn), lambda l, j, k: (l, 0, j)),
        ],
        out_specs=pl.BlockSpec((1, 2, tn), lambda l, j, k: (l, 0, j)),
        out_shape=jax.ShapeDtypeStruct((L, 2, N), F32),
        scratch_shapes=[pltpu.VMEM((2, 8, tn), F32)],
        compiler_params=_cparams(("parallel", "parallel", "arbitrary")),
    )(s_b, ada_w, ada_b.reshape(L, 1, N))


def _normmod_kernel(x_ref, nw_ref, sc_ref, sh_ref, o_ref):
    x = x_ref[...]
    ms = jnp.mean(x * x, axis=-1, keepdims=True)
    y = x * lax.rsqrt(ms + NORM_EPS) * nw_ref[...]
    o_ref[...] = (y * (1.0 + sc_ref[...]) + sh_ref[...]).astype(o_ref.dtype)


LORA_COLS = 2 * W_LORA + 2 * A_LORA + G_LORA
W_IN_BLOCKS = IN_COLS_PAD // 512
W_IN_FRONT = FOURIER_WIDTH // 512
W_IN_MID = W_IN_BLOCKS - 1 - W_IN_FRONT
CAST_ROWS, CAST_COLS = 256, 1024
WIN_TK = 1024


def _w_in_src_block(j):
    return jnp.where(j < W_IN_MID, j + W_IN_FRONT, jnp.where(j < W_IN_BLOCKS - 1, j - W_IN_MID, W_IN_BLOCKS - 1))


def _cast_block(s_ref, so_ref, kind, block):
    if kind == 'plain':
        so_ref[...] = s_ref[...].astype(BF16)
    else:
        row = lax.broadcasted_iota(jnp.int32, s_ref.shape[1:], 0)
        keep = (block % W_IN_BLOCKS < W_IN_BLOCKS - 1) | (row < LORA_COLS)
        so_ref[0] = jnp.where(keep, s_ref[0], 0.0).T.astype(BF16)


def _cast_blocks(kind, src, rows):
    K = src.shape[1]
    return (K // rows) * (src.shape[2] // CAST_COLS if kind == 'plain' else W_IN_BLOCKS)


def _cast_specs(kind, src, l, block_of, rows):
    nblocks = _cast_blocks(kind, src, rows)
    if kind == 'plain':
        _, K, N = src.shape
        ncb = N // CAST_COLS
        blk = (1, rows, CAST_COLS)
        in_spec = pl.BlockSpec(blk, lambda *g: (l, block_of(*g) // ncb, block_of(*g) % ncb))
        out_spec = pl.BlockSpec(blk, lambda *g: (0, block_of(*g) // ncb, block_of(*g) % ncb))
        return src, in_spec, out_spec, jax.ShapeDtypeStruct((1, K, N), BF16), nblocks
    _, K, _ = src.shape
    in_spec = pl.BlockSpec((1, 512, rows), lambda *g: (l, _w_in_src_block(block_of(*g) % W_IN_BLOCKS),
                                                       block_of(*g) // W_IN_BLOCKS))
    out_spec = pl.BlockSpec((1, rows, 512), lambda *g: (0, block_of(*g) // W_IN_BLOCKS, block_of(*g) % W_IN_BLOCKS))
    return jnp.swapaxes(src, 1, 2), in_spec, out_spec, jax.ShapeDtypeStruct((1, K, IN_COLS_PAD), BF16), nblocks


def _castonly_kernel(s_ref, so_ref, *, kind):
    _cast_block(s_ref, so_ref, kind, pl.program_id(0))


def cast_layer(kind, w, l):
    src, in_spec, out_spec, out_shape, nblocks = _cast_specs(kind, w, l, lambda b: b, WIN_TK)
    return pl.pallas_call(
        functools.partial(_castonly_kernel, kind=kind),
        grid=(nblocks,),
        in_specs=[in_spec],
        out_specs=out_spec,
        out_shape=out_shape,
        compiler_params=_cparams(("parallel",)),
    )(src)


def _side_cast_specs(side, grid):
    kind, w, l = side
    steps = grid[0] * grid[1]
    K = w.shape[1]
    start = CAST_ROWS if kind == 'plain' else WIN_TK
    rows = next((r for r in (start * 2 ** e for e in range(8)) if r <= K and K % r == 0
                 and _cast_blocks(kind, w, r) <= steps), None)
    assert rows is not None, "the matmul grid has too few steps to cast every block of the weight"
    nb = _cast_blocks(kind, w, rows)
    return _cast_specs(kind, w, l, lambda i, j: jnp.minimum(i * grid[1] + j, nb - 1), rows) + (kind,)


def _mm_kernel(a_ref, w_ref, *rest, side_kind, side_blocks):
    if side_kind is None:
        (o_ref,) = rest
    else:
        s_ref, o_ref, so_ref = rest
        step = pl.program_id(0) * pl.num_programs(1) + pl.program_id(1)
        _cast_block(s_ref, so_ref, side_kind, jnp.minimum(step, side_blocks - 1))
    o_ref[...] = _dot(a_ref[...], w_ref[0]).astype(o_ref.dtype)


def norm_mod_matmul(x, nw, sc, sh, w, l, out_dtype=F32, side=None):
    M, K = x.shape
    N = w.shape[2]
    tr = min(512, M)
    vec = pl.BlockSpec((1, K), lambda i: (0, 0))
    h = pl.pallas_call(
        _normmod_kernel,
        grid=(M // tr,),
        in_specs=[pl.BlockSpec((tr, K), lambda i: (i, 0)), vec, vec, vec],
        out_specs=pl.BlockSpec((tr, K), lambda i: (i, 0)),
        out_shape=jax.ShapeDtypeStruct((M, K), BF16),
        compiler_params=_cparams(("parallel",)),
    )(x, nw.reshape(1, K), sc.reshape(1, K), sh.reshape(1, K))
    tm = min(1024, M)
    tn = next(t for t in (1024, 768, 512) if N % t == 0)
    grid = (M // tm, N // tn)
    args = [h, w]
    in_specs = [
        pl.BlockSpec((tm, K), lambda i, j: (i, 0)),
        pl.BlockSpec((1, K, tn), lambda i, j: (l, 0, j)),
    ]
    out_specs = [pl.BlockSpec((tm, tn), lambda i, j: (i, j))]
    out_shape = [jax.ShapeDtypeStruct((M, N), out_dtype)]
    side_kind, side_blocks = None, 0
    if side is not None:
        src, s_in, s_out, s_shape, side_blocks, side_kind = _side_cast_specs(side, grid)
        args.append(src)
        in_specs.append(s_in)
        out_specs.append(s_out)
        out_shape.append(s_shape)
    outs = pl.pallas_call(
        functools.partial(_mm_kernel, side_kind=side_kind, side_blocks=side_blocks),
        grid=grid,
        in_specs=in_specs,
        out_specs=out_specs,
        out_shape=out_shape,
        compiler_params=_cparams(("arbitrary", "arbitrary") if side is not None else ("parallel", "arbitrary")),
    )(*args)
    return outs[0] if side is None else tuple(outs)


def _mmres_kernel(*refs, ksplits, side_kind, side_blocks):
    n = len(ksplits)
    a_refs = refs[:n]
    if side_kind is None:
        w_ref, x_ref, g_ref, o_ref = refs[n:]
    else:
        w_ref, x_ref, g_ref, s_ref, o_ref, so_ref = refs[n:]
        step = pl.program_id(0) * pl.num_programs(1) + pl.program_id(1)
        _cast_block(s_ref, so_ref, side_kind, jnp.minimum(step, side_blocks - 1))
    acc = None
    off = 0
    for a_ref, kp in zip(a_refs, ksplits):
        part = _dot(a_ref[...].astype(BF16), w_ref[0, off:off + kp, :])
        acc = part if acc is None else acc + part
        off += kp
    o_ref[...] = x_ref[...] + g_ref[...] * acc


def matmul_residual(parts, w, l, x, gate, side=None):
    M, N = x.shape
    K = w.shape[1]
    ksplits = tuple(p.shape[1] for p in parts)
    assert sum(ksplits) == K
    tm = min(1024, M)
    tn = 512
    grid = (M // tm, N // tn)
    args = list(parts) + [w, x, gate.reshape(1, N)]
    in_specs = [pl.BlockSpec((tm, kp), lambda i, j: (i, 0)) for kp in ksplits]
    in_specs += [
        pl.BlockSpec((1, K, tn), lambda i, j: (l, 0, j)),
        pl.BlockSpec((tm, tn), lambda i, j: (i, j)),
        pl.BlockSpec((1, tn), lambda i, j: (0, j)),
    ]
    out_specs = [pl.BlockSpec((tm, tn), lambda i, j: (i, j))]
    out_shape = [jax.ShapeDtypeStruct((M, N), F32)]
    side_kind, side_blocks = None, 0
    if side is not None:
        src, s_in, s_out, s_shape, side_blocks, side_kind = _side_cast_specs(side, grid)
        args.append(src)
        in_specs.append(s_in)
        out_specs.append(s_out)
        out_shape.append(s_shape)
    outs = pl.pallas_call(
        functools.partial(_mmres_kernel, ksplits=ksplits, side_kind=side_kind, side_blocks=side_blocks),
        grid=grid,
        in_specs=in_specs,
        out_specs=out_specs,
        out_shape=out_shape,
        compiler_params=_cparams(("arbitrary", "arbitrary") if side is not None else ("parallel", "arbitrary")),
    )(*args)
    return outs[0] if side is None else tuple(outs)


def _conv3(main, prev_row, next_row, w):
    tm = main.shape[0]
    row = lax.broadcasted_iota(jnp.int32, main.shape, 0)
    dn = jnp.where(row == 0, prev_row, pltpu.roll(main, 1, axis=0))
    up = jnp.where(row == tm - 1, next_row, pltpu.roll(main, tm - 1, axis=0))
    return dn * w[0:1, :] + main * w[1:2, :] + up * w[2:3, :]


def _halo_rows(prev_ref, next_ref, i, n_i):
    prev = prev_ref[...].astype(F32)
    nxt = next_ref[...].astype(F32)
    hr = prev.shape[0]
    prev_row = jnp.where(i == 0, 0.0, prev[hr - 1:hr, :])
    next_row = jnp.where(i == n_i - 1, 0.0, nxt[0:1, :])
    return prev_row, next_row


def _halo_specs(tm, tn, n_rows, col_fn, hr=8):
    rb = tm // hr
    last = n_rows // hr - 1
    return [
        pl.BlockSpec((tm, tn), lambda i, j: (i, col_fn(j))),
        pl.BlockSpec((hr, tn), lambda i, j: (jnp.maximum(i * rb - 1, 0), col_fn(j))),
        pl.BlockSpec((hr, tn), lambda i, j: (jnp.minimum((i + 1) * rb, last), col_fn(j))),
    ]


def _convgate_kernel(a_ref, ap_ref, an_ref, b_ref, bp_ref, bn_ref, wa_ref, wb_ref, o_ref):
    i = pl.program_id(0)
    n_i = pl.num_programs(0)
    pa, na = _halo_rows(ap_ref, an_ref, i, n_i)
    pb, nb = _halo_rows(bp_ref, bn_ref, i, n_i)
    a = _conv3(a_ref[...].astype(F32), pa, na, wa_ref[...])
    b = _conv3(b_ref[...].astype(F32), pb, nb, wb_ref[...])
    o_ref[...] = (a * jax.nn.sigmoid(a) * b).astype(o_ref.dtype)


def conv_gate(u, conv_w):
    M, N2 = u.shape
    F = N2 // 2
    tm = min(512, M)
    tn = 1024
    nb = F // tn
    specs = _halo_specs(tm, tn, M, lambda j: j, 16) + _halo_specs(tm, tn, M, lambda j: j + nb, 16)
    specs += [pl.BlockSpec((3, tn), lambda i, j: (0, j)), pl.BlockSpec((3, tn), lambda i, j: (0, j + nb))]
    return pl.pallas_call(
        _convgate_kernel,
        grid=(M // tm, nb),
        in_specs=specs,
        out_specs=pl.BlockSpec((tm, tn), lambda i, j: (i, j)),
        out_shape=jax.ShapeDtypeStruct((M, F), BF16),
        compiler_params=_cparams(("parallel", "parallel")),
    )(u, u, u, u, u, u, conv_w, conv_w)


def _dft_tables(T):
    A, B = T // FFT_B, FFT_B
    ka = jnp.arange(A, dtype=jnp.int32)
    a = jnp.arange(A, dtype=jnp.int32)
    b = jnp.arange(B, dtype=jnp.int32)
    n = (ka[None, :, None] * (B * a[None, None, :] + b[:, None, None])) % T
    ang = n.astype(F32) * (2.0 * math.pi / T)
    sa = 1.0 / math.sqrt(A)
    m1 = jnp.concatenate([jnp.cos(ang) * sa, -jnp.sin(ang) * sa], axis=1).astype(BF16)
    kb = jnp.arange(B, dtype=jnp.int32)
    n2 = (kb[:, None] * b[None, :]) % B
    ang2 = n2.astype(F32) * (2.0 * math.pi / B)
    sb = 1.0 / math.sqrt(B)
    m2 = jnp.concatenate([jnp.cos(ang2) * sb, jnp.sin(ang2) * sb], axis=1).astype(BF16)
    return m1, m2


def _channel_dft():
    c = np.arange(FOURIER_DIM)
    ang = 2.0 * np.pi * ((c[:, None] * c[None, :]) % FOURIER_DIM) / FOURIER_DIM
    s = 1.0 / math.sqrt(FOURIER_DIM)
    return np.cos(ang) * s, np.sin(ang) * s


FFT_SUB = 8


def _f1_kernel(x_ref, m_ref, ch_ref, pm_ref, re_ref, im_ref):
    A = x_ref.shape[0]
    ch = ch_ref[...]
    x_all = x_ref[...].reshape(A * FFT_SUB, x_ref.shape[2]).astype(BF16)
    xp = _dot(pm_ref[...], x_all).astype(BF16)
    for j in range(FFT_SUB):
        y = _dot(m_ref[j], xp[j * A:(j + 1) * A])
        for h in range(2):
            sl = slice(h * FOURIER_DIM, (h + 1) * FOURIER_DIM)
            lhs = jnp.concatenate([y[:A, sl], y[A:, sl]], axis=1).astype(BF16)
            yp = _dot(lhs, ch)
            re_ref[j, :, sl] = yp[:, :FOURIER_DIM]
            im_ref[j, :, sl] = yp[:, FOURIER_DIM:]


def _f2_kernel(re_ref, im_ref, m_ref, w_ref, b_ref, o_ref):
    for j in range(FFT_SUB):
        rhs = jnp.concatenate([re_ref[:, j, :], im_ref[:, j, :]], axis=0).astype(BF16)
        spec = _dot(m_ref[...], rhs)
        outs = []
        for h in range(FOURIER_HEADS):
            sl = slice(h * FOURIER_DIM, (h + 1) * FOURIER_DIM)
            outs.append(_dot(spec[:, sl].astype(BF16), w_ref[h].astype(BF16)) + b_ref[h])
        o_ref[:, j, :] = jnp.concatenate(outs, axis=1)


def fourier_mix(z, fw, fb):
    T, NC = z.shape
    A, B = T // FFT_B, FFT_B
    m1, m2 = _dft_tables(T)
    cc, ss = _channel_dft()
    ch = jnp.asarray(np.block([[cc, -ss], [ss, cc]]), BF16)
    cb0 = COL_F // 512
    n = A * FFT_SUB
    dst = np.arange(n)
    pm = jnp.asarray((((dst % A) * FFT_SUB + dst // A)[:, None] == np.arange(n)[None, :]).astype(np.float32), BF16)
    yre, yim = pl.pallas_call(
        _f1_kernel,
        grid=(B // FFT_SUB, 2),
        in_specs=[
            pl.BlockSpec((A, FFT_SUB, 512), lambda b, c: (0, b, cb0 + c)),
            pl.BlockSpec((FFT_SUB, 2 * A, A), lambda b, c: (b, 0, 0)),
            pl.BlockSpec((512, 512), lambda b, c: (0, 0)),
            pl.BlockSpec((n, n), lambda b, c: (0, 0)),
        ],
        out_specs=[pl.BlockSpec((FFT_SUB, A, 512), lambda b, c: (b, 0, c))] * 2,
        out_shape=[jax.ShapeDtypeStruct((B, A, FOURIER_WIDTH), F32)] * 2,
        compiler_params=_cparams(("parallel", "parallel")),
    )(z.reshape(A, B, NC), m1, ch, pm)
    blk = pl.BlockSpec((B, FFT_SUB, FOURIER_WIDTH), lambda i: (0, i, 0))
    out = pl.pallas_call(
        _f2_kernel,
        grid=(A // FFT_SUB,),
        in_specs=[
            blk, blk,
            pl.BlockSpec((B, 2 * B), lambda i: (0, 0)),
            pl.BlockSpec((FOURIER_HEADS, FOURIER_DIM, FOURIER_DIM), lambda i: (0, 0, 0)),
            pl.BlockSpec((FOURIER_HEADS, 1, FOURIER_DIM), lambda i: (0, 0, 0)),
        ],
        out_specs=blk,
        out_shape=jax.ShapeDtypeStruct((B, A, FOURIER_WIDTH), F32),
        compiler_params=_cparams(("parallel",)),
    )(yre, yim, m2, fw, fb.reshape(FOURIER_HEADS, 1, FOURIER_DIM))
    return out.reshape(T, FOURIER_WIDTH)


def _fctx_kernel(f_ref, cs_ref, ts_ref, w_ref, b_ref, o_ref):
    for h in range(FOURIER_HEADS):
        sl = slice(h * FOURIER_DIM, (h + 1) * FOURIER_DIM)
        g = _dot(f_ref[:, sl].astype(BF16), cs_ref[...])
        gg = jnp.concatenate([g[:, :FOURIER_DIM], g[:, FOURIER_DIM:]], axis=0).astype(BF16)
        spec = _dot(ts_ref[...], gg)
        out = _dot(spec.astype(BF16), w_ref[h].astype(BF16)) + b_ref[h]
        o_ref[:, sl] = out.astype(o_ref.dtype)


def fourier_mix_ctx(zc, fw, fb):
    T = zc.shape[0]
    cc, ss = _channel_dft()
    cs = jnp.asarray(np.concatenate([cc, ss], axis=1), BF16)
    t = np.arange(T)
    ang = 2.0 * np.pi * ((t[:, None] * t[None, :]) % T) / T
    st = 1.0 / math.sqrt(T)
    ts = jnp.asarray(np.concatenate([np.cos(ang) * st, -np.sin(ang) * st], axis=1), BF16)
    return pl.pallas_call(
        _fctx_kernel,
        grid=(1,),
        in_specs=[
            pl.BlockSpec((T, FOURIER_WIDTH), lambda i: (0, COL_F // FOURIER_WIDTH)),
            pl.BlockSpec((FOURIER_DIM, 2 * FOURIER_DIM), lambda i: (0, 0)),
            pl.BlockSpec((T, 2 * T), lambda i: (0, 0)),
            pl.BlockSpec((FOURIER_HEADS, FOURIER_DIM, FOURIER_DIM), lambda i: (0, 0, 0)),
            pl.BlockSpec((FOURIER_HEADS, 1, FOURIER_DIM), lambda i: (0, 0, 0)),
        ],
        out_specs=pl.BlockSpec((T, FOURIER_WIDTH), lambda i: (0, 0)),
        out_shape=jax.ShapeDtypeStruct((T, FOURIER_WIDTH), BF16),
        compiler_params=_cparams(("arbitrary",)),
    )(zc, cs, ts, fw, fb.reshape(FOURIER_HEADS, 1, FOURIER_DIM))


def _rope_tables(T):
    nf = NA_HEAD_DIM // 4
    t = jnp.arange(T)
    inv = 1.0 / (ROPE_THETA ** (jnp.arange(nf, dtype=F32) / nf))
    lane = jnp.arange(NA_HEAD_DIM)
    pos = jnp.where(lane[None, :] < NA_HEAD_DIM // 2, (t // GRID_W)[:, None], (t % GRID_W)[:, None]).astype(F32)
    ang = pos * inv[lane % nf][None, :]
    sign = jnp.where((lane % (2 * nf)) < nf, -1.0, 1.0)[None, :]
    return jnp.cos(ang), jnp.sin(ang) * sign


def _head_norm_rope(x, w, cos, sin, perm):
    ms = jnp.mean(x * x, axis=-1, keepdims=True)
    y = x * lax.rsqrt(ms + NORM_EPS) * w
    if cos is None:
        return y
    return y * cos + _dot(y.astype(BF16), perm) * sin


def _qkprep_kernel(q_ref, k_ref, qw_ref, kw_ref, cos_ref, sin_ref, perm_ref, qo_ref, ko_ref, *, rope):
    cos = cos_ref[...] if rope else None
    sin = sin_ref[...] if rope else None
    perm = perm_ref[...]
    qw = qw_ref[...] * (NA_HEAD_DIM ** -0.5)
    kw = kw_ref[...]
    for h in range(4):
        sl = slice(h * LANE, (h + 1) * LANE)
        qo_ref[:, sl] = _head_norm_rope(q_ref[:, sl], qw, cos, sin, perm).astype(qo_ref.dtype)
        ko_ref[:, sl] = _head_norm_rope(k_ref[:, sl], kw, cos, sin, perm).astype(ko_ref.dtype)


def qk_prep(z, qw, kw, rope):
    T = z.shape[0]
    tm = min(512, T)
    if rope:
        cos, sin = _rope_tables(T)
    else:
        cos = sin = jnp.zeros((T, LANE), F32)
    nq = NA_WIDTH // 512
    lane = np.arange(LANE)
    src = np.where(lane % 64 < 32, lane + 32, lane - 32)
    perm = jnp.asarray((lane[:, None] == src[None, :]).astype(np.float32), BF16)
    return pl.pallas_call(
        functools.partial(_qkprep_kernel, rope=rope),
        grid=(T // tm, nq),
        in_specs=[
            pl.BlockSpec((tm, 512), lambda i, j: (i, COL_Q // 512 + j)),
            pl.BlockSpec((tm, 512), lambda i, j: (i, COL_K // 512 + j)),
            pl.BlockSpec((1, LANE), lambda i, j: (0, 0)),
            pl.BlockSpec((1, LANE), lambda i, j: (0, 0)),
            pl.BlockSpec((tm, LANE), lambda i, j: (i, 0)),
            pl.BlockSpec((tm, LANE), lambda i, j: (i, 0)),
            pl.BlockSpec((LANE, LANE), lambda i, j: (0, 0)),
        ],
        out_specs=[pl.BlockSpec((tm, 512), lambda i, j: (i, j))] * 2,
        out_shape=[jax.ShapeDtypeStruct((T, NA_WIDTH), BF16)] * 2,
        compiler_params=_cparams(("parallel", "parallel")),
    )(z, z, qw.reshape(1, LANE), kw.reshape(1, LANE), cos, sin, perm)


def _toeplitz_kernel(r_ref, e_ref, o_ref):
    r = r_ref[0]
    acc = jnp.zeros(o_ref.shape[1:], F32)
    for d in range(2 * NA_KW - 1):
        acc = acc + r[:, d:d + 1] * e_ref[d:d + 1, :]
    o_ref[0] = acc


def rpb_bias_tables(rpb):
    L, H = rpb.shape[0], rpb.shape[1]
    ndr, ndc = 2 * NA_KH - 1, 2 * NA_KW - 1
    q = np.arange(GRID_W)
    e = np.zeros((32, GRID_W, GRID_W), np.float32)
    for d in range(ndc):
        e[d] = (q[None, :] - q[:, None] + (NA_KW - 1)) == d
    e = jnp.asarray(e.reshape(32, GRID_W * GRID_W))
    rp = jnp.pad(rpb.reshape(L * H, ndr, ndc), ((0, 0), (0, 16 - ndr), (0, 32 - ndc)))
    toep = pl.pallas_call(
        _toeplitz_kernel,
        grid=(L * H,),
        in_specs=[pl.BlockSpec((1, 16, 32), lambda i: (i, 0, 0)),
                  pl.BlockSpec((32, GRID_W * GRID_W), lambda i: (0, 0))],
        out_specs=pl.BlockSpec((1, 16, GRID_W * GRID_W), lambda i: (i, 0, 0)),
        out_shape=jax.ShapeDtypeStruct((L * H, 16, GRID_W * GRID_W), F32),
        compiler_params=_cparams(("parallel",)),
    )(rp, e)
    toep = toep.reshape(L, H, 16, GRID_W, GRID_W)
    col_start = np.clip(q - NA_KW // 2, 0, GRID_W - NA_KW)
    in_win = (q[None, :] >= col_start[:, None]) & (q[None, :] < col_start[:, None] + NA_KW)
    mask = jnp.asarray(np.where(in_win, 0.0, -1e30).astype(np.float32))
    tabs = []
    for o in range(NA_KH):
        band = toep[:, :, NA_KH - 1 - o:2 * NA_KH - 1 - o]
        band = band + mask[None, None, None]
        tabs.append(jnp.transpose(band, (0, 1, 3, 2, 4)).reshape(L, H, GRID_W, NA_KH * GRID_W))
    return jnp.stack(tabs, axis=2)


NA_HEADS_PER_STEP = 2


def _na_kernel(q_ref, kp_ref, kc_ref, kn_ref, vp_ref, vc_ref, vn_ref, kx_ref, vx_ref, bt_ref, o_ref,
               kbuf, vbuf, *, nrows):
    m = pl.program_id(1)
    blk = NA_KH * GRID_W
    kbuf[0:blk] = kp_ref[...]
    kbuf[blk:2 * blk] = kc_ref[...]
    kbuf[2 * blk:3 * blk] = kn_ref[...]
    vbuf[0:blk] = vp_ref[...].astype(BF16)
    vbuf[blk:2 * blk] = vc_ref[...].astype(BF16)
    vbuf[2 * blk:3 * blk] = vn_ref[...].astype(BF16)
    kctx = kx_ref[...]
    vctx = vx_ref[...].astype(BF16)
    work, scores = [], []
    for hh in range(NA_HEADS_PER_STEP):
        hs = slice(hh * LANE, (hh + 1) * LANE)
        for j in range(NA_KH):
            r = m * NA_KH + j
            rs = jnp.clip(r - NA_KH // 2, 0, nrows - NA_KH)
            start = pl.multiple_of((rs - (m - 1) * NA_KH) * GRID_W, GRID_W)
            q = q_ref[j * GRID_W:(j + 1) * GRID_W, hs]
            s = _dot_nt(q, kbuf[pl.ds(start, blk), hs]) + bt_ref[hh, r - rs]
            work.append((hs, j, start))
            scores.append((s, _dot_nt(q, kctx[:, hs])))
    probs = []
    for s, sc in scores:
        mx = jnp.maximum(jnp.max(s, axis=-1, keepdims=True), jnp.max(sc, axis=-1, keepdims=True))
        p = jnp.exp(s - mx)
        pc = jnp.exp(sc - mx)
        den = jnp.sum(p, axis=-1, keepdims=True) + jnp.sum(pc, axis=-1, keepdims=True)
        probs.append((p.astype(BF16), pc.astype(BF16), den))
    for (hs, j, start), (p, pc, den) in zip(work, probs):
        acc = _dot(p, vbuf[pl.ds(start, blk), hs]) + _dot(pc, vctx[:, hs])
        o_ref[j * GRID_W:(j + 1) * GRID_W, hs] = (acc / den).astype(o_ref.dtype)


def na_attention(qn, kn, z, kcn, zc, bias_tab):
    T = qn.shape[0]
    C = kcn.shape[0]
    nrows = T // GRID_W
    blk = NA_KH * GRID_W
    nblk = T // blk
    hw = NA_HEADS_PER_STEP * LANE
    vcol = COL_V // hw
    prev = lambda h, m: (jnp.maximum(m - 1, 0), h)
    cur = lambda h, m: (m, h)
    nxt = lambda h, m: (jnp.minimum(m + 1, nblk - 1), h)
    vprev = lambda h, m: (jnp.maximum(m - 1, 0), vcol + h)
    vcur = lambda h, m: (m, vcol + h)
    vnxt = lambda h, m: (jnp.minimum(m + 1, nblk - 1), vcol + h)
    return pl.pallas_call(
        functools.partial(_na_kernel, nrows=nrows),
        grid=(NA_HEADS // NA_HEADS_PER_STEP, nblk),
        in_specs=[
            pl.BlockSpec((blk, hw), cur),
            pl.BlockSpec((blk, hw), prev), pl.BlockSpec((blk, hw), cur), pl.BlockSpec((blk, hw), nxt),
            pl.BlockSpec((blk, hw), vprev), pl.BlockSpec((blk, hw), vcur), pl.BlockSpec((blk, hw), vnxt),
            pl.BlockSpec((C, hw), lambda h, m: (0, h)),
            pl.BlockSpec((C, hw), lambda h, m: (0, vcol + h)),
            pl.BlockSpec((NA_HEADS_PER_STEP, NA_KH, GRID_W, blk), lambda h, m: (h, 0, 0, 0)),
        ],
        out_specs=pl.BlockSpec((blk, hw), cur),
        out_shape=jax.ShapeDtypeStruct((T, NA_WIDTH), BF16),
        scratch_shapes=[pltpu.VMEM((3 * blk, hw), BF16), pltpu.VMEM((3 * blk, hw), BF16)],
        compiler_params=_cparams(("parallel", "parallel")),
    )(qn, kn, kn, kn, z, z, z, kcn, zc, bias_tab)


def _ctxattn_kernel(q_ref, k_ref, v_ref, o_ref):
    s = _dot_nt(q_ref[...], k_ref[...])
    p = jnp.exp(s - jnp.max(s, axis=-1, keepdims=True))
    den = jnp.sum(p, axis=-1, keepdims=True)
    o_ref[...] = (_dot(p.astype(BF16), v_ref[...].astype(BF16)) / den).astype(o_ref.dtype)


def ctx_attention(qcn, kcn, zc):
    C = qcn.shape[0]
    vcol = COL_V // LANE
    return pl.pallas_call(
        _ctxattn_kernel,
        grid=(NA_HEADS,),
        in_specs=[pl.BlockSpec((C, LANE), lambda h: (0, h)), pl.BlockSpec((C, LANE), lambda h: (0, h)),
                  pl.BlockSpec((C, LANE), lambda h: (0, vcol + h))],
        out_specs=pl.BlockSpec((C, LANE), lambda h: (0, h)),
        out_shape=jax.ShapeDtypeStruct((C, NA_WIDTH), BF16),
        compiler_params=_cparams(("parallel",)),
    )(qcn, kcn, zc)


def _seg_sum64(x):
    lane = lax.broadcasted_iota(jnp.int32, x.shape, 1)
    low = lane < RWKV_HEAD_DIM
    s_lo = jnp.sum(jnp.where(low, x, 0.0), axis=-1, keepdims=True)
    s_hi = jnp.sum(jnp.where(low, 0.0, x), axis=-1, keepdims=True)
    return jnp.where(low, s_lo, s_hi)


def _seg_sum(x):
    return jnp.concatenate([_seg_sum64(x[:, i * LANE:(i + 1) * LANE]) for i in range(x.shape[1] // LANE)], axis=1)


def _rwkvprep_kernel(r_ref, rp_ref, rn_ref, k_ref, kp_ref, kn_ref, v_ref, vp_ref, vn_ref, lo_ref, cw_ref,
                     w2_ref, a2_ref, g2_ref, w0_ref, a0_ref, kk_ref, ka_ref, rk_ref,
                     r_o, v_o, kk_o, g_o, bonus_o, logw_o, kd_o, ag_o):
    i = pl.program_id(0)
    n_i = pl.num_programs(0)
    W = RWKV_WIDTH
    cw = cw_ref[...]
    r = _conv3(r_ref[...], *_halo_rows(rp_ref, rn_ref, i, n_i), cw[:, 0:W])
    k = _conv3(k_ref[...], *_halo_rows(kp_ref, kn_ref, i, n_i), cw[:, W:2 * W])
    v = _conv3(v_ref[...], *_halo_rows(vp_ref, vn_ref, i, n_i), cw[:, 2 * W:3 * W])
    lora = lo_ref[...]
    wl = _dot(jnp.tanh(lora[:, 0:2 * W_LORA]).astype(BF16), w2_ref[...])
    al = _dot(lora[:, 2 * W_LORA:2 * (W_LORA + A_LORA)].astype(BF16), a2_ref[...])
    g = _dot(jax.nn.sigmoid(lora[:, 2 * (W_LORA + A_LORA):]).astype(BF16), g2_ref[...])
    kkr = k * kk_ref[...]
    kk = kkr * lax.rsqrt(jnp.maximum(_seg_sum(kkr * kkr), 1e-24))
    kds = []
    for d in range(2):
        logw = -math.exp(-0.5) * jax.nn.sigmoid(w0_ref[d:d + 1, :] + wl[:, d * W:(d + 1) * W])
        a = jax.nn.sigmoid(a0_ref[d:d + 1, :] + al[:, d * W:(d + 1) * W])
        kd = k * (1.0 + (a - 1.0) * ka_ref[...])
        kds.append(kd)
        for gi in range(N_GROUPS):
            sl = slice(gi * GROUP, (gi + 1) * GROUP)
            logw_o[d, gi] = logw[:, sl]
            kd_o[d, gi] = kd[:, sl].astype(kd_o.dtype)
            ag_o[d, gi] = a[:, sl].astype(ag_o.dtype)
    bonus = _seg_sum(r * (kds[0] + kds[1]) * rk_ref[...]) * v
    for gi in range(N_GROUPS):
        sl = slice(gi * GROUP, (gi + 1) * GROUP)
        r_o[gi] = r[:, sl].astype(r_o.dtype)
        v_o[gi] = v[:, sl].astype(v_o.dtype)
        kk_o[gi] = kk[:, sl].astype(kk_o.dtype)
        g_o[gi] = g[:, sl].astype(g_o.dtype)
        bonus_o[gi] = bonus[:, sl].astype(bonus_o.dtype)


def rwkv_prep(z, p):
    T = z.shape[0]
    tm = min(128, T)
    W = RWKV_WIDTH
    cb = COL_R // W
    specs = []
    for c in range(3):
        specs += _halo_specs(tm, W, T, lambda j, c=c: cb + c)
    specs = [pl.BlockSpec(s.block_shape, lambda i, f=s.index_map: f(i, 0)) for s in specs]
    full = lambda shape: pl.BlockSpec(shape, lambda i: (0,) * len(shape))
    specs += [
        pl.BlockSpec((tm, LORA_PAD), lambda i: (i, COL_L // LORA_PAD)),
        full((3, 3 * W)), full((2 * W_LORA, 2 * W)), full((2 * A_LORA, 2 * W)), full((LORA_PAD - 256, W)),
        full((2, W)), full((2, W)), full((1, W)), full((1, W)), full((1, W)),
    ]
    g1 = pl.BlockSpec((N_GROUPS, tm, GROUP), lambda i: (0, i, 0))
    g2 = pl.BlockSpec((2, N_GROUPS, tm, GROUP), lambda i: (0, 0, i, 0))
    s1 = jax.ShapeDtypeStruct((N_GROUPS, T, GROUP), BF16)
    s2 = jax.ShapeDtypeStruct((2, N_GROUPS, T, GROUP), BF16)
    s2f = jax.ShapeDtypeStruct((2, N_GROUPS, T, GROUP), F32)
    return pl.pallas_call(
        _rwkvprep_kernel,
        grid=(T // tm,),
        in_specs=specs,
        out_specs=[g1] * 5 + [g2] * 3,
        out_shape=[s1] * 5 + [s2f, s2, s2],
        compiler_params=_cparams(("parallel",)),
    )(z, z, z, z, z, z, z, z, z, z, p['rwkv_conv'], p['w2bd'], p['a2bd'], p['g2p'], p['w0'], p['a0'],
      p['k_k'].reshape(1, W), p['k_a'].reshape(1, W), p['r_k'].reshape(1, W))


def _fold(x):
    c = CHUNK
    return x[0:c] + x[c:2 * c] + x[2 * c:3 * c] + x[3 * c:4 * c]


def _rwkv_kernel(r_ref, v_ref, kk_ref, logw_ref, kd_ref, ag_ref, tri_ref, ms_ref, mi_ref, bd_ref, bdf_ref, z0_ref,
                 y_ref, zf_ref, z_scr):
    forward = pl.program_id(0) == 0
    c = pl.program_id(1)

    @pl.when(c == 0)
    def _():
        z_scr[...] = z0_ref[0]

    tri = tri_ref[0]
    m_strict = ms_ref[0]
    m_incl = mi_ref[0]
    bd = bd_ref[...]
    bdf = bdf_ref[...]
    row = lax.broadcasted_iota(jnp.int32, (CHUNK, GROUP), 0)
    col = lax.broadcasted_iota(jnp.int32, (CHUNK, GROUP), 1)
    diag = row == (col % RWKV_HEAD_DIM)
    eye_f = jnp.where(diag, 1.0, 0.0)

    def expand(x):
        return jnp.concatenate([x.astype(BF16)] * 4, axis=0) * bd

    def stack(*xs):
        return jnp.concatenate([x.astype(BF16) for x in xs], axis=0)

    cs_all = _dot(tri, jnp.concatenate([h for g in range(N_GROUPS) for h in _split(logw_ref[0, g])], axis=1))

    def prep(g):
        lw = logw_ref[0, g]
        cs = cs_all[:, 2 * g * GROUP:2 * (g + 1) * GROUP]
        linc = cs[:, :GROUP] + cs[:, GROUP:]
        ltot = jnp.where(forward, linc[CHUNK - 1:CHUNK, :], linc[0:1, :])
        e_inc = jnp.exp(linc)
        e_neg = jnp.exp(-linc).astype(BF16)
        e_exc = jnp.exp(linc - lw).astype(BF16)
        e_rem = jnp.exp(ltot - linc).astype(BF16)
        kk = kk_ref[g]
        kd = kd_ref[0, g]
        b = kk * ag_ref[0, g]
        at = -(kk * e_exc)
        rt = r_ref[g].astype(F32) * e_inc
        gram = _dot_nt(stack(at, rt), jnp.concatenate([expand(b * e_neg), expand(kd * e_neg)], axis=0))
        fab = gram[:CHUNK, :GROUP] * m_strict
        return dict(g=g, rt=rt, at=at, e_tot=jnp.exp(ltot), bh=b * e_rem, kh=kd * e_rem, v=v_ref[g],
                    fp=fab, ft=eye_f + fab,
                    fak=gram[:CHUNK, GROUP:] * m_strict, frb=gram[CHUNK:, :GROUP] * m_incl,
                    frk=gram[CHUNK:, GROUP:] * m_incl)

    st = [prep(g) for g in range(N_GROUPS)]
    for s in st:
        res = _dot(stack(s['fp'], s['frb']), expand(s['fp']))
        s['fp'] = res[:CHUNK]
        s['fg'] = s['frb'] + res[CHUNK:]
    for _ in range(4):
        for s in st:
            res = _dot(stack(s['fp'], s['ft'], s['fg']), expand(s['fp']))
            s['fp'] = res[:CHUNK]
            s['ft'] = s['ft'] + res[CHUNK:2 * CHUNK]
            s['fg'] = s['fg'] + res[2 * CHUNK:]
    for s in st:
        res = _dot(stack(s['ft'], s['fg']), expand(s['fp']))
        s['ft'] = s['ft'] + res[:CHUNK]
        s['fg'] = s['fg'] + res[CHUNK:]
    for s in st:
        res = _dot(stack(s['ft'], s['fg']), jnp.concatenate([expand(s['fak']), expand(s['at'])], axis=1))
        s['fta'] = res[:CHUNK, :GROUP]
        s['ff'] = res[CHUNK:, :GROUP] + s['frk']
        s['fa1'] = res[:CHUNK, GROUP:]
        s['rp'] = s['rt'] + res[CHUNK:, GROUP:]
    for s in st:
        res = _dot(stack(s['fta'], s['ff']), expand(s['v']))
        s['fu0'] = res[:CHUNK]
        s['y0'] = res[CHUNK:]
    for s in st:
        zero = jnp.zeros((CHUNK, GROUP), BF16)
        rhs = jnp.concatenate([jnp.concatenate([s['fa1'].astype(BF16), zero], axis=1),
                               jnp.concatenate([zero, s['fu0'].astype(BF16)], axis=1),
                               jnp.concatenate([zero, s['v']], axis=1)], axis=0)
        mn = _dot_tn(stack(s['bh'], s['bh'], s['kh']), rhs)
        s['mf'] = _fold(mn[:, :GROUP] * bdf) + jnp.where(diag, s['e_tot'], 0.0)
        s['nf'] = _fold(mn[:, GROUP:] * bdf)
    for s in st:
        g = s['g']
        zhx = expand(z_scr[g])
        mh, ml = _split(s['mf'])
        res = _dot(jnp.concatenate([s['rp'].astype(BF16), mh, ml], axis=0), zhx)
        y_ref[0, g] = res[:CHUNK] + s['y0']
        z_scr[g] = res[CHUNK:2 * CHUNK] + res[2 * CHUNK:] + s['nf']

    @pl.when(c == pl.num_programs(1) - 1)
    def _():
        zf_ref[0] = z_scr[...]


def _chunk_masks():
    i = np.arange(CHUNK)
    tri = np.stack([i[None, :] <= i[:, None], i[None, :] >= i[:, None]]).astype(np.float32)
    j = np.arange(GROUP)
    same = ((j[:, None] // CHUNK) == (j[None, :] // CHUNK)).astype(np.float32)
    js = j[None, :] % CHUNK
    strict = np.stack([js < i[:, None], js > i[:, None]]).astype(np.float32)
    incl = np.stack([js <= i[:, None], js >= i[:, None]]).astype(np.float32)
    return jnp.asarray(tri, BF16), jnp.asarray(strict), jnp.asarray(incl), jnp.asarray(same, BF16), jnp.asarray(same)


def rwkv_scan(r, v, kk, logw, kd, ag, z0):
    T = r.shape[1]
    nc = T // CHUNK
    tri, strict, incl, bd, bdf = _chunk_masks()
    order = lambda d, c: c + d * (nc - 1 - 2 * c)
    b1 = pl.BlockSpec((N_GROUPS, CHUNK, GROUP), lambda d, c: (0, order(d, c), 0))
    b2 = pl.BlockSpec((1, N_GROUPS, CHUNK, GROUP), lambda d, c: (d, 0, order(d, c), 0))
    per_dir = lambda shape: pl.BlockSpec((1,) + shape, lambda d, c: (d,) + (0,) * len(shape))
    full = lambda shape: pl.BlockSpec(shape, lambda d, c: (0,) * len(shape))
    zspec = per_dir((N_GROUPS, CHUNK, GROUP))
    return pl.pallas_call(
        _rwkv_kernel,
        grid=(2, nc),
        in_specs=[b1, b1, b1, b2, b2, b2, per_dir((CHUNK, CHUNK)), per_dir((CHUNK, GROUP)), per_dir((CHUNK, GROUP)),
                  full((GROUP, GROUP)), full((GROUP, GROUP)), zspec],
        out_specs=[b2, zspec],
        out_shape=[jax.ShapeDtypeStruct((2, N_GROUPS, T, GROUP), F32),
                   jax.ShapeDtypeStruct((2, N_GROUPS, CHUNK, GROUP), F32)],
        scratch_shapes=[pltpu.VMEM((N_GROUPS, CHUNK, GROUP), F32)],
        compiler_params=_cparams(("arbitrary", "arbitrary")),
    )(r, v, kk, logw, kd, ag, tri, strict, incl, bd, bdf, z0)


def _rwkvpost_kernel(y_ref, g_ref, bonus_ref, gw_ref, gb_ref, o_ref):
    for gi in range(N_GROUPS):
        y = y_ref[0, gi] + y_ref[1, gi]
        mu = _seg_sum(y) * (1.0 / RWKV_HEAD_DIM)
        yc = y - mu
        var = _seg_sum(yc * yc) * (1.0 / RWKV_HEAD_DIM)
        sl = slice(gi * GROUP, (gi + 1) * GROUP)
        yn = yc * lax.rsqrt(var + GN_EPS) * gw_ref[:, sl] + gb_ref[:, sl]
        o_ref[:, sl] = ((yn + bonus_ref[gi].astype(F32)) * g_ref[gi].astype(F32)).astype(o_ref.dtype)


def rwkv_post(y, g, bonus, gn_w, gn_b):
    T = y.shape[2]
    tm = min(256, T)
    W = RWKV_WIDTH
    return pl.pallas_call(
        _rwkvpost_kernel,
        grid=(T // tm,),
        in_specs=[pl.BlockSpec((2, N_GROUPS, tm, GROUP), lambda i: (0, 0, i, 0)),
                  pl.BlockSpec((N_GROUPS, tm, GROUP), lambda i: (0, i, 0)),
                  pl.BlockSpec((N_GROUPS, tm, GROUP), lambda i: (0, i, 0)),
                  pl.BlockSpec((1, W), lambda i: (0, 0)), pl.BlockSpec((1, W), lambda i: (0, 0))],
        out_specs=pl.BlockSpec((tm, W), lambda i: (i, 0)),
        out_shape=jax.ShapeDtypeStruct((T, W), BF16),
        compiler_params=_cparams(("parallel",)),
    )(y, g, bonus, gn_w.reshape(1, W), gn_b.reshape(1, W))


def rwkv_mix(z, p, z0):
    r, v, kk, g, bonus, logw, kd, ag = rwkv_prep(z, p)
    y, zf = rwkv_scan(r, v, kk, logw, kd, ag, z0)
    return rwkv_post(y, g, bonus, p['gn_w'], p['gn_b']), zf


def _lora_heads(l, w2, a2, g2):
    W = RWKV_WIDTH
    zw = jnp.zeros((W_LORA, W), F32)
    w2bd = jnp.concatenate([jnp.concatenate([w2[l, 0], zw], axis=1), jnp.concatenate([zw, w2[l, 1]], axis=1)], axis=0)
    a2bd = jnp.concatenate([jnp.concatenate([a2[l, 0], zw], axis=1), jnp.concatenate([zw, a2[l, 1]], axis=1)], axis=0)
    g2p = jnp.pad(g2[l], ((0, LORA_PAD - 256 - G_LORA), (0, 0)))
    return dict(w2bd=w2bd.astype(BF16), a2bd=a2bd.astype(BF16), g2p=g2p.astype(BF16))


def _layer(x, ctx, mod_x, mod_c, p, bias_tab, ctx_out, big, li, has_next):
    Dm = D_MODEL
    l = 0
    sh1, sc1, gt1, sh2, sc2, gt2 = [mod_x[i * Dm:(i + 1) * Dm] for i in range(6)]
    csh1, csc1, cgt1, csh2, csc2, cgt2 = [mod_c[i * Dm:(i + 1) * Dm] for i in range(6)]
    zx, w_out_b = norm_mod_matmul(x, p['norm1_w'], sc1, sh1, p['w_in'], l, side=('plain', big['w_out'], li))
    zc = norm_mod_matmul(ctx, p['norm1_w'], csc1, csh1, p['w_in'], l)
    fx = fourier_mix(zx, p['fourier_w'], p['fourier_b'])
    qx, kx = qk_prep(zx, p['q_norm_w'], p['k_norm_w'], rope=True)
    qc, kc = qk_prep(zc, p['q_norm_w'], p['k_norm_w'], rope=False)
    ax = na_attention(qx, kx, zx, kc, zc, bias_tab)
    z0 = jnp.zeros((2, N_GROUPS, CHUNK, GROUP), F32)
    rc, zf = rwkv_mix(zc, p, z0)
    rx, _ = rwkv_mix(zx, p, zf)
    x, w_ffn_in_b = matmul_residual([fx, ax, rx], w_out_b, l, x, gt1, side=('plain', big['w_ffn_in'], li))
    u, w_ffn_out_b = norm_mod_matmul(x, p['norm2_w'], sc2, sh2, w_ffn_in_b, l, out_dtype=BF16,
                                     side=('plain', big['w_ffn_out'], li))
    gated = [conv_gate(u, p['ffn_conv'])]
    if has_next:
        x, w_in_next = matmul_residual(gated, w_ffn_out_b, l, x, gt2, side=('win', big['w_in'], li + 1))
    else:
        x, w_in_next = matmul_residual(gated, w_ffn_out_b, l, x, gt2), None
    if ctx_out:
        fc = fourier_mix_ctx(zc, p['fourier_w'], p['fourier_b'])
        ac = ctx_attention(qc, kc, zc)
        ctx = matmul_residual([fc, ac, rc], w_out_b, l, ctx, cgt1)
        uc = norm_mod_matmul(ctx, p['norm2_w'], csc2, csh2, w_ffn_in_b, l, out_dtype=BF16)
        ctx = matmul_residual([conv_gate(uc, p['ffn_conv'])], w_ffn_out_b, l, ctx, cgt2)
    return x, ctx, w_in_next


def kernel(x, c, ctx, c_ctx, ada_w, ada_b, norm1_w, norm2_w, w_in, fourier_w, fourier_b, q_norm_w, k_norm_w, rpb,
           rwkv_conv, w0, w2, a0, a2, g2, k_k, k_a, r_k, gn_w, gn_b, w_out, ffn_conv, w_ffn_in, w_ffn_out):
    L = ada_w.shape[0]
    xs = x[0]
    cs = ctx[0]
    mods = ada_mod(jnp.concatenate([c, c_ctx[None, :]], axis=0), ada_w, ada_b)
    bias_tabs = rpb_bias_tables(rpb)
    big = dict(w_in=w_in, w_out=w_out, w_ffn_in=w_ffn_in, w_ffn_out=w_ffn_out)
    w_in_b = cast_layer('win', w_in, 0)
    for l in range(L):
        p = dict(w_in=w_in_b)
        p.update(_lora_heads(l, w2, a2, g2))
        p.update(norm1_w=norm1_w[l], norm2_w=norm2_w[l], fourier_w=fourier_w[l], fourier_b=fourier_b[l],
                 q_norm_w=q_norm_w[l], k_norm_w=k_norm_w[l], rwkv_conv=rwkv_conv[l], w0=w0[l], a0=a0[l],
                 k_k=k_k[l], k_a=k_a[l], r_k=r_k[l], gn_w=gn_w[l], gn_b=gn_b[l], ffn_conv=ffn_conv[l])
        xs, cs, w_in_b = _layer(xs, cs, mods[l, 0], mods[l, 1], p, bias_tabs[l], l < L - 1, big, l, l < L - 1)
    return xs[None]
```

```python
import functools
import math

import jax
import jax.numpy as jnp
import numpy as np
from jax import lax
from jax.experimental import pallas as pl
from jax.experimental.pallas import tpu as pltpu

F32 = jnp.float32
BF16 = jnp.bfloat16

D_MODEL = 4096
GRID_W = 64
FOURIER_WIDTH = 1024
FOURIER_HEADS = 4
FOURIER_DIM = 256
NA_WIDTH = 1536
NA_HEAD_DIM = 128
NA_HEADS = 12
NA_KH = 8
NA_KW = 16
RWKV_WIDTH = 1536
RWKV_HEAD_DIM = 64
W_LORA = 64
A_LORA = 64
G_LORA = 224
D_FF = 5120
ROPE_THETA = 10000.0
NORM_EPS = 1e-6
GN_EPS = 64e-5

COL_Q = 0
COL_K = NA_WIDTH
COL_V = 2 * NA_WIDTH
COL_R = 3 * NA_WIDTH
COL_F = 6 * NA_WIDTH
COL_L = COL_F + FOURIER_WIDTH
LORA_PAD = 512
IN_COLS_PAD = COL_L + LORA_PAD

CHUNK = 64
GROUP = 256
N_GROUPS = RWKV_WIDTH // GROUP
FFT_B = 128

LANE = 128
VMEM_LIMIT = 48 * 1024 * 1024


def _cparams(sem):
    return pltpu.CompilerParams(dimension_semantics=sem, vmem_limit_bytes=VMEM_LIMIT)


def _dot(a, b):
    return jnp.dot(a, b, preferred_element_type=F32)


def _dot_nt(a, b):
    return lax.dot_general(a, b, (((1,), (1,)), ((), ())), preferred_element_type=F32)


def _dot_tn(a, b):
    return lax.dot_general(a, b, (((0,), (0,)), ((), ())), preferred_element_type=F32)


def _split(x):
    hi = x.astype(BF16)
    lo = (x - hi.astype(F32)).astype(BF16)
    return hi, lo


def _ada_kernel(s_ref, w_ref, b_ref, o_ref, acc_ref):
    k = pl.program_id(2)
    tk, tn = w_ref.shape[1], w_ref.shape[2]
    rep = tn // LANE

    @pl.when(k == 0)
    def _():
        acc_ref[...] = jnp.zeros_like(acc_ref)

    def body(i, carry):
        a0, a1 = carry
        r = pl.multiple_of(i * 8, 8)
        rs = pl.multiple_of(k * tk + i * 8, 8)
        w = w_ref[0, pl.ds(r, 8), :]
        s0 = s_ref[0, pl.ds(rs, 8), :]
        s1 = s_ref[1, pl.ds(rs, 8), :]
        s0 = s0 * jax.nn.sigmoid(s0)
        s1 = s1 * jax.nn.sigmoid(s1)
        a0 = a0 + w * jnp.concatenate([s0] * rep, axis=1)
        a1 = a1 + w * jnp.concatenate([s1] * rep, axis=1)
        return a0, a1

    a0, a1 = lax.fori_loop(0, tk // 8, body, (acc_ref[0], acc_ref[1]), unroll=4)
    acc_ref[0] = a0
    acc_ref[1] = a1

    @pl.when(k == pl.num_programs(2) - 1)
    def _():
        o_ref[0, 0:1, :] = jnp.sum(a0, axis=0, keepdims=True) + b_ref[0]
        o_ref[0, 1:2, :] = jnp.sum(a1, axis=0, keepdims=True) + b_ref[0]


def ada_mod(cc, ada_w, ada_b):
    L, K, N = ada_w.shape
    tk, tn = 2048, 1024
    s_b = jnp.broadcast_to(cc[:, :, None], (2, K, LANE))
    return pl.pallas_call(
        _ada_kernel,
        grid=(L, N // tn, K // tk),
        in_specs=[
            pl.BlockSpec((2, K, LANE), lambda l, j, k: (0, 0, 0)),
            pl.BlockSpec((1, tk, tn), lambda l, j, k: (l, k, j)),
            pl.BlockSpec((1, 1, tn), lambda l, j, k: (l, 0, j)),
        ],
        out_specs=pl.BlockSpec((1, 2, tn), lambda l, j, k: (l, 0, j)),
        out_shape=jax.ShapeDtypeStruct((L, 2, N), F32),
        scratch_shapes=[pltpu.VMEM((2, 8, tn), F32)],
        compiler_params=_cparams(("parallel", "parallel", "arbitrary")),
    )(s_b, ada_w, ada_b.reshape(L, 1, N))


def _normmod_kernel(x_ref, nw_ref, sc_ref, sh_ref, o_ref):
    x = x_ref[...]
    ms = jnp.mean(x * x, axis=-1, keepdims=True)
    y = x * lax.rsqrt(ms + NORM_EPS) * nw_ref[...]
    o_ref[...] = (y * (1.0 + sc_ref[...]) + sh_ref[...]).astype(o_ref.dtype)


LORA_COLS = 2 * W_LORA + 2 * A_LORA + G_LORA
W_IN_BLOCKS = IN_COLS_PAD // 512
W_IN_FRONT = FOURIER_WIDTH // 512
W_IN_MID = W_IN_BLOCKS - 1 - W_IN_FRONT
CAST_ROWS, CAST_COLS = 256, 1024
WIN_TK = 1024


def _w_in_src_block(j):
    return jnp.where(j < W_IN_MID, j + W_IN_FRONT, jnp.where(j < W_IN_BLOCKS - 1, j - W_IN_MID, W_IN_BLOCKS - 1))


def _cast_block(s_ref, so_ref, kind, block):
    if kind == 'plain':
        so_ref[...] = s_ref[...].astype(BF16)
    else:
        row = lax.broadcasted_iota(jnp.int32, s_ref.shape[1:], 0)
        keep = (block % W_IN_BLOCKS < W_IN_BLOCKS - 1) | (row < LORA_COLS)
        so_ref[0] = jnp.where(keep, s_ref[0], 0.0).T.astype(BF16)


def _cast_blocks(kind, src, rows):
    K = src.shape[1]
    return (K // rows) * (src.shape[2] // CAST_COLS if kind == 'plain' else W_IN_BLOCKS)


def _cast_specs(kind, src, l, block_of, rows):
    nblocks = _cast_blocks(kind, src, rows)
    if kind == 'plain':
        _, K, N = src.shape
        ncb = N // CAST_COLS
        blk = (1, rows, CAST_COLS)
        in_spec = pl.BlockSpec(blk, lambda *g: (l, block_of(*g) // ncb, block_of(*g) % ncb))
        out_spec = pl.BlockSpec(blk, lambda *g: (0, block_of(*g) // ncb, block_of(*g) % ncb))
        return src, in_spec, out_spec, jax.ShapeDtypeStruct((1, K, N), BF16), nblocks
    _, K, _ = src.shape
    in_spec = pl.BlockSpec((1, 512, rows), lambda *g: (l, _w_in_src_block(block_of(*g) % W_IN_BLOCKS),
                                                       block_of(*g) // W_IN_BLOCKS))
    out_spec = pl.BlockSpec((1, rows, 512), lambda *g: (0, block_of(*g) // W_IN_BLOCKS, block_of(*g) % W_IN_BLOCKS))
    return jnp.swapaxes(src, 1, 2), in_spec, out_spec, jax.ShapeDtypeStruct((1, K, IN_COLS_PAD), BF16), nblocks


def _castonly_kernel(s_ref, so_ref, *, kind):
    _cast_block(s_ref, so_ref, kind, pl.program_id(0))


def cast_layer(kind, w, l):
    src, in_spec, out_spec, out_shape, nblocks = _cast_specs(kind, w, l, lambda b: b, WIN_TK)
    return pl.pallas_call(
        functools.partial(_castonly_kernel, kind=kind),
        grid=(nblocks,),
        in_specs=[in_spec],
        out_specs=out_spec,
        out_shape=out_shape,
        compiler_params=_cparams(("parallel",)),
    )(src)


def _side_cast_specs(side, grid):
    kind, w, l = side
    steps = grid[0] * grid[1]
    K = w.shape[1]
    start = CAST_ROWS if kind == 'plain' else WIN_TK
    rows = next((r for r in (start * 2 ** e for e in range(8)) if r <= K and K % r == 0
                 and _cast_blocks(kind, w, r) <= steps), None)
    assert rows is not None, "the matmul grid has too few steps to cast every block of the weight"
    nb = _cast_blocks(kind, w, rows)
    return _cast_specs(kind, w, l, lambda i, j: jnp.minimum(i * grid[1] + j, nb - 1), rows) + (kind,)


def _mm_kernel(a_ref, w_ref, *rest, side_kind, side_blocks):
    if side_kind is None:
        (o_ref,) = rest
    else:
        s_ref, o_ref, so_ref = rest
        step = pl.program_id(0) * pl.num_programs(1) + pl.program_id(1)
        _cast_block(s_ref, so_ref, side_kind, jnp.minimum(step, side_blocks - 1))
    o_ref[...] = _dot(a_ref[...], w_ref[0]).astype(o_ref.dtype)


def norm_mod_matmul(x, nw, sc, sh, w, l, out_dtype=F32, side=None):
    M, K = x.shape
    N = w.shape[2]
    tr = min(512, M)
    vec = pl.BlockSpec((1, K), lambda i: (0, 0))
    h = pl.pallas_call(
        _normmod_kernel,
        grid=(M // tr,),
        in_specs=[pl.BlockSpec((tr, K), lambda i: (i, 0)), vec, vec, vec],
        out_specs=pl.BlockSpec((tr, K), lambda i: (i, 0)),
        out_shape=jax.ShapeDtypeStruct((M, K), BF16),
        compiler_params=_cparams(("parallel",)),
    )(x, nw.reshape(1, K), sc.reshape(1, K), sh.reshape(1, K))
    tm = min(1024, M)
    tn = next(t for t in (1024, 768, 512) if N % t == 0)
    grid = (M // tm, N // tn)
    args = [h, w]
    in_specs = [
        pl.BlockSpec((tm, K), lambda i, j: (i, 0)),
        pl.BlockSpec((1, K, tn), lambda i, j: (l, 0, j)),
    ]
    out_specs = [pl.BlockSpec((tm, tn), lambda i, j: (i, j))]
    out_shape = [jax.ShapeDtypeStruct((M, N), out_dtype)]
    side_kind, side_blocks = None, 0
    if side is not None:
        src, s_in, s_out, s_shape, side_blocks, side_kind = _side_cast_specs(side, grid)
        args.append(src)
        in_specs.append(s_in)
        out_specs.append(s_out)
        out_shape.append(s_shape)
    outs = pl.pallas_call(
        functools.partial(_mm_kernel, side_kind=side_kind, side_blocks=side_blocks),
        grid=grid,
        in_specs=in_specs,
        out_specs=out_specs,
        out_shape=out_shape,
        compiler_params=_cparams(("arbitrary", "arbitrary") if side is not None else ("parallel", "arbitrary")),
    )(*args)
    return outs[0] if side is None else tuple(outs)


def _mmres_kernel(*refs, ksplits, side_kind, side_blocks):
    n = len(ksplits)
    a_refs = refs[:n]
    if side_kind is None:
        w_ref, x_ref, g_ref, o_ref = refs[n:]
    else:
        w_ref, x_ref, g_ref, s_ref, o_ref, so_ref = refs[n:]
        step = pl.program_id(0) * pl.num_programs(1) + pl.program_id(1)
        _cast_block(s_ref, so_ref, side_kind, jnp.minimum(step, side_blocks - 1))
    acc = None
    off = 0
    for a_ref, kp in zip(a_refs, ksplits):
        part = _dot(a_ref[...].astype(BF16), w_ref[0, off:off + kp, :])
        acc = part if acc is None else acc + part
        off += kp
    o_ref[...] = x_ref[...] + g_ref[...] * acc


def matmul_residual(parts, w, l, x, gate, side=None):
    M, N = x.shape
    K = w.shape[1]
    ksplits = tuple(p.shape[1] for p in parts)
    assert sum(ksplits) == K
    tm = min(1024, M)
    tn = 512
    grid = (M // tm, N // tn)
    args = list(parts) + [w, x, gate.reshape(1, N)]
    in_specs = [pl.BlockSpec((tm, kp), lambda i, j: (i, 0)) for kp in ksplits]
    in_specs += [
        pl.BlockSpec((1, K, tn), lambda i, j: (l, 0, j)),
        pl.BlockSpec((tm, tn), lambda i, j: (i, j)),
        pl.BlockSpec((1, tn), lambda i, j: (0, j)),
    ]
    out_specs = [pl.BlockSpec((tm, tn), lambda i, j: (i, j))]
    out_shape = [jax.ShapeDtypeStruct((M, N), F32)]
    side_kind, side_blocks = None, 0
    if side is not None:
        src, s_in, s_out, s_shape, side_blocks, side_kind = _side_cast_specs(side, grid)
        args.append(src)
        in_specs.append(s_in)
        out_specs.append(s_out)
        out_shape.append(s_shape)
    outs = pl.pallas_call(
        functools.partial(_mmres_kernel, ksplits=ksplits, side_kind=side_kind, side_blocks=side_blocks),
        grid=grid,
        in_specs=in_specs,
        out_specs=out_specs,
        out_shape=out_shape,
        compiler_params=_cparams(("arbitrary", "arbitrary") if side is not None else ("parallel", "arbitrary")),
    )(*args)
    return outs[0] if side is None else tuple(outs)


def _conv3(main, prev_row, next_row, w):
    tm = main.shape[0]
    row = lax.broadcasted_iota(jnp.int32, main.shape, 0)
    dn = jnp.where(row == 0, prev_row, pltpu.roll(main, 1, axis=0))
    up = jnp.where(row == tm - 1, next_row, pltpu.roll(main, tm - 1, axis=0))
    return dn * w[0:1, :] + main * w[1:2, :] + up * w[2:3, :]


def _halo_rows(prev_ref, next_ref, i, n_i):
    prev = prev_ref[...].astype(F32)
    nxt = next_ref[...].astype(F32)
    hr = prev.shape[0]
    prev_row = jnp.where(i == 0, 0.0, prev[hr - 1:hr, :])
    next_row = jnp.where(i == n_i - 1, 0.0, nxt[0:1, :])
    return prev_row, next_row


def _halo_specs(tm, tn, n_rows, col_fn, hr=8):
    rb = tm // hr
    last = n_rows // hr - 1
    return [
        pl.BlockSpec((tm, tn), lambda i, j: (i, col_fn(j))),
        pl.BlockSpec((hr, tn), lambda i, j: (jnp.maximum(i * rb - 1, 0), col_fn(j))),
        pl.BlockSpec((hr, tn), lambda i, j: (jnp.minimum((i + 1) * rb, last), col_fn(j))),
    ]


def _convgate_kernel(a_ref, ap_ref, an_ref, b_ref, bp_ref, bn_ref, wa_ref, wb_ref, o_ref):
    i = pl.program_id(0)
    n_i = pl.num_programs(0)
    pa, na = _halo_rows(ap_ref, an_ref, i, n_i)
    pb, nb = _halo_rows(bp_ref, bn_ref, i, n_i)
    a = _conv3(a_ref[...].astype(F32), pa, na, wa_ref[...])
    b = _conv3(b_ref[...].astype(F32), pb, nb, wb_ref[...])
    o_ref[...] = (a * jax.nn.sigmoid(a) * b).astype(o_ref.dtype)


def conv_gate(u, conv_w):
    M, N2 = u.shape
    F = N2 // 2
    tm = min(512, M)
    tn = 1024
    nb = F // tn
    specs = _halo_specs(tm, tn, M, lambda j: j, 16) + _halo_specs(tm, tn, M, lambda j: j + nb, 16)
    specs += [pl.BlockSpec((3, tn), lambda i, j: (0, j)), pl.BlockSpec((3, tn), lambda i, j: (0, j + nb))]
    return pl.pallas_call(
        _convgate_kernel,
        grid=(M // tm, nb),
        in_specs=specs,
        out_specs=pl.BlockSpec((tm, tn), lambda i, j: (i, j)),
        out_shape=jax.ShapeDtypeStruct((M, F), BF16),
        compiler_params=_cparams(("parallel", "parallel")),
    )(u, u, u, u, u, u, conv_w, conv_w)


def _dft_tables(T):
    A, B = T // FFT_B, FFT_B
    ka = np.arange(A)
    a = np.arange(A)
    b = np.arange(B)
    n = (ka[None, :, None] * (B * a[None, None, :] + b[:, None, None])) % T
    ang = n * (2.0 * np.pi / T)
    sa = 1.0 / math.sqrt(A)
    m1 = np.concatenate([np.cos(ang) * sa, -np.sin(ang) * sa], axis=1).astype(np.float32)
    kb = np.arange(B)
    ang2 = ((kb[:, None] * b[None, :]) % B) * (2.0 * np.pi / B)
    sb = 1.0 / math.sqrt(B)
    m2 = np.concatenate([np.cos(ang2) * sb, np.sin(ang2) * sb], axis=1).astype(np.float32)
    return jnp.asarray(m1, BF16), jnp.asarray(m2, BF16)


def _channel_dft():
    c = np.arange(FOURIER_DIM)
    ang = 2.0 * np.pi * ((c[:, None] * c[None, :]) % FOURIER_DIM) / FOURIER_DIM
    s = 1.0 / math.sqrt(FOURIER_DIM)
    return np.cos(ang) * s, np.sin(ang) * s


FFT_SUB = 8


def _f1_kernel(x_ref, m_ref, ch_ref, pm_ref, re_ref, im_ref):
    A = x_ref.shape[0]
    ch = ch_ref[...]
    x_all = x_ref[...].reshape(A * FFT_SUB, x_ref.shape[2]).astype(BF16)
    xp = _dot(pm_ref[...], x_all).astype(BF16)
    for j in range(FFT_SUB):
        y = _dot(m_ref[j], xp[j * A:(j + 1) * A])
        for h in range(2):
            sl = slice(h * FOURIER_DIM, (h + 1) * FOURIER_DIM)
            lhs = jnp.concatenate([y[:A, sl], y[A:, sl]], axis=1).astype(BF16)
            yp = _dot(lhs, ch)
            re_ref[j, :, sl] = yp[:, :FOURIER_DIM]
            im_ref[j, :, sl] = yp[:, FOURIER_DIM:]


def _f2_kernel(re_ref, im_ref, m_ref, w_ref, b_ref, o_ref):
    for j in range(FFT_SUB):
        rhs = jnp.concatenate([re_ref[:, j, :], im_ref[:, j, :]], axis=0).astype(BF16)
        spec = _dot(m_ref[...], rhs)
        outs = []
        for h in range(FOURIER_HEADS):
            sl = slice(h * FOURIER_DIM, (h + 1) * FOURIER_DIM)
            outs.append(_dot(spec[:, sl].astype(BF16), w_ref[h].astype(BF16)) + b_ref[h])
        o_ref[:, j, :] = jnp.concatenate(outs, axis=1)


def fourier_mix(z, fw, fb):
    T, NC = z.shape
    A, B = T // FFT_B, FFT_B
    m1, m2 = _dft_tables(T)
    cc, ss = _channel_dft()
    ch = jnp.asarray(np.block([[cc, -ss], [ss, cc]]), BF16)
    cb0 = COL_F // 512
    n = A * FFT_SUB
    dst = np.arange(n)
    pm = jnp.asarray((((dst % A) * FFT_SUB + dst // A)[:, None] == np.arange(n)[None, :]).astype(np.float32), BF16)
    yre, yim = pl.pallas_call(
        _f1_kernel,
        grid=(B // FFT_SUB, 2),
        in_specs=[
            pl.BlockSpec((A, FFT_SUB, 512), lambda b, c: (0, b, cb0 + c)),
            pl.BlockSpec((FFT_SUB, 2 * A, A), lambda b, c: (b, 0, 0)),
            pl.BlockSpec((512, 512), lambda b, c: (0, 0)),
            pl.BlockSpec((n, n), lambda b, c: (0, 0)),
        ],
        out_specs=[pl.BlockSpec((FFT_SUB, A, 512), lambda b, c: (b, 0, c))] * 2,
        out_shape=[jax.ShapeDtypeStruct((B, A, FOURIER_WIDTH), F32)] * 2,
        compiler_params=_cparams(("parallel", "parallel")),
    )(z.reshape(A, B, NC), m1, ch, pm)
    blk = pl.BlockSpec((B, FFT_SUB, FOURIER_WIDTH), lambda i: (0, i, 0))
    out = pl.pallas_call(
        _f2_kernel,
        grid=(A // FFT_SUB,),
        in_specs=[
            blk, blk,
            pl.BlockSpec((B, 2 * B), lambda i: (0, 0)),
            pl.BlockSpec((FOURIER_HEADS, FOURIER_DIM, FOURIER_DIM), lambda i: (0, 0, 0)),
            pl.BlockSpec((FOURIER_HEADS, 1, FOURIER_DIM), lambda i: (0, 0, 0)),
        ],
        out_specs=blk,
        out_shape=jax.ShapeDtypeStruct((B, A, FOURIER_WIDTH), F32),
        compiler_params=_cparams(("parallel",)),
    )(yre, yim, m2, fw, fb.reshape(FOURIER_HEADS, 1, FOURIER_DIM))
    return out.reshape(T, FOURIER_WIDTH)


def _fctx_kernel(f_ref, cs_ref, ts_ref, w_ref, b_ref, o_ref):
    for h in range(FOURIER_HEADS):
        sl = slice(h * FOURIER_DIM, (h + 1) * FOURIER_DIM)
        g = _dot(f_ref[:, sl].astype(BF16), cs_ref[...])
        gg = jnp.concatenate([g[:, :FOURIER_DIM], g[:, FOURIER_DIM:]], axis=0).astype(BF16)
        spec = _dot(ts_ref[...], gg)
        out = _dot(spec.astype(BF16), w_ref[h].astype(BF16)) + b_ref[h]
        o_ref[:, sl] = out.astype(o_ref.dtype)


def fourier_mix_ctx(zc, fw, fb):
    T = zc.shape[0]
    cc, ss = _channel_dft()
    cs = jnp.asarray(np.concatenate([cc, ss], axis=1), BF16)
    t = np.arange(T)
    ang = 2.0 * np.pi * ((t[:, None] * t[None, :]) % T) / T
    st = 1.0 / math.sqrt(T)
    ts = jnp.asarray(np.concatenate([np.cos(ang) * st, -np.sin(ang) * st], axis=1), BF16)
    return pl.pallas_call(
        _fctx_kernel,
        grid=(1,),
        in_specs=[
            pl.BlockSpec((T, FOURIER_WIDTH), lambda i: (0, COL_F // FOURIER_WIDTH)),
            pl.BlockSpec((FOURIER_DIM, 2 * FOURIER_DIM), lambda i: (0, 0)),
            pl.BlockSpec((T, 2 * T), lambda i: (0, 0)),
            pl.BlockSpec((FOURIER_HEADS, FOURIER_DIM, FOURIER_DIM), lambda i: (0, 0, 0)),
            pl.BlockSpec((FOURIER_HEADS, 1, FOURIER_DIM), lambda i: (0, 0, 0)),
        ],
        out_specs=pl.BlockSpec((T, FOURIER_WIDTH), lambda i: (0, 0)),
        out_shape=jax.ShapeDtypeStruct((T, FOURIER_WIDTH), BF16),
        compiler_params=_cparams(("arbitrary",)),
    )(zc, cs, ts, fw, fb.reshape(FOURIER_HEADS, 1, FOURIER_DIM))


def _rope_tables(T):
    nf = NA_HEAD_DIM // 4
    t = np.arange(T)
    inv = 1.0 / (ROPE_THETA ** (np.arange(nf) / nf))
    lane = np.arange(NA_HEAD_DIM)
    pos = np.where(lane[None, :] < NA_HEAD_DIM // 2, (t // GRID_W)[:, None], (t % GRID_W)[:, None])
    ang = pos * inv[lane % nf][None, :]
    sign = np.where((lane % (2 * nf)) < nf, -1.0, 1.0)[None, :]
    return jnp.asarray(np.cos(ang), F32), jnp.asarray(np.sin(ang) * sign, F32)


def _head_norm_rope(x, w, cos, sin, perm):
    ms = jnp.mean(x * x, axis=-1, keepdims=True)
    y = x * lax.rsqrt(ms + NORM_EPS) * w
    if cos is None:
        return y
    return y * cos + _dot(y.astype(BF16), perm) * sin


def _qkprep_kernel(q_ref, k_ref, qw_ref, kw_ref, cos_ref, sin_ref, perm_ref, qo_ref, ko_ref, *, rope):
    cos = cos_ref[...] if rope else None
    sin = sin_ref[...] if rope else None
    perm = perm_ref[...]
    qw = qw_ref[...] * (NA_HEAD_DIM ** -0.5)
    kw = kw_ref[...]
    for h in range(4):
        sl = slice(h * LANE, (h + 1) * LANE)
        qo_ref[:, sl] = _head_norm_rope(q_ref[:, sl], qw, cos, sin, perm).astype(qo_ref.dtype)
        ko_ref[:, sl] = _head_norm_rope(k_ref[:, sl], kw, cos, sin, perm).astype(ko_ref.dtype)


def qk_prep(z, qw, kw, rope):
    T = z.shape[0]
    tm = min(512, T)
    if rope:
        cos, sin = _rope_tables(T)
    else:
        cos = sin = jnp.zeros((T, LANE), F32)
    nq = NA_WIDTH // 512
    lane = np.arange(LANE)
    src = np.where(lane % 64 < 32, lane + 32, lane - 32)
    perm = jnp.asarray((lane[:, None] == src[None, :]).astype(np.float32), BF16)
    return pl.pallas_call(
        functools.partial(_qkprep_kernel, rope=rope),
        grid=(T // tm, nq),
        in_specs=[
            pl.BlockSpec((tm, 512), lambda i, j: (i, COL_Q // 512 + j)),
            pl.BlockSpec((tm, 512), lambda i, j: (i, COL_K // 512 + j)),
            pl.BlockSpec((1, LANE), lambda i, j: (0, 0)),
            pl.BlockSpec((1, LANE), lambda i, j: (0, 0)),
            pl.BlockSpec((tm, LANE), lambda i, j: (i, 0)),
            pl.BlockSpec((tm, LANE), lambda i, j: (i, 0)),
            pl.BlockSpec((LANE, LANE), lambda i, j: (0, 0)),
        ],
        out_specs=[pl.BlockSpec((tm, 512), lambda i, j: (i, j))] * 2,
        out_shape=[jax.ShapeDtypeStruct((T, NA_WIDTH), BF16)] * 2,
        compiler_params=_cparams(("parallel", "parallel")),
    )(z, z, qw.reshape(1, LANE), kw.reshape(1, LANE), cos, sin, perm)


def _toeplitz_kernel(r_ref, e_ref, o_ref):
    r = r_ref[0]
    acc = jnp.zeros(o_ref.shape[1:], F32)
    for d in range(2 * NA_KW - 1):
        acc = acc + r[:, d:d + 1] * e_ref[d:d + 1, :]
    o_ref[0] = acc


def rpb_bias_tables(rpb):
    L, H = rpb.shape[0], rpb.shape[1]
    ndr, ndc = 2 * NA_KH - 1, 2 * NA_KW - 1
    q = np.arange(GRID_W)
    e = np.zeros((32, GRID_W, GRID_W), np.float32)
    for d in range(ndc):
        e[d] = (q[None, :] - q[:, None] + (NA_KW - 1)) == d
    e = jnp.asarray(e.reshape(32, GRID_W * GRID_W))
    rp = jnp.pad(rpb.reshape(L * H, ndr, ndc), ((0, 0), (0, 16 - ndr), (0, 32 - ndc)))
    toep = pl.pallas_call(
        _toeplitz_kernel,
        grid=(L * H,),
        in_specs=[pl.BlockSpec((1, 16, 32), lambda i: (i, 0, 0)),
                  pl.BlockSpec((32, GRID_W * GRID_W), lambda i: (0, 0))],
        out_specs=pl.BlockSpec((1, 16, GRID_W * GRID_W), lambda i: (i, 0, 0)),
        out_shape=jax.ShapeDtypeStruct((L * H, 16, GRID_W * GRID_W), F32),
        compiler_params=_cparams(("parallel",)),
    )(rp, e)
    toep = toep.reshape(L, H, 16, GRID_W, GRID_W)
    col_start = np.clip(q - NA_KW // 2, 0, GRID_W - NA_KW)
    in_win = (q[None, :] >= col_start[:, None]) & (q[None, :] < col_start[:, None] + NA_KW)
    mask = jnp.asarray(np.where(in_win, 0.0, -1e30).astype(np.float32))
    dr = np.arange(NA_KH)[None, :] - np.arange(NA_KH)[:, None] + (NA_KH - 1)
    band = toep[:, :, dr] + mask
    return jnp.transpose(band, (0, 1, 2, 4, 3, 5)).reshape(L, H, NA_KH, GRID_W, NA_KH * GRID_W)


NA_HEADS_PER_STEP = 2


def _na_kernel(q_ref, kp_ref, kc_ref, kn_ref, vp_ref, vc_ref, vn_ref, kx_ref, vx_ref, bt_ref, o_ref,
               kbuf, vbuf, *, nrows):
    m = pl.program_id(1)
    blk = NA_KH * GRID_W
    kbuf[0:blk] = kp_ref[...]
    kbuf[blk:2 * blk] = kc_ref[...]
    kbuf[2 * blk:3 * blk] = kn_ref[...]
    vbuf[0:blk] = vp_ref[...].astype(BF16)
    vbuf[blk:2 * blk] = vc_ref[...].astype(BF16)
    vbuf[2 * blk:3 * blk] = vn_ref[...].astype(BF16)
    kctx = kx_ref[...]
    vctx = vx_ref[...].astype(BF16)
    work, scores = [], []
    for hh in range(NA_HEADS_PER_STEP):
        hs = slice(hh * LANE, (hh + 1) * LANE)
        for j in range(NA_KH):
            r = m * NA_KH + j
            rs = jnp.clip(r - NA_KH // 2, 0, nrows - NA_KH)
            start = pl.multiple_of((rs - (m - 1) * NA_KH) * GRID_W, GRID_W)
            q = q_ref[j * GRID_W:(j + 1) * GRID_W, hs]
            s = _dot_nt(q, kbuf[pl.ds(start, blk), hs]) + bt_ref[hh, r - rs]
            work.append((hs, j, start))
            scores.append((s, _dot_nt(q, kctx[:, hs])))
    probs = []
    for s, sc in scores:
        mx = jnp.maximum(jnp.max(s, axis=-1, keepdims=True), jnp.max(sc, axis=-1, keepdims=True))
        p = jnp.exp(s - mx)
        pc = jnp.exp(sc - mx)
        den = jnp.sum(p, axis=-1, keepdims=True) + jnp.sum(pc, axis=-1, keepdims=True)
        probs.append((p.astype(BF16), pc.astype(BF16), den))
    for (hs, j, start), (p, pc, den) in zip(work, probs):
        acc = _dot(p, vbuf[pl.ds(start, blk), hs]) + _dot(pc, vctx[:, hs])
        o_ref[j * GRID_W:(j + 1) * GRID_W, hs] = (acc / den).astype(o_ref.dtype)


def na_attention(qn, kn, z, kcn, zc, bias_tab):
    T = qn.shape[0]
    C = kcn.shape[0]
    nrows = T // GRID_W
    blk = NA_KH * GRID_W
    nblk = T // blk
    hw = NA_HEADS_PER_STEP * LANE
    vcol = COL_V // hw
    prev = lambda h, m: (jnp.maximum(m - 1, 0), h)
    cur = lambda h, m: (m, h)
    nxt = lambda h, m: (jnp.minimum(m + 1, nblk - 1), h)
    vprev = lambda h, m: (jnp.maximum(m - 1, 0), vcol + h)
    vcur = lambda h, m: (m, vcol + h)
    vnxt = lambda h, m: (jnp.minimum(m + 1, nblk - 1), vcol + h)
    return pl.pallas_call(
        functools.partial(_na_kernel, nrows=nrows),
        grid=(NA_HEADS // NA_HEADS_PER_STEP, nblk),
        in_specs=[
            pl.BlockSpec((blk, hw), cur),
            pl.BlockSpec((blk, hw), prev), pl.BlockSpec((blk, hw), cur), pl.BlockSpec((blk, hw), nxt),
            pl.BlockSpec((blk, hw), vprev), pl.BlockSpec((blk, hw), vcur), pl.BlockSpec((blk, hw), vnxt),
            pl.BlockSpec((C, hw), lambda h, m: (0, h)),
            pl.BlockSpec((C, hw), lambda h, m: (0, vcol + h)),
            pl.BlockSpec((NA_HEADS_PER_STEP, NA_KH, GRID_W, blk), lambda h, m: (h, 0, 0, 0)),
        ],
        out_specs=pl.BlockSpec((blk, hw), cur),
        out_shape=jax.ShapeDtypeStruct((T, NA_WIDTH), BF16),
        scratch_shapes=[pltpu.VMEM((3 * blk, hw), BF16), pltpu.VMEM((3 * blk, hw), BF16)],
        compiler_params=_cparams(("parallel", "parallel")),
    )(qn, kn, kn, kn, z, z, z, kcn, zc, bias_tab)


def _ctxattn_kernel(q_ref, k_ref, v_ref, o_ref):
    s = _dot_nt(q_ref[...], k_ref[...])
    p = jnp.exp(s - jnp.max(s, axis=-1, keepdims=True))
    den = jnp.sum(p, axis=-1, keepdims=True)
    o_ref[...] = (_dot(p.astype(BF16), v_ref[...].astype(BF16)) / den).astype(o_ref.dtype)


def ctx_attention(qcn, kcn, zc):
    C = qcn.shape[0]
    vcol = COL_V // LANE
    return pl.pallas_call(
        _ctxattn_kernel,
        grid=(NA_HEADS,),
        in_specs=[pl.BlockSpec((C, LANE), lambda h: (0, h)), pl.BlockSpec((C, LANE), lambda h: (0, h)),
                  pl.BlockSpec((C, LANE), lambda h: (0, vcol + h))],
        out_specs=pl.BlockSpec((C, LANE), lambda h: (0, h)),
        out_shape=jax.ShapeDtypeStruct((C, NA_WIDTH), BF16),
        compiler_params=_cparams(("parallel",)),
    )(qcn, kcn, zc)


def _seg_sum64(x):
    lane = lax.broadcasted_iota(jnp.int32, x.shape, 1)
    low = lane < RWKV_HEAD_DIM
    s_lo = jnp.sum(jnp.where(low, x, 0.0), axis=-1, keepdims=True)
    s_hi = jnp.sum(jnp.where(low, 0.0, x), axis=-1, keepdims=True)
    return jnp.where(low, s_lo, s_hi)


def _seg_sum(x):
    return jnp.concatenate([_seg_sum64(x[:, i * LANE:(i + 1) * LANE]) for i in range(x.shape[1] // LANE)], axis=1)


def _rwkvprep_kernel(r_ref, rp_ref, rn_ref, k_ref, kp_ref, kn_ref, v_ref, vp_ref, vn_ref, lo_ref, cw_ref,
                     w2_ref, a2_ref, g2_ref, w0_ref, a0_ref, kk_ref, ka_ref, rk_ref,
                     r_o, v_o, kk_o, g_o, bonus_o, logw_o, kd_o, ag_o):
    i = pl.program_id(0)
    n_i = pl.num_programs(0)
    W = RWKV_WIDTH
    cw = cw_ref[...]
    r = _conv3(r_ref[...], *_halo_rows(rp_ref, rn_ref, i, n_i), cw[:, 0:W])
    k = _conv3(k_ref[...], *_halo_rows(kp_ref, kn_ref, i, n_i), cw[:, W:2 * W])
    v = _conv3(v_ref[...], *_halo_rows(vp_ref, vn_ref, i, n_i), cw[:, 2 * W:3 * W])
    lora = lo_ref[...]
    wl = _dot(jnp.tanh(lora[:, 0:2 * W_LORA]).astype(BF16), w2_ref[...])
    al = _dot(lora[:, 2 * W_LORA:2 * (W_LORA + A_LORA)].astype(BF16), a2_ref[...])
    g = _dot(jax.nn.sigmoid(lora[:, 2 * (W_LORA + A_LORA):]).astype(BF16), g2_ref[...])
    kkr = k * kk_ref[...]
    kk = kkr * lax.rsqrt(jnp.maximum(_seg_sum(kkr * kkr), 1e-24))
    kds = []
    for d in range(2):
        logw = -math.exp(-0.5) * jax.nn.sigmoid(w0_ref[d:d + 1, :] + wl[:, d * W:(d + 1) * W])
        a = jax.nn.sigmoid(a0_ref[d:d + 1, :] + al[:, d * W:(d + 1) * W])
        kd = k * (1.0 + (a - 1.0) * ka_ref[...])
        kds.append(kd)
        for gi in range(N_GROUPS):
            sl = slice(gi * GROUP, (gi + 1) * GROUP)
            logw_o[d, gi] = logw[:, sl]
            kd_o[d, gi] = kd[:, sl].astype(kd_o.dtype)
            ag_o[d, gi] = a[:, sl].astype(ag_o.dtype)
    bonus = _seg_sum(r * (kds[0] + kds[1]) * rk_ref[...]) * v
    for gi in range(N_GROUPS):
        sl = slice(gi * GROUP, (gi + 1) * GROUP)
        r_o[gi] = r[:, sl].astype(r_o.dtype)
        v_o[gi] = v[:, sl].astype(v_o.dtype)
        kk_o[gi] = kk[:, sl].astype(kk_o.dtype)
        g_o[gi] = g[:, sl].astype(g_o.dtype)
        bonus_o[gi] = bonus[:, sl].astype(bonus_o.dtype)


def rwkv_prep(z, p):
    T = z.shape[0]
    tm = min(128, T)
    W = RWKV_WIDTH
    cb = COL_R // W
    specs = []
    for c in range(3):
        specs += _halo_specs(tm, W, T, lambda j, c=c: cb + c)
    specs = [pl.BlockSpec(s.block_shape, lambda i, f=s.index_map: f(i, 0)) for s in specs]
    full = lambda shape: pl.BlockSpec(shape, lambda i: (0,) * len(shape))
    specs += [
        pl.BlockSpec((tm, LORA_PAD), lambda i: (i, COL_L // LORA_PAD)),
        full((3, 3 * W)), full((2 * W_LORA, 2 * W)), full((2 * A_LORA, 2 * W)), full((LORA_PAD - 256, W)),
        full((2, W)), full((2, W)), full((1, W)), full((1, W)), full((1, W)),
    ]
    g1 = pl.BlockSpec((N_GROUPS, tm, GROUP), lambda i: (0, i, 0))
    g2 = pl.BlockSpec((2, N_GROUPS, tm, GROUP), lambda i: (0, 0, i, 0))
    s1 = jax.ShapeDtypeStruct((N_GROUPS, T, GROUP), BF16)
    s2 = jax.ShapeDtypeStruct((2, N_GROUPS, T, GROUP), BF16)
    s2f = jax.ShapeDtypeStruct((2, N_GROUPS, T, GROUP), F32)
    return pl.pallas_call(
        _rwkvprep_kernel,
        grid=(T // tm,),
        in_specs=specs,
        out_specs=[g1] * 5 + [g2] * 3,
        out_shape=[s1] * 5 + [s2f, s2, s2],
        compiler_params=_cparams(("parallel",)),
    )(z, z, z, z, z, z, z, z, z, z, p['rwkv_conv'], p['w2bd'], p['a2bd'], p['g2p'], p['w0'], p['a0'],
      p['k_k'].reshape(1, W), p['k_a'].reshape(1, W), p['r_k'].reshape(1, W))


def _fold(x):
    c = CHUNK
    return x[0:c] + x[c:2 * c] + x[2 * c:3 * c] + x[3 * c:4 * c]


def _rwkv_kernel(r_ref, v_ref, kk_ref, logw_ref, kd_ref, ag_ref, tri_ref, ms_ref, mi_ref, bd_ref, bdf_ref, z0_ref,
                 y_ref, zf_ref, z_scr):
    forward = pl.program_id(0) == 0
    c = pl.program_id(1)

    @pl.when(c == 0)
    def _():
        z_scr[...] = z0_ref[0]

    tri = tri_ref[0]
    m_strict = ms_ref[0]
    m_incl = mi_ref[0]
    bd = bd_ref[...]
    bdf = bdf_ref[...]
    row = lax.broadcasted_iota(jnp.int32, (CHUNK, GROUP), 0)
    col = lax.broadcasted_iota(jnp.int32, (CHUNK, GROUP), 1)
    diag = row == (col % RWKV_HEAD_DIM)
    eye_f = jnp.where(diag, 1.0, 0.0)

    def expand(x):
        return jnp.concatenate([x.astype(BF16)] * 4, axis=0) * bd

    def stack(*xs):
        return jnp.concatenate([x.astype(BF16) for x in xs], axis=0)

    cs_all = _dot(tri, jnp.concatenate([h for g in range(N_GROUPS) for h in _split(logw_ref[0, g])], axis=1))

    def prep(g):
        lw = logw_ref[0, g]
        cs = cs_all[:, 2 * g * GROUP:2 * (g + 1) * GROUP]
        linc = cs[:, :GROUP] + cs[:, GROUP:]
        ltot = jnp.where(forward, linc[CHUNK - 1:CHUNK, :], linc[0:1, :])
        e_inc = jnp.exp(linc)
        e_neg = jnp.exp(-linc).astype(BF16)
        e_exc = jnp.exp(linc - lw).astype(BF16)
        e_rem = jnp.exp(ltot - linc).astype(BF16)
        kk = kk_ref[g]
        kd = kd_ref[0, g]
        b = kk * ag_ref[0, g]
        at = -(kk * e_exc)
        rt = r_ref[g].astype(F32) * e_inc
        gram = _dot_nt(stack(at, rt), jnp.concatenate([expand(b * e_neg), expand(kd * e_neg)], axis=0))
        fab = gram[:CHUNK, :GROUP] * m_strict
        return dict(g=g, rt=rt, at=at, e_tot=jnp.exp(ltot), bh=b * e_rem, kh=kd * e_rem, v=v_ref[g],
                    fp=fab, ft=eye_f + fab,
                    fak=gram[:CHUNK, GROUP:] * m_strict, frb=gram[CHUNK:, :GROUP] * m_incl,
                    frk=gram[CHUNK:, GROUP:] * m_incl)

    st = [prep(g) for g in range(N_GROUPS)]
    for s in st:
        res = _dot(stack(s['fp'], s['frb']), expand(s['fp']))
        s['fp'] = res[:CHUNK]
        s['fg'] = s['frb'] + res[CHUNK:]
    for _ in range(4):
        for s in st:
            res = _dot(stack(s['fp'], s['ft'], s['fg']), expand(s['fp']))
            s['fp'] = res[:CHUNK]
            s['ft'] = s['ft'] + res[CHUNK:2 * CHUNK]
            s['fg'] = s['fg'] + res[2 * CHUNK:]
    for s in st:
        res = _dot(stack(s['ft'], s['fg']), expand(s['fp']))
        s['ft'] = s['ft'] + res[:CHUNK]
        s['fg'] = s['fg'] + res[CHUNK:]
    for s in st:
        res = _dot(stack(s['ft'], s['fg']), jnp.concatenate([expand(s['fak']), expand(s['at'])], axis=1))
        s['fta'] = res[:CHUNK, :GROUP]
        s['ff'] = res[CHUNK:, :GROUP] + s['frk']
        s['fa1'] = res[:CHUNK, GROUP:]
        s['rp'] = s['rt'] + res[CHUNK:, GROUP:]
    for s in st:
        res = _dot(stack(s['fta'], s['ff']), expand(s['v']))
        s['fu0'] = res[:CHUNK]
        s['y0'] = res[CHUNK:]
    for s in st:
        zero = jnp.zeros((CHUNK, GROUP), BF16)
        rhs = jnp.concatenate([jnp.concatenate([s['fa1'].astype(BF16), zero], axis=1),
                               jnp.concatenate([zero, s['fu0'].astype(BF16)], axis=1),
                               jnp.concatenate([zero, s['v']], axis=1)], axis=0)
        mn = _dot_tn(stack(s['bh'], s['bh'], s['kh']), rhs)
        s['mf'] = _fold(mn[:, :GROUP] * bdf) + jnp.where(diag, s['e_tot'], 0.0)
        s['nf'] = _fold(mn[:, GROUP:] * bdf)
    for s in st:
        g = s['g']
        zhx = expand(z_scr[g])
        mh, ml = _split(s['mf'])
        res = _dot(jnp.concatenate([s['rp'].astype(BF16), mh, ml], axis=0), zhx)
        y_ref[0, g] = res[:CHUNK] + s['y0']
        z_scr[g] = res[CHUNK:2 * CHUNK] + res[2 * CHUNK:] + s['nf']

    @pl.when(c == pl.num_programs(1) - 1)
    def _():
        zf_ref[0] = z_scr[...]


def _chunk_masks():
    i = np.arange(CHUNK)
    tri = np.stack([i[None, :] <= i[:, None], i[None, :] >= i[:, None]]).astype(np.float32)
    j = np.arange(GROUP)
    same = ((j[:, None] // CHUNK) == (j[None, :] // CHUNK)).astype(np.float32)
    js = j[None, :] % CHUNK
    strict = np.stack([js < i[:, None], js > i[:, None]]).astype(np.float32)
    incl = np.stack([js <= i[:, None], js >= i[:, None]]).astype(np.float32)
    return jnp.asarray(tri, BF16), jnp.asarray(strict), jnp.asarray(incl), jnp.asarray(same, BF16), jnp.asarray(same)


def rwkv_scan(r, v, kk, logw, kd, ag, z0):
    T = r.shape[1]
    nc = T // CHUNK
    tri, strict, incl, bd, bdf = _chunk_masks()
    order = lambda d, c: c + d * (nc - 1 - 2 * c)
    b1 = pl.BlockSpec((N_GROUPS, CHUNK, GROUP), lambda d, c: (0, order(d, c), 0))
    b2 = pl.BlockSpec((1, N_GROUPS, CHUNK, GROUP), lambda d, c: (d, 0, order(d, c), 0))
    per_dir = lambda shape: pl.BlockSpec((1,) + shape, lambda d, c: (d,) + (0,) * len(shape))
    full = lambda shape: pl.BlockSpec(shape, lambda d, c: (0,) * len(shape))
    zspec = per_dir((N_GROUPS, CHUNK, GROUP))
    return pl.pallas_call(
        _rwkv_kernel,
        grid=(2, nc),
        in_specs=[b1, b1, b1, b2, b2, b2, per_dir((CHUNK, CHUNK)), per_dir((CHUNK, GROUP)), per_dir((CHUNK, GROUP)),
                  full((GROUP, GROUP)), full((GROUP, GROUP)), zspec],
        out_specs=[b2, zspec],
        out_shape=[jax.ShapeDtypeStruct((2, N_GROUPS, T, GROUP), F32),
                   jax.ShapeDtypeStruct((2, N_GROUPS, CHUNK, GROUP), F32)],
        scratch_shapes=[pltpu.VMEM((N_GROUPS, CHUNK, GROUP), F32)],
        compiler_params=_cparams(("arbitrary", "arbitrary")),
    )(r, v, kk, logw, kd, ag, tri, strict, incl, bd, bdf, z0)


def _rwkvpost_kernel(y_ref, g_ref, bonus_ref, gw_ref, gb_ref, o_ref):
    for gi in range(N_GROUPS):
        y = y_ref[0, gi] + y_ref[1, gi]
        mu = _seg_sum(y) * (1.0 / RWKV_HEAD_DIM)
        yc = y - mu
        var = _seg_sum(yc * yc) * (1.0 / RWKV_HEAD_DIM)
        sl = slice(gi * GROUP, (gi + 1) * GROUP)
        yn = yc * lax.rsqrt(var + GN_EPS) * gw_ref[:, sl] + gb_ref[:, sl]
        o_ref[:, sl] = ((yn + bonus_ref[gi].astype(F32)) * g_ref[gi].astype(F32)).astype(o_ref.dtype)


def rwkv_post(y, g, bonus, gn_w, gn_b):
    T = y.shape[2]
    tm = min(256, T)
    W = RWKV_WIDTH
    return pl.pallas_call(
        _rwkvpost_kernel,
        grid=(T // tm,),
        in_specs=[pl.BlockSpec((2, N_GROUPS, tm, GROUP), lambda i: (0, 0, i, 0)),
                  pl.BlockSpec((N_GROUPS, tm, GROUP), lambda i: (0, i, 0)),
                  pl.BlockSpec((N_GROUPS, tm, GROUP), lambda i: (0, i, 0)),
                  pl.BlockSpec((1, W), lambda i: (0, 0)), pl.BlockSpec((1, W), lambda i: (0, 0))],
        out_specs=pl.BlockSpec((tm, W), lambda i: (i, 0)),
        out_shape=jax.ShapeDtypeStruct((T, W), BF16),
        compiler_params=_cparams(("parallel",)),
    )(y, g, bonus, gn_w.reshape(1, W), gn_b.reshape(1, W))


def rwkv_mix(z, p, z0):
    r, v, kk, g, bonus, logw, kd, ag = rwkv_prep(z, p)
    y, zf = rwkv_scan(r, v, kk, logw, kd, ag, z0)
    return rwkv_post(y, g, bonus, p['gn_w'], p['gn_b']), zf


def _lora_heads(l, w2, a2, g2):
    W = RWKV_WIDTH
    zw = jnp.zeros((W_LORA, W), F32)
    w2bd = jnp.concatenate([jnp.concatenate([w2[l, 0], zw], axis=1), jnp.concatenate([zw, w2[l, 1]], axis=1)], axis=0)
    a2bd = jnp.concatenate([jnp.concatenate([a2[l, 0], zw], axis=1), jnp.concatenate([zw, a2[l, 1]], axis=1)], axis=0)
    g2p = jnp.pad(g2[l], ((0, LORA_PAD - 256 - G_LORA), (0, 0)))
    return dict(w2bd=w2bd.astype(BF16), a2bd=a2bd.astype(BF16), g2p=g2p.astype(BF16))


def _layer(x, ctx, mod_x, mod_c, p, bias_tab, ctx_out, big, li, has_next):
    Dm = D_MODEL
    l = 0
    sh1, sc1, gt1, sh2, sc2, gt2 = [mod_x[i * Dm:(i + 1) * Dm] for i in range(6)]
    csh1, csc1, cgt1, csh2, csc2, cgt2 = [mod_c[i * Dm:(i + 1) * Dm] for i in range(6)]
    zx, w_out_b = norm_mod_matmul(x, p['norm1_w'], sc1, sh1, p['w_in'], l, side=('plain', big['w_out'], li))
    zc = norm_mod_matmul(ctx, p['norm1_w'], csc1, csh1, p['w_in'], l)
    fx = fourier_mix(zx, p['fourier_w'], p['fourier_b'])
    qx, kx = qk_prep(zx, p['q_norm_w'], p['k_norm_w'], rope=True)
    qc, kc = qk_prep(zc, p['q_norm_w'], p['k_norm_w'], rope=False)
    ax = na_attention(qx, kx, zx, kc, zc, bias_tab)
    z0 = jnp.zeros((2, N_GROUPS, CHUNK, GROUP), F32)
    rc, zf = rwkv_mix(zc, p, z0)
    rx, _ = rwkv_mix(zx, p, zf)
    x, w_ffn_in_b = matmul_residual([fx, ax, rx], w_out_b, l, x, gt1, side=('plain', big['w_ffn_in'], li))
    u, w_ffn_out_b = norm_mod_matmul(x, p['norm2_w'], sc2, sh2, w_ffn_in_b, l, out_dtype=BF16,
                                     side=('plain', big['w_ffn_out'], li))
    gated = [conv_gate(u, p['ffn_conv'])]
    if has_next:
        x, w_in_next = matmul_residual(gated, w_ffn_out_b, l, x, gt2, side=('win', big['w_in'], li + 1))
    else:
        x, w_in_next = matmul_residual(gated, w_ffn_out_b, l, x, gt2), None
    if ctx_out:
        fc = fourier_mix_ctx(zc, p['fourier_w'], p['fourier_b'])
        ac = ctx_attention(qc, kc, zc)
        ctx = matmul_residual([fc, ac, rc], w_out_b, l, ctx, cgt1)
        uc = norm_mod_matmul(ctx, p['norm2_w'], csc2, csh2, w_ffn_in_b, l, out_dtype=BF16)
        ctx = matmul_residual([conv_gate(uc, p['ffn_conv'])], w_ffn_out_b, l, ctx, cgt2)
    return x, ctx, w_in_next


def kernel(x, c, ctx, c_ctx, ada_w, ada_b, norm1_w, norm2_w, w_in, fourier_w, fourier_b, q_norm_w, k_norm_w, rpb,
           rwkv_conv, w0, w2, a0, a2, g2, k_k, k_a, r_k, gn_w, gn_b, w_out, ffn_conv, w_ffn_in, w_ffn_out):
    L = ada_w.shape[0]
    xs = x[0]
    cs = ctx[0]
    mods = ada_mod(jnp.concatenate([c, c_ctx[None, :]], axis=0), ada_w, ada_b)
    bias_tabs = rpb_bias_tables(rpb)
    big = dict(w_in=w_in, w_out=w_out, w_ffn_in=w_ffn_in, w_ffn_out=w_ffn_out)
    w_in_b = cast_layer('win', w_in, 0)
    for l in range(L):
        p = dict(w_in=w_in_b)
        p.update(_lora_heads(l, w2, a2, g2))
        p.update(norm1_w=norm1_w[l], norm2_w=norm2_w[l], fourier_w=fourier_w[l], fourier_b=fourier_b[l],
                 q_norm_w=q_norm_w[l], k_norm_w=k_norm_w[l], rwkv_conv=rwkv_conv[l], w0=w0[l], a0=a0[l],
                 k_k=k_k[l], k_a=k_a[l], r_k=r_k[l], gn_w=gn_w[l], gn_b=gn_b[l], ffn_conv=ffn_conv[l])
        xs, cs, w_in_b = _layer(xs, cs, mods[l, 0], mods[l, 1], p, bias_tabs[l], l < L - 1, big, l, l < L - 1)
    return xs[None]
```

```python
import functools
import math

import jax
import jax.numpy as jnp
import numpy as np
from jax import lax
from jax.experimental import pallas as pl
from jax.experimental.pallas import tpu as pltpu

F32 = jnp.float32
BF16 = jnp.bfloat16

D_MODEL = 4096
GRID_W = 64
FOURIER_WIDTH = 1024
FOURIER_HEADS = 4
FOURIER_DIM = 256
NA_WIDTH = 1536
NA_HEAD_DIM = 128
NA_HEADS = 12
NA_KH = 8
NA_KW = 16
RWKV_WIDTH = 1536
RWKV_HEAD_DIM = 64
W_LORA = 64
A_LORA = 64
G_LORA = 224
D_FF = 5120
ROPE_THETA = 10000.0
NORM_EPS = 1e-6
GN_EPS = 64e-5

COL_Q = 0
COL_K = NA_WIDTH
COL_V = 2 * NA_WIDTH
COL_R = 3 * NA_WIDTH
COL_F = 6 * NA_WIDTH
COL_L = COL_F + FOURIER_WIDTH
LORA_PAD = 512
IN_COLS_PAD = COL_L + LORA_PAD

CHUNK = 64
GROUP = 256
N_GROUPS = RWKV_WIDTH // GROUP
FFT_B = 128

LANE = 128
VMEM_LIMIT = 48 * 1024 * 1024


def _cparams(sem):
    return pltpu.CompilerParams(dimension_semantics=sem, vmem_limit_bytes=VMEM_LIMIT)


def _dot(a, b):
    return jnp.dot(a, b, preferred_element_type=F32)


def _dot_nt(a, b):
    return lax.dot_general(a, b, (((1,), (1,)), ((), ())), preferred_element_type=F32)


def _dot_tn(a, b):
    return lax.dot_general(a, b, (((0,), (0,)), ((), ())), preferred_element_type=F32)


def _split(x):
    hi = x.astype(BF16)
    lo = (x - hi.astype(F32)).astype(BF16)
    return hi, lo


def _ada_kernel(s_ref, w_ref, b_ref, o_ref, acc_ref):
    k = pl.program_id(2)
    tk, tn = w_ref.shape[1], w_ref.shape[2]
    rep = tn // LANE

    @pl.when(k == 0)
    def _():
        acc_ref[...] = jnp.zeros_like(acc_ref)

    def body(i, carry):
        a0, a1 = carry
        r = pl.multiple_of(i * 8, 8)
        rs = pl.multiple_of(k * tk + i * 8, 8)
        w = w_ref[0, pl.ds(r, 8), :]
        s0 = s_ref[0, pl.ds(rs, 8), :]
        s1 = s_ref[1, pl.ds(rs, 8), :]
        s0 = s0 * jax.nn.sigmoid(s0)
        s1 = s1 * jax.nn.sigmoid(s1)
        a0 = a0 + w * jnp.concatenate([s0] * rep, axis=1)
        a1 = a1 + w * jnp.concatenate([s1] * rep, axis=1)
        return a0, a1

    a0, a1 = lax.fori_loop(0, tk // 8, body, (acc_ref[0], acc_ref[1]), unroll=4)
    acc_ref[0] = a0
    acc_ref[1] = a1

    @pl.when(k == pl.num_programs(2) - 1)
    def _():
        o_ref[0, 0:1, :] = jnp.sum(a0, axis=0, keepdims=True) + b_ref[0]
        o_ref[0, 1:2, :] = jnp.sum(a1, axis=0, keepdims=True) + b_ref[0]


def ada_mod(cc, ada_w, ada_b):
    L, K, N = ada_w.shape
    tk, tn = 2048, 1024
    s_b = jnp.broadcast_to(cc[:, :, None], (2, K, LANE))
    return pl.pallas_call(
        _ada_kernel,
        grid=(L, N // tn, K // tk),
        in_specs=[
            pl.BlockSpec((2, K, LANE), lambda l, j, k: (0, 0, 0)),
            pl.BlockSpec((1, tk, tn), lambda l, j, k: (l, k, j)),
            pl.BlockSpec((1, 1, tn), lambda l, j, k: (l, 0, j)),
        ],
        out_specs=pl.BlockSpec((1, 2, tn), lambda l, j, k: (l, 0, j)),
        out_shape=jax.ShapeDtypeStruct((L, 2, N), F32),
        scratch_shapes=[pltpu.VMEM((2, 8, tn), F32)],
        compiler_params=_cparams(("parallel", "parallel", "arbitrary")),
    )(s_b, ada_w, ada_b.reshape(L, 1, N))


def _normmod_kernel(x_ref, nw_ref, sc_ref, sh_ref, o_ref):
    x = x_ref[...]
    ms = jnp.mean(x * x, axis=-1, keepdims=True)
    y = x * lax.rsqrt(ms + NORM_EPS) * nw_ref[...]
    o_ref[...] = (y * (1.0 + sc_ref[...]) + sh_ref[...]).astype(o_ref.dtype)


LORA_COLS = 2 * W_LORA + 2 * A_LORA + G_LORA
W_IN_BLOCKS = IN_COLS_PAD // 512
W_IN_FRONT = FOURIER_WIDTH // 512
W_IN_MID = W_IN_BLOCKS - 1 - W_IN_FRONT
CAST_ROWS, CAST_COLS = 256, 1024
WIN_TK = 1024


def _w_in_src_block(j):
    return jnp.where(j < W_IN_MID, j + W_IN_FRONT, jnp.where(j < W_IN_BLOCKS - 1, j - W_IN_MID, W_IN_BLOCKS - 1))


def _cast_block(s_ref, so_ref, kind, block):
    if kind == 'plain':
        so_ref[...] = s_ref[...].astype(BF16)
    else:
        row = lax.broadcasted_iota(jnp.int32, s_ref.shape[1:], 0)
        keep = (block % W_IN_BLOCKS < W_IN_BLOCKS - 1) | (row < LORA_COLS)
        so_ref[0] = jnp.where(keep, s_ref[0], 0.0).T.astype(BF16)


def _cast_blocks(kind, src, rows):
    K = src.shape[1]
    return (K // rows) * (src.shape[2] // CAST_COLS if kind == 'plain' else W_IN_BLOCKS)


def _cast_specs(kind, src, l, block_of, rows):
    nblocks = _cast_blocks(kind, src, rows)
    if kind == 'plain':
        _, K, N = src.shape
        ncb = N // CAST_COLS
        blk = (1, rows, CAST_COLS)
        in_spec = pl.BlockSpec(blk, lambda *g: (l, block_of(*g) // ncb, block_of(*g) % ncb))
        out_spec = pl.BlockSpec(blk, lambda *g: (0, block_of(*g) // ncb, block_of(*g) % ncb))
        return src, in_spec, out_spec, jax.ShapeDtypeStruct((1, K, N), BF16), nblocks
    _, K, _ = src.shape
    in_spec = pl.BlockSpec((1, 512, rows), lambda *g: (l, _w_in_src_block(block_of(*g) % W_IN_BLOCKS),
                                                       block_of(*g) // W_IN_BLOCKS))
    out_spec = pl.BlockSpec((1, rows, 512), lambda *g: (0, block_of(*g) // W_IN_BLOCKS, block_of(*g) % W_IN_BLOCKS))
    return jnp.swapaxes(src, 1, 2), in_spec, out_spec, jax.ShapeDtypeStruct((1, K, IN_COLS_PAD), BF16), nblocks


def _castonly_kernel(s_ref, so_ref, *, kind):
    _cast_block(s_ref, so_ref, kind, pl.program_id(0))


def cast_layer(kind, w, l):
    src, in_spec, out_spec, out_shape, nblocks = _cast_specs(kind, w, l, lambda b: b, WIN_TK)
    return pl.pallas_call(
        functools.partial(_castonly_kernel, kind=kind),
        grid=(nblocks,),
        in_specs=[in_spec],
        out_specs=out_spec,
        out_shape=out_shape,
        compiler_params=_cparams(("parallel",)),
    )(src)


def _side_cast_specs(side, grid):
    kind, w, l = side
    steps = grid[0] * grid[1]
    K = w.shape[1]
    start = CAST_ROWS if kind == 'plain' else WIN_TK
    rows = next((r for r in (start * 2 ** e for e in range(8)) if r <= K and K % r == 0
                 and _cast_blocks(kind, w, r) <= steps), None)
    assert rows is not None, "the matmul grid has too few steps to cast every block of the weight"
    nb = _cast_blocks(kind, w, rows)
    return _cast_specs(kind, w, l, lambda i, j: jnp.minimum(i * grid[1] + j, nb - 1), rows) + (kind,)


def _mm_kernel(a_ref, w_ref, *rest, side_kind, side_blocks):
    if side_kind is None:
        (o_ref,) = rest
    else:
        s_ref, o_ref, so_ref = rest
        step = pl.program_id(0) * pl.num_programs(1) + pl.program_id(1)
        _cast_block(s_ref, so_ref, side_kind, jnp.minimum(step, side_blocks - 1))
    o_ref[...] = _dot(a_ref[...], w_ref[0]).astype(o_ref.dtype)


def norm_mod_matmul(x, nw, sc, sh, w, l, out_dtype=F32, side=None):
    M, K = x.shape
    N = w.shape[2]
    tr = min(512, M)
    vec = pl.BlockSpec((1, K), lambda i: (0, 0))
    h = pl.pallas_call(
        _normmod_kernel,
        grid=(M // tr,),
        in_specs=[pl.BlockSpec((tr, K), lambda i: (i, 0)), vec, vec, vec],
        out_specs=pl.BlockSpec((tr, K), lambda i: (i, 0)),
        out_shape=jax.ShapeDtypeStruct((M, K), BF16),
        compiler_params=_cparams(("parallel",)),
    )(x, nw.reshape(1, K), sc.reshape(1, K), sh.reshape(1, K))
    tm = min(1024, M)
    tn = next(t for t in (1024, 768, 512) if N % t == 0)
    grid = (M // tm, N // tn)
    args = [h, w]
    in_specs = [
        pl.BlockSpec((tm, K), lambda i, j: (i, 0)),
        pl.BlockSpec((1, K, tn), lambda i, j: (l, 0, j)),
    ]
    out_specs = [pl.BlockSpec((tm, tn), lambda i, j: (i, j))]
    out_shape = [jax.ShapeDtypeStruct((M, N), out_dtype)]
    side_kind, side_blocks = None, 0
    if side is not None:
        src, s_in, s_out, s_shape, side_blocks, side_kind = _side_cast_specs(side, grid)
        args.append(src)
        in_specs.append(s_in)
        out_specs.append(s_out)
        out_shape.append(s_shape)
    outs = pl.pallas_call(
        functools.partial(_mm_kernel, side_kind=side_kind, side_blocks=side_blocks),
        grid=grid,
        in_specs=in_specs,
        out_specs=out_specs,
        out_shape=out_shape,
        compiler_params=_cparams(("arbitrary", "arbitrary") if side is not None else ("parallel", "arbitrary")),
    )(*args)
    return outs[0] if side is None else tuple(outs)


def _mmres_kernel(*refs, ksplits, side_kind, side_blocks):
    n = len(ksplits)
    a_refs = refs[:n]
    if side_kind is None:
        w_ref, x_ref, g_ref, o_ref = refs[n:]
    else:
        w_ref, x_ref, g_ref, s_ref, o_ref, so_ref = refs[n:]
        step = pl.program_id(0) * pl.num_programs(1) + pl.program_id(1)
        _cast_block(s_ref, so_ref, side_kind, jnp.minimum(step, side_blocks - 1))
    acc = None
    off = 0
    for a_ref, kp in zip(a_refs, ksplits):
        part = _dot(a_ref[...].astype(BF16), w_ref[0, off:off + kp, :])
        acc = part if acc is None else acc + part
        off += kp
    o_ref[...] = x_ref[...] + g_ref[...] * acc


def matmul_residual(parts, w, l, x, gate, side=None):
    M, N = x.shape
    K = w.shape[1]
    ksplits = tuple(p.shape[1] for p in parts)
    assert sum(ksplits) == K
    tm = min(1024, M)
    tn = 512
    grid = (M // tm, N // tn)
    args = list(parts) + [w, x, gate.reshape(1, N)]
    in_specs = [pl.BlockSpec((tm, kp), lambda i, j: (i, 0)) for kp in ksplits]
    in_specs += [
        pl.BlockSpec((1, K, tn), lambda i, j: (l, 0, j)),
        pl.BlockSpec((tm, tn), lambda i, j: (i, j)),
        pl.BlockSpec((1, tn), lambda i, j: (0, j)),
    ]
    out_specs = [pl.BlockSpec((tm, tn), lambda i, j: (i, j))]
    out_shape = [jax.ShapeDtypeStruct((M, N), F32)]
    side_kind, side_blocks = None, 0
    if side is not None:
        src, s_in, s_out, s_shape, side_blocks, side_kind = _side_cast_specs(side, grid)
        args.append(src)
        in_specs.append(s_in)
        out_specs.append(s_out)
        out_shape.append(s_shape)
    outs = pl.pallas_call(
        functools.partial(_mmres_kernel, ksplits=ksplits, side_kind=side_kind, side_blocks=side_blocks),
        grid=grid,
        in_specs=in_specs,
        out_specs=out_specs,
        out_shape=out_shape,
        compiler_params=_cparams(("arbitrary", "arbitrary") if side is not None else ("parallel", "arbitrary")),
    )(*args)
    return outs[0] if side is None else tuple(outs)


def _conv3(main, prev_row, next_row, w):
    tm = main.shape[0]
    row = lax.broadcasted_iota(jnp.int32, main.shape, 0)
    dn = jnp.where(row == 0, prev_row, pltpu.roll(main, 1, axis=0))
    up = jnp.where(row == tm - 1, next_row, pltpu.roll(main, tm - 1, axis=0))
    return dn * w[0:1, :] + main * w[1:2, :] + up * w[2:3, :]


def _halo_rows(prev_ref, next_ref, i, n_i):
    prev = prev_ref[...].astype(F32)
    nxt = next_ref[...].astype(F32)
    hr = prev.shape[0]
    prev_row = jnp.where(i == 0, 0.0, prev[hr - 1:hr, :])
    next_row = jnp.where(i == n_i - 1, 0.0, nxt[0:1, :])
    return prev_row, next_row


def _halo_specs(tm, tn, n_rows, col_fn, hr=8):
    rb = tm // hr
    last = n_rows // hr - 1
    return [
        pl.BlockSpec((tm, tn), lambda i, j: (i, col_fn(j))),
        pl.BlockSpec((hr, tn), lambda i, j: (jnp.maximum(i * rb - 1, 0), col_fn(j))),
        pl.BlockSpec((hr, tn), lambda i, j: (jnp.minimum((i + 1) * rb, last), col_fn(j))),
    ]


def _convgate_kernel(a_ref, ap_ref, an_ref, b_ref, bp_ref, bn_ref, wa_ref, wb_ref, o_ref):
    i = pl.program_id(0)
    n_i = pl.num_programs(0)
    pa, na = _halo_rows(ap_ref, an_ref, i, n_i)
    pb, nb = _halo_rows(bp_ref, bn_ref, i, n_i)
    a = _conv3(a_ref[...].astype(F32), pa, na, wa_ref[...])
    b = _conv3(b_ref[...].astype(F32), pb, nb, wb_ref[...])
    o_ref[...] = (a * jax.nn.sigmoid(a) * b).astype(o_ref.dtype)


def conv_gate(u, conv_w):
    M, N2 = u.shape
    F = N2 // 2
    tm = min(512, M)
    tn = 1024
    nb = F // tn
    specs = _halo_specs(tm, tn, M, lambda j: j, 16) + _halo_specs(tm, tn, M, lambda j: j + nb, 16)
    specs += [pl.BlockSpec((3, tn), lambda i, j: (0, j)), pl.BlockSpec((3, tn), lambda i, j: (0, j + nb))]
    return pl.pallas_call(
        _convgate_kernel,
        grid=(M // tm, nb),
        in_specs=specs,
        out_specs=pl.BlockSpec((tm, tn), lambda i, j: (i, j)),
        out_shape=jax.ShapeDtypeStruct((M, F), BF16),
        compiler_params=_cparams(("parallel", "parallel")),
    )(u, u, u, u, u, u, conv_w, conv_w)


def _dft_tables(T):
    A, B = T // FFT_B, FFT_B
    ka = np.arange(A)
    a = np.arange(A)
    b = np.arange(B)
    n = (ka[None, :, None] * (B * a[None, None, :] + b[:, None, None])) % T
    ang = n * (2.0 * np.pi / T)
    sa = 1.0 / math.sqrt(A)
    m1 = np.concatenate([np.cos(ang) * sa, -np.sin(ang) * sa], axis=1).astype(np.float32)
    kb = np.arange(B)
    ang2 = ((kb[:, None] * b[None, :]) % B) * (2.0 * np.pi / B)
    sb = 1.0 / math.sqrt(B)
    m2 = np.concatenate([np.cos(ang2) * sb, np.sin(ang2) * sb], axis=1).astype(np.float32)
    return jnp.asarray(m1, BF16), jnp.asarray(m2, BF16)


def _channel_dft():
    c = np.arange(FOURIER_DIM)
    ang = 2.0 * np.pi * ((c[:, None] * c[None, :]) % FOURIER_DIM) / FOURIER_DIM
    s = 1.0 / math.sqrt(FOURIER_DIM)
    return np.cos(ang) * s, np.sin(ang) * s


FFT_SUB = 8


def _f1_kernel(x_ref, m_ref, ch_ref, pm_ref, re_ref, im_ref):
    A = x_ref.shape[0]
    ch = ch_ref[...]
    x_all = x_ref[...].reshape(A * FFT_SUB, x_ref.shape[2]).astype(BF16)
    xp = _dot(pm_ref[...], x_all).astype(BF16)
    for j in range(FFT_SUB):
        y = _dot(m_ref[j], xp[j * A:(j + 1) * A])
        for h in range(2):
            sl = slice(h * FOURIER_DIM, (h + 1) * FOURIER_DIM)
            lhs = jnp.concatenate([y[:A, sl], y[A:, sl]], axis=1).astype(BF16)
            yp = _dot(lhs, ch)
            re_ref[j, :, sl] = yp[:, :FOURIER_DIM]
            im_ref[j, :, sl] = yp[:, FOURIER_DIM:]


def _f2_kernel(re_ref, im_ref, m_ref, w_ref, b_ref, o_ref):
    for j in range(FFT_SUB):
        rhs = jnp.concatenate([re_ref[:, j, :], im_ref[:, j, :]], axis=0).astype(BF16)
        spec = _dot(m_ref[...], rhs)
        outs = []
        for h in range(FOURIER_HEADS):
            sl = slice(h * FOURIER_DIM, (h + 1) * FOURIER_DIM)
            outs.append(_dot(spec[:, sl].astype(BF16), w_ref[h].astype(BF16)) + b_ref[h])
        o_ref[:, j, :] = jnp.concatenate(outs, axis=1)


def fourier_mix(z, fw, fb):
    T, NC = z.shape
    A, B = T // FFT_B, FFT_B
    m1, m2 = _dft_tables(T)
    cc, ss = _channel_dft()
    ch = jnp.asarray(np.block([[cc, -ss], [ss, cc]]), BF16)
    cb0 = COL_F // 512
    n = A * FFT_SUB
    dst = np.arange(n)
    pm = jnp.asarray((((dst % A) * FFT_SUB + dst // A)[:, None] == np.arange(n)[None, :]).astype(np.float32), BF16)
    yre, yim = pl.pallas_call(
        _f1_kernel,
        grid=(B // FFT_SUB, 2),
        in_specs=[
            pl.BlockSpec((A, FFT_SUB, 512), lambda b, c: (0, b, cb0 + c)),
            pl.BlockSpec((FFT_SUB, 2 * A, A), lambda b, c: (b, 0, 0)),
            pl.BlockSpec((512, 512), lambda b, c: (0, 0)),
            pl.BlockSpec((n, n), lambda b, c: (0, 0)),
        ],
        out_specs=[pl.BlockSpec((FFT_SUB, A, 512), lambda b, c: (b, 0, c))] * 2,
        out_shape=[jax.ShapeDtypeStruct((B, A, FOURIER_WIDTH), F32)] * 2,
        compiler_params=_cparams(("parallel", "parallel")),
    )(z.reshape(A, B, NC), m1, ch, pm)
    blk = pl.BlockSpec((B, FFT_SUB, FOURIER_WIDTH), lambda i: (0, i, 0))
    out = pl.pallas_call(
        _f2_kernel,
        grid=(A // FFT_SUB,),
        in_specs=[
            blk, blk,
            pl.BlockSpec((B, 2 * B), lambda i: (0, 0)),
            pl.BlockSpec((FOURIER_HEADS, FOURIER_DIM, FOURIER_DIM), lambda i: (0, 0, 0)),
            pl.BlockSpec((FOURIER_HEADS, 1, FOURIER_DIM), lambda i: (0, 0, 0)),
        ],
        out_specs=blk,
        out_shape=jax.ShapeDtypeStruct((B, A, FOURIER_WIDTH), F32),
        compiler_params=_cparams(("parallel",)),
    )(yre, yim, m2, fw, fb.reshape(FOURIER_HEADS, 1, FOURIER_DIM))
    return out.reshape(T, FOURIER_WIDTH)


def _fctx_kernel(f_ref, cs_ref, ts_ref, w_ref, b_ref, o_ref):
    for h in range(FOURIER_HEADS):
        sl = slice(h * FOURIER_DIM, (h + 1) * FOURIER_DIM)
        g = _dot(f_ref[:, sl].astype(BF16), cs_ref[...])
        gg = jnp.concatenate([g[:, :FOURIER_DIM], g[:, FOURIER_DIM:]], axis=0).astype(BF16)
        spec = _dot(ts_ref[...], gg)
        out = _dot(spec.astype(BF16), w_ref[h].astype(BF16)) + b_ref[h]
        o_ref[:, sl] = out.astype(o_ref.dtype)


def fourier_mix_ctx(zc, fw, fb):
    T = zc.shape[0]
    cc, ss = _channel_dft()
    cs = jnp.asarray(np.concatenate([cc, ss], axis=1), BF16)
    t = np.arange(T)
    ang = 2.0 * np.pi * ((t[:, None] * t[None, :]) % T) / T
    st = 1.0 / math.sqrt(T)
    ts = jnp.asarray(np.concatenate([np.cos(ang) * st, -np.sin(ang) * st], axis=1), BF16)
    return pl.pallas_call(
        _fctx_kernel,
        grid=(1,),
        in_specs=[
            pl.BlockSpec((T, FOURIER_WIDTH), lambda i: (0, COL_F // FOURIER_WIDTH)),
            pl.BlockSpec((FOURIER_DIM, 2 * FOURIER_DIM), lambda i: (0, 0)),
            pl.BlockSpec((T, 2 * T), lambda i: (0, 0)),
            pl.BlockSpec((FOURIER_HEADS, FOURIER_DIM, FOURIER_DIM), lambda i: (0, 0, 0)),
            pl.BlockSpec((FOURIER_HEADS, 1, FOURIER_DIM), lambda i: (0, 0, 0)),
        ],
        out_specs=pl.BlockSpec((T, FOURIER_WIDTH), lambda i: (0, 0)),
        out_shape=jax.ShapeDtypeStruct((T, FOURIER_WIDTH), BF16),
        compiler_params=_cparams(("arbitrary",)),
    )(zc, cs, ts, fw, fb.reshape(FOURIER_HEADS, 1, FOURIER_DIM))


def _rope_tables(T):
    nf = NA_HEAD_DIM // 4
    t = np.arange(T)
    inv = 1.0 / (ROPE_THETA ** (np.arange(nf) / nf))
    lane = np.arange(NA_HEAD_DIM)
    pos = np.where(lane[None, :] < NA_HEAD_DIM // 2, (t // GRID_W)[:, None], (t % GRID_W)[:, None])
    ang = pos * inv[lane % nf][None, :]
    sign = np.where((lane % (2 * nf)) < nf, -1.0, 1.0)[None, :]
    return jnp.asarray(np.cos(ang), F32), jnp.asarray(np.sin(ang) * sign, F32)


def _head_norm_rope(x, w, cos, sin, perm):
    ms = jnp.mean(x * x, axis=-1, keepdims=True)
    y = x * lax.rsqrt(ms + NORM_EPS) * w
    if cos is None:
        return y
    return y * cos + _dot(y.astype(BF16), perm) * sin


def _qkprep_kernel(q_ref, k_ref, qw_ref, kw_ref, cos_ref, sin_ref, perm_ref, qo_ref, ko_ref, *, rope):
    cos = cos_ref[...] if rope else None
    sin = sin_ref[...] if rope else None
    perm = perm_ref[...]
    qw = qw_ref[...] * (NA_HEAD_DIM ** -0.5)
    kw = kw_ref[...]
    for h in range(4):
        sl = slice(h * LANE, (h + 1) * LANE)
        qo_ref[:, sl] = _head_norm_rope(q_ref[:, sl], qw, cos, sin, perm).astype(qo_ref.dtype)
        ko_ref[:, sl] = _head_norm_rope(k_ref[:, sl], kw, cos, sin, perm).astype(ko_ref.dtype)


def qk_prep(z, qw, kw, rope):
    T = z.shape[0]
    tm = min(512, T)
    if rope:
        cos, sin = _rope_tables(T)
    else:
        cos = sin = jnp.zeros((T, LANE), F32)
    nq = NA_WIDTH // 512
    lane = np.arange(LANE)
    src = np.where(lane % 64 < 32, lane + 32, lane - 32)
    perm = jnp.asarray((lane[:, None] == src[None, :]).astype(np.float32), BF16)
    return pl.pallas_call(
        functools.partial(_qkprep_kernel, rope=rope),
        grid=(T // tm, nq),
        in_specs=[
            pl.BlockSpec((tm, 512), lambda i, j: (i, COL_Q // 512 + j)),
            pl.BlockSpec((tm, 512), lambda i, j: (i, COL_K // 512 + j)),
            pl.BlockSpec((1, LANE), lambda i, j: (0, 0)),
            pl.BlockSpec((1, LANE), lambda i, j: (0, 0)),
            pl.BlockSpec((tm, LANE), lambda i, j: (i, 0)),
            pl.BlockSpec((tm, LANE), lambda i, j: (i, 0)),
            pl.BlockSpec((LANE, LANE), lambda i, j: (0, 0)),
        ],
        out_specs=[pl.BlockSpec((tm, 512), lambda i, j: (i, j))] * 2,
        out_shape=[jax.ShapeDtypeStruct((T, NA_WIDTH), BF16)] * 2,
        compiler_params=_cparams(("parallel", "parallel")),
    )(z, z, qw.reshape(1, LANE), kw.reshape(1, LANE), cos, sin, perm)


def _toeplitz_kernel(r_ref, e_ref, o_ref):
    r = r_ref[0]
    acc = jnp.zeros(o_ref.shape[1:], F32)
    for d in range(2 * NA_KW - 1):
        acc = acc + r[:, d:d + 1] * e_ref[d:d + 1, :]
    o_ref[0] = acc


def rpb_bias_tables(rpb):
    L, H = rpb.shape[0], rpb.shape[1]
    ndr, ndc = 2 * NA_KH - 1, 2 * NA_KW - 1
    q = np.arange(GRID_W)
    e = np.zeros((32, GRID_W, GRID_W), np.float32)
    for d in range(ndc):
        e[d] = (q[None, :] - q[:, None] + (NA_KW - 1)) == d
    e = jnp.asarray(e.reshape(32, GRID_W * GRID_W))
    rp = jnp.pad(rpb.reshape(L * H, ndr, ndc), ((0, 0), (0, 16 - ndr), (0, 32 - ndc)))
    toep = pl.pallas_call(
        _toeplitz_kernel,
        grid=(L * H,),
        in_specs=[pl.BlockSpec((1, 16, 32), lambda i: (i, 0, 0)),
                  pl.BlockSpec((32, GRID_W * GRID_W), lambda i: (0, 0))],
        out_specs=pl.BlockSpec((1, 16, GRID_W * GRID_W), lambda i: (i, 0, 0)),
        out_shape=jax.ShapeDtypeStruct((L * H, 16, GRID_W * GRID_W), F32),
        compiler_params=_cparams(("parallel",)),
    )(rp, e)
    toep = toep.reshape(L, H, 16, GRID_W, GRID_W)
    col_start = np.clip(q - NA_KW // 2, 0, GRID_W - NA_KW)
    in_win = (q[None, :] >= col_start[:, None]) & (q[None, :] < col_start[:, None] + NA_KW)
    mask = jnp.asarray(np.where(in_win, 0.0, -1e30).astype(np.float32))
    dr = np.arange(NA_KH)[None, :] - np.arange(NA_KH)[:, None] + (NA_KH - 1)
    band = toep[:, :, dr] + mask
    return jnp.transpose(band, (0, 1, 2, 4, 3, 5)).reshape(L, H, NA_KH, GRID_W, NA_KH * GRID_W)


NA_HEADS_PER_STEP = 4


def _na_kernel(q_ref, kp_ref, kc_ref, kn_ref, vp_ref, vc_ref, vn_ref, kx_ref, vx_ref, bt_ref, o_ref,
               kbuf, vbuf, *, nrows):
    m = pl.program_id(1)
    blk = NA_KH * GRID_W
    kbuf[0:blk] = kp_ref[...]
    kbuf[blk:2 * blk] = kc_ref[...]
    kbuf[2 * blk:3 * blk] = kn_ref[...]
    vbuf[0:blk] = vp_ref[...].astype(BF16)
    vbuf[blk:2 * blk] = vc_ref[...].astype(BF16)
    vbuf[2 * blk:3 * blk] = vn_ref[...].astype(BF16)
    kctx = kx_ref[...]
    vctx = vx_ref[...].astype(BF16)
    work, scores = [], []
    for hh in range(NA_HEADS_PER_STEP):
        hs = slice(hh * LANE, (hh + 1) * LANE)
        for j in range(NA_KH):
            r = m * NA_KH + j
            rs = jnp.clip(r - NA_KH // 2, 0, nrows - NA_KH)
            start = pl.multiple_of((rs - (m - 1) * NA_KH) * GRID_W, GRID_W)
            q = q_ref[j * GRID_W:(j + 1) * GRID_W, hs]
            s = _dot_nt(q, kbuf[pl.ds(start, blk), hs]) + bt_ref[hh, r - rs]
            work.append((hs, j, start))
            scores.append((s, _dot_nt(q, kctx[:, hs])))
    probs = []
    for s, sc in scores:
        mx = jnp.maximum(jnp.max(s, axis=-1, keepdims=True), jnp.max(sc, axis=-1, keepdims=True))
        p = jnp.exp(s - mx)
        pc = jnp.exp(sc - mx)
        den = jnp.sum(p, axis=-1, keepdims=True) + jnp.sum(pc, axis=-1, keepdims=True)
        probs.append((p.astype(BF16), pc.astype(BF16), den))
    for (hs, j, start), (p, pc, den) in zip(work, probs):
        acc = _dot(p, vbuf[pl.ds(start, blk), hs]) + _dot(pc, vctx[:, hs])
        o_ref[j * GRID_W:(j + 1) * GRID_W, hs] = (acc / den).astype(o_ref.dtype)


def na_attention(qn, kn, z, kcn, zc, bias_tab):
    T = qn.shape[0]
    C = kcn.shape[0]
    nrows = T // GRID_W
    blk = NA_KH * GRID_W
    nblk = T // blk
    hw = NA_HEADS_PER_STEP * LANE
    vcol = COL_V // hw
    prev = lambda h, m: (jnp.maximum(m - 1, 0), h)
    cur = lambda h, m: (m, h)
    nxt = lambda h, m: (jnp.minimum(m + 1, nblk - 1), h)
    vprev = lambda h, m: (jnp.maximum(m - 1, 0), vcol + h)
    vcur = lambda h, m: (m, vcol + h)
    vnxt = lambda h, m: (jnp.minimum(m + 1, nblk - 1), vcol + h)
    return pl.pallas_call(
        functools.partial(_na_kernel, nrows=nrows),
        grid=(NA_HEADS // NA_HEADS_PER_STEP, nblk),
        in_specs=[
            pl.BlockSpec((blk, hw), cur),
            pl.BlockSpec((blk, hw), prev), pl.BlockSpec((blk, hw), cur), pl.BlockSpec((blk, hw), nxt),
            pl.BlockSpec((blk, hw), vprev), pl.BlockSpec((blk, hw), vcur), pl.BlockSpec((blk, hw), vnxt),
            pl.BlockSpec((C, hw), lambda h, m: (0, h)),
            pl.BlockSpec((C, hw), lambda h, m: (0, vcol + h)),
            pl.BlockSpec((NA_HEADS_PER_STEP, NA_KH, GRID_W, blk), lambda h, m: (h, 0, 0, 0)),
        ],
        out_specs=pl.BlockSpec((blk, hw), cur),
        out_shape=jax.ShapeDtypeStruct((T, NA_WIDTH), BF16),
        scratch_shapes=[pltpu.VMEM((3 * blk, hw), BF16), pltpu.VMEM((3 * blk, hw), BF16)],
        compiler_params=_cparams(("parallel", "parallel")),
    )(qn, kn, kn, kn, z, z, z, kcn, zc, bias_tab)


def _ctxattn_kernel(q_ref, k_ref, v_ref, o_ref):
    s = _dot_nt(q_ref[...], k_ref[...])
    p = jnp.exp(s - jnp.max(s, axis=-1, keepdims=True))
    den = jnp.sum(p, axis=-1, keepdims=True)
    o_ref[...] = (_dot(p.astype(BF16), v_ref[...].astype(BF16)) / den).astype(o_ref.dtype)


def ctx_attention(qcn, kcn, zc):
    C = qcn.shape[0]
    vcol = COL_V // LANE
    return pl.pallas_call(
        _ctxattn_kernel,
        grid=(NA_HEADS,),
        in_specs=[pl.BlockSpec((C, LANE), lambda h: (0, h)), pl.BlockSpec((C, LANE), lambda h: (0, h)),
                  pl.BlockSpec((C, LANE), lambda h: (0, vcol + h))],
        out_specs=pl.BlockSpec((C, LANE), lambda h: (0, h)),
        out_shape=jax.ShapeDtypeStruct((C, NA_WIDTH), BF16),
        compiler_params=_cparams(("parallel",)),
    )(qcn, kcn, zc)


def _seg_sum64(x):
    lane = lax.broadcasted_iota(jnp.int32, x.shape, 1)
    low = lane < RWKV_HEAD_DIM
    s_lo = jnp.sum(jnp.where(low, x, 0.0), axis=-1, keepdims=True)
    s_hi = jnp.sum(jnp.where(low, 0.0, x), axis=-1, keepdims=True)
    return jnp.where(low, s_lo, s_hi)


def _seg_sum(x):
    return jnp.concatenate([_seg_sum64(x[:, i * LANE:(i + 1) * LANE]) for i in range(x.shape[1] // LANE)], axis=1)


def _rwkvprep_kernel(r_ref, rp_ref, rn_ref, k_ref, kp_ref, kn_ref, v_ref, vp_ref, vn_ref, lo_ref, cw_ref,
                     w2_ref, a2_ref, g2_ref, w0_ref, a0_ref, kk_ref, ka_ref, rk_ref,
                     r_o, v_o, kk_o, g_o, bonus_o, logw_o, kd_o, ag_o):
    i = pl.program_id(0)
    n_i = pl.num_programs(0)
    W = RWKV_WIDTH
    cw = cw_ref[...]
    r = _conv3(r_ref[...], *_halo_rows(rp_ref, rn_ref, i, n_i), cw[:, 0:W])
    k = _conv3(k_ref[...], *_halo_rows(kp_ref, kn_ref, i, n_i), cw[:, W:2 * W])
    v = _conv3(v_ref[...], *_halo_rows(vp_ref, vn_ref, i, n_i), cw[:, 2 * W:3 * W])
    lora = lo_ref[...]
    wl = _dot(jnp.tanh(lora[:, 0:2 * W_LORA]).astype(BF16), w2_ref[...])
    al = _dot(lora[:, 2 * W_LORA:2 * (W_LORA + A_LORA)].astype(BF16), a2_ref[...])
    g = _dot(jax.nn.sigmoid(lora[:, 2 * (W_LORA + A_LORA):]).astype(BF16), g2_ref[...])
    kkr = k * kk_ref[...]
    kk = kkr * lax.rsqrt(jnp.maximum(_seg_sum(kkr * kkr), 1e-24))
    kds = []
    for d in range(2):
        logw = -math.exp(-0.5) * jax.nn.sigmoid(w0_ref[d:d + 1, :] + wl[:, d * W:(d + 1) * W])
        a = jax.nn.sigmoid(a0_ref[d:d + 1, :] + al[:, d * W:(d + 1) * W])
        kd = k * (1.0 + (a - 1.0) * ka_ref[...])
        kds.append(kd)
        for gi in range(N_GROUPS):
            sl = slice(gi * GROUP, (gi + 1) * GROUP)
            logw_o[d, gi] = logw[:, sl]
            kd_o[d, gi] = kd[:, sl].astype(kd_o.dtype)
            ag_o[d, gi] = a[:, sl].astype(ag_o.dtype)
    bonus = _seg_sum(r * (kds[0] + kds[1]) * rk_ref[...]) * v
    for gi in range(N_GROUPS):
        sl = slice(gi * GROUP, (gi + 1) * GROUP)
        r_o[gi] = r[:, sl].astype(r_o.dtype)
        v_o[gi] = v[:, sl].astype(v_o.dtype)
        kk_o[gi] = kk[:, sl].astype(kk_o.dtype)
        g_o[gi] = g[:, sl].astype(g_o.dtype)
        bonus_o[gi] = bonus[:, sl].astype(bonus_o.dtype)


def rwkv_prep(z, p):
    T = z.shape[0]
    tm = min(128, T)
    W = RWKV_WIDTH
    cb = COL_R // W
    specs = []
    for c in range(3):
        specs += _halo_specs(tm, W, T, lambda j, c=c: cb + c)
    specs = [pl.BlockSpec(s.block_shape, lambda i, f=s.index_map: f(i, 0)) for s in specs]
    full = lambda shape: pl.BlockSpec(shape, lambda i: (0,) * len(shape))
    specs += [
        pl.BlockSpec((tm, LORA_PAD), lambda i: (i, COL_L // LORA_PAD)),
        full((3, 3 * W)), full((2 * W_LORA, 2 * W)), full((2 * A_LORA, 2 * W)), full((LORA_PAD - 256, W)),
        full((2, W)), full((2, W)), full((1, W)), full((1, W)), full((1, W)),
    ]
    g1 = pl.BlockSpec((N_GROUPS, tm, GROUP), lambda i: (0, i, 0))
    g2 = pl.BlockSpec((2, N_GROUPS, tm, GROUP), lambda i: (0, 0, i, 0))
    s1 = jax.ShapeDtypeStruct((N_GROUPS, T, GROUP), BF16)
    s2 = jax.ShapeDtypeStruct((2, N_GROUPS, T, GROUP), BF16)
    s2f = jax.ShapeDtypeStruct((2, N_GROUPS, T, GROUP), F32)
    return pl.pallas_call(
        _rwkvprep_kernel,
        grid=(T // tm,),
        in_specs=specs,
        out_specs=[g1] * 5 + [g2] * 3,
        out_shape=[s1] * 5 + [s2f, s2, s2],
        compiler_params=_cparams(("parallel",)),
    )(z, z, z, z, z, z, z, z, z, z, p['rwkv_conv'], p['w2bd'], p['a2bd'], p['g2p'], p['w0'], p['a0'],
      p['k_k'].reshape(1, W), p['k_a'].reshape(1, W), p['r_k'].reshape(1, W))


def _fold(x):
    c = CHUNK
    return x[0:c] + x[c:2 * c] + x[2 * c:3 * c] + x[3 * c:4 * c]


def _rwkv_kernel(r_ref, v_ref, kk_ref, logw_ref, kd_ref, ag_ref, tri_ref, ms_ref, mi_ref, bd_ref, bdf_ref, z0_ref,
                 y_ref, zf_ref, z_scr):
    forward = pl.program_id(0) == 0
    c = pl.program_id(1)

    @pl.when(c == 0)
    def _():
        z_scr[...] = z0_ref[0]

    tri = tri_ref[0]
    m_strict = ms_ref[0]
    m_incl = mi_ref[0]
    bd = bd_ref[...]
    bdf = bdf_ref[...]
    row = lax.broadcasted_iota(jnp.int32, (CHUNK, GROUP), 0)
    col = lax.broadcasted_iota(jnp.int32, (CHUNK, GROUP), 1)
    diag = row == (col % RWKV_HEAD_DIM)
    eye_f = jnp.where(diag, 1.0, 0.0)

    def expand(x):
        return jnp.concatenate([x.astype(BF16)] * 4, axis=0) * bd

    def stack(*xs):
        return jnp.concatenate([x.astype(BF16) for x in xs], axis=0)

    cs_all = _dot(tri, jnp.concatenate([h for g in range(N_GROUPS) for h in _split(logw_ref[0, g])], axis=1))

    def prep(g):
        lw = logw_ref[0, g]
        cs = cs_all[:, 2 * g * GROUP:2 * (g + 1) * GROUP]
        linc = cs[:, :GROUP] + cs[:, GROUP:]
        ltot = jnp.where(forward, linc[CHUNK - 1:CHUNK, :], linc[0:1, :])
        e_inc = jnp.exp(linc)
        e_neg = jnp.exp(-linc).astype(BF16)
        e_exc = jnp.exp(linc - lw).astype(BF16)
        e_rem = jnp.exp(ltot - linc).astype(BF16)
        kk = kk_ref[g]
        kd = kd_ref[0, g]
        b = kk * ag_ref[0, g]
        at = -(kk * e_exc)
        rt = r_ref[g].astype(F32) * e_inc
        gram = _dot_nt(stack(at, rt), jnp.concatenate([expand(b * e_neg), expand(kd * e_neg)], axis=0))
        fab = gram[:CHUNK, :GROUP] * m_strict
        return dict(g=g, rt=rt, at=at, e_tot=jnp.exp(ltot), bh=b * e_rem, kh=kd * e_rem, v=v_ref[g],
                    fp=fab, ft=eye_f + fab,
                    fak=gram[:CHUNK, GROUP:] * m_strict, frb=gram[CHUNK:, :GROUP] * m_incl,
                    frk=gram[CHUNK:, GROUP:] * m_incl)

    st = [prep(g) for g in range(N_GROUPS)]
    for s in st:
        res = _dot(stack(s['fp'], s['frb']), expand(s['fp']))
        s['fp'] = res[:CHUNK]
        s['fg'] = s['frb'] + res[CHUNK:]
    for _ in range(4):
        for s in st:
            res = _dot(stack(s['fp'], s['ft'], s['fg']), expand(s['fp']))
            s['fp'] = res[:CHUNK]
            s['ft'] = s['ft'] + res[CHUNK:2 * CHUNK]
            s['fg'] = s['fg'] + res[2 * CHUNK:]
    for s in st:
        res = _dot(stack(s['ft'], s['fg']), expand(s['fp']))
        s['ft'] = s['ft'] + res[:CHUNK]
        s['fg'] = s['fg'] + res[CHUNK:]
    for s in st:
        res = _dot(stack(s['ft'], s['fg']), jnp.concatenate([expand(s['fak']), expand(s['at'])], axis=1))
        s['fta'] = res[:CHUNK, :GROUP]
        s['ff'] = res[CHUNK:, :GROUP] + s['frk']
        s['fa1'] = res[:CHUNK, GROUP:]
        s['rp'] = s['rt'] + res[CHUNK:, GROUP:]
    for s in st:
        res = _dot(stack(s['fta'], s['ff']), expand(s['v']))
        s['fu0'] = res[:CHUNK]
        s['y0'] = res[CHUNK:]
    for s in st:
        zero = jnp.zeros((CHUNK, GROUP), BF16)
        rhs = jnp.concatenate([jnp.concatenate([s['fa1'].astype(BF16), zero], axis=1),
                               jnp.concatenate([zero, s['fu0'].astype(BF16)], axis=1),
                               jnp.concatenate([zero, s['v']], axis=1)], axis=0)
        mn = _dot_tn(stack(s['bh'], s['bh'], s['kh']), rhs)
        s['mf'] = _fold(mn[:, :GROUP] * bdf) + jnp.where(diag, s['e_tot'], 0.0)
        s['nf'] = _fold(mn[:, GROUP:] * bdf)
    for s in st:
        g = s['g']
        zhx = expand(z_scr[g])
        mh, ml = _split(s['mf'])
        res = _dot(jnp.concatenate([s['rp'].astype(BF16), mh, ml], axis=0), zhx)
        y_ref[0, g] = res[:CHUNK] + s['y0']
        z_scr[g] = res[CHUNK:2 * CHUNK] + res[2 * CHUNK:] + s['nf']

    @pl.when(c == pl.num_programs(1) - 1)
    def _():
        zf_ref[0] = z_scr[...]


def _chunk_masks():
    i = np.arange(CHUNK)
    tri = np.stack([i[None, :] <= i[:, None], i[None, :] >= i[:, None]]).astype(np.float32)
    j = np.arange(GROUP)
    same = ((j[:, None] // CHUNK) == (j[None, :] // CHUNK)).astype(np.float32)
    js = j[None, :] % CHUNK
    strict = np.stack([js < i[:, None], js > i[:, None]]).astype(np.float32)
    incl = np.stack([js <= i[:, None], js >= i[:, None]]).astype(np.float32)
    return jnp.asarray(tri, BF16), jnp.asarray(strict), jnp.asarray(incl), jnp.asarray(same, BF16), jnp.asarray(same)


def rwkv_scan(r, v, kk, logw, kd, ag, z0):
    T = r.shape[1]
    nc = T // CHUNK
    tri, strict, incl, bd, bdf = _chunk_masks()
    order = lambda d, c: c + d * (nc - 1 - 2 * c)
    b1 = pl.BlockSpec((N_GROUPS, CHUNK, GROUP), lambda d, c: (0, order(d, c), 0))
    b2 = pl.BlockSpec((1, N_GROUPS, CHUNK, GROUP), lambda d, c: (d, 0, order(d, c), 0))
    per_dir = lambda shape: pl.BlockSpec((1,) + shape, lambda d, c: (d,) + (0,) * len(shape))
    full = lambda shape: pl.BlockSpec(shape, lambda d, c: (0,) * len(shape))
    zspec = per_dir((N_GROUPS, CHUNK, GROUP))
    return pl.pallas_call(
        _rwkv_kernel,
        grid=(2, nc),
        in_specs=[b1, b1, b1, b2, b2, b2, per_dir((CHUNK, CHUNK)), per_dir((CHUNK, GROUP)), per_dir((CHUNK, GROUP)),
                  full((GROUP, GROUP)), full((GROUP, GROUP)), zspec],
        out_specs=[b2, zspec],
        out_shape=[jax.ShapeDtypeStruct((2, N_GROUPS, T, GROUP), F32),
                   jax.ShapeDtypeStruct((2, N_GROUPS, CHUNK, GROUP), F32)],
        scratch_shapes=[pltpu.VMEM((N_GROUPS, CHUNK, GROUP), F32)],
        compiler_params=_cparams(("arbitrary", "arbitrary")),
    )(r, v, kk, logw, kd, ag, tri, strict, incl, bd, bdf, z0)


def _rwkvpost_kernel(y_ref, g_ref, bonus_ref, gw_ref, gb_ref, o_ref):
    for gi in range(N_GROUPS):
        y = y_ref[0, gi] + y_ref[1, gi]
        mu = _seg_sum(y) * (1.0 / RWKV_HEAD_DIM)
        yc = y - mu
        var = _seg_sum(yc * yc) * (1.0 / RWKV_HEAD_DIM)
        sl = slice(gi * GROUP, (gi + 1) * GROUP)
        yn = yc * lax.rsqrt(var + GN_EPS) * gw_ref[:, sl] + gb_ref[:, sl]
        o_ref[:, sl] = ((yn + bonus_ref[gi].astype(F32)) * g_ref[gi].astype(F32)).astype(o_ref.dtype)


def rwkv_post(y, g, bonus, gn_w, gn_b):
    T = y.shape[2]
    tm = min(256, T)
    W = RWKV_WIDTH
    return pl.pallas_call(
        _rwkvpost_kernel,
        grid=(T // tm,),
        in_specs=[pl.BlockSpec((2, N_GROUPS, tm, GROUP), lambda i: (0, 0, i, 0)),
                  pl.BlockSpec((N_GROUPS, tm, GROUP), lambda i: (0, i, 0)),
                  pl.BlockSpec((N_GROUPS, tm, GROUP), lambda i: (0, i, 0)),
                  pl.BlockSpec((1, W), lambda i: (0, 0)), pl.BlockSpec((1, W), lambda i: (0, 0))],
        out_specs=pl.BlockSpec((tm, W), lambda i: (i, 0)),
        out_shape=jax.ShapeDtypeStruct((T, W), BF16),
        compiler_params=_cparams(("parallel",)),
    )(y, g, bonus, gn_w.reshape(1, W), gn_b.reshape(1, W))


def rwkv_mix(z, p, z0):
    r, v, kk, g, bonus, logw, kd, ag = rwkv_prep(z, p)
    y, zf = rwkv_scan(r, v, kk, logw, kd, ag, z0)
    return rwkv_post(y, g, bonus, p['gn_w'], p['gn_b']), zf


def _lora_heads(l, w2, a2, g2):
    W = RWKV_WIDTH
    zw = jnp.zeros((W_LORA, W), F32)
    w2bd = jnp.concatenate([jnp.concatenate([w2[l, 0], zw], axis=1), jnp.concatenate([zw, w2[l, 1]], axis=1)], axis=0)
    a2bd = jnp.concatenate([jnp.concatenate([a2[l, 0], zw], axis=1), jnp.concatenate([zw, a2[l, 1]], axis=1)], axis=0)
    g2p = jnp.pad(g2[l], ((0, LORA_PAD - 256 - G_LORA), (0, 0)))
    return dict(w2bd=w2bd.astype(BF16), a2bd=a2bd.astype(BF16), g2p=g2p.astype(BF16))


def _layer(x, ctx, mod_x, mod_c, p, bias_tab, ctx_out, big, li, has_next):
    Dm = D_MODEL
    l = 0
    sh1, sc1, gt1, sh2, sc2, gt2 = [mod_x[i * Dm:(i + 1) * Dm] for i in range(6)]
    csh1, csc1, cgt1, csh2, csc2, cgt2 = [mod_c[i * Dm:(i + 1) * Dm] for i in range(6)]
    zx, w_out_b = norm_mod_matmul(x, p['norm1_w'], sc1, sh1, p['w_in'], l, side=('plain', big['w_out'], li))
    zc = norm_mod_matmul(ctx, p['norm1_w'], csc1, csh1, p['w_in'], l)
    fx = fourier_mix(zx, p['fourier_w'], p['fourier_b'])
    qx, kx = qk_prep(zx, p['q_norm_w'], p['k_norm_w'], rope=True)
    qc, kc = qk_prep(zc, p['q_norm_w'], p['k_norm_w'], rope=False)
    ax = na_attention(qx, kx, zx, kc, zc, bias_tab)
    z0 = jnp.zeros((2, N_GROUPS, CHUNK, GROUP), F32)
    rc, zf = rwkv_mix(zc, p, z0)
    rx, _ = rwkv_mix(zx, p, zf)
    x, w_ffn_in_b = matmul_residual([fx, ax, rx], w_out_b, l, x, gt1, side=('plain', big['w_ffn_in'], li))
    u, w_ffn_out_b = norm_mod_matmul(x, p['norm2_w'], sc2, sh2, w_ffn_in_b, l, out_dtype=BF16,
                                     side=('plain', big['w_ffn_out'], li))
    gated = [conv_gate(u, p['ffn_conv'])]
    if has_next:
        x, w_in_next = matmul_residual(gated, w_ffn_out_b, l, x, gt2, side=('win', big['w_in'], li + 1))
    else:
        x, w_in_next = matmul_residual(gated, w_ffn_out_b, l, x, gt2), None
    if ctx_out:
        fc = fourier_mix_ctx(zc, p['fourier_w'], p['fourier_b'])
        ac = ctx_attention(qc, kc, zc)
        ctx = matmul_residual([fc, ac, rc], w_out_b, l, ctx, cgt1)
        uc = norm_mod_matmul(ctx, p['norm2_w'], csc2, csh2, w_ffn_in_b, l, out_dtype=BF16)
        ctx = matmul_residual([conv_gate(uc, p['ffn_conv'])], w_ffn_out_b, l, ctx, cgt2)
    return x, ctx, w_in_next


def kernel(x, c, ctx, c_ctx, ada_w, ada_b, norm1_w, norm2_w, w_in, fourier_w, fourier_b, q_norm_w, k_norm_w, rpb,
           rwkv_conv, w0, w2, a0, a2, g2, k_k, k_a, r_k, gn_w, gn_b, w_out, ffn_conv, w_ffn_in, w_ffn_out):
    L = ada_w.shape[0]
    xs = x[0]
    cs = ctx[0]
    mods = ada_mod(jnp.concatenate([c, c_ctx[None, :]], axis=0), ada_w, ada_b)
    bias_tabs = rpb_bias_tables(rpb)
    big = dict(w_in=w_in, w_out=w_out, w_ffn_in=w_ffn_in, w_ffn_out=w_ffn_out)
    w_in_b = cast_layer('win', w_in, 0)
    for l in range(L):
        p = dict(w_in=w_in_b)
        p.update(_lora_heads(l, w2, a2, g2))
        p.update(norm1_w=norm1_w[l], norm2_w=norm2_w[l], fourier_w=fourier_w[l], fourier_b=fourier_b[l],
                 q_norm_w=q_norm_w[l], k_norm_w=k_norm_w[l], rwkv_conv=rwkv_conv[l], w0=w0[l], a0=a0[l],
                 k_k=k_k[l], k_a=k_a[l], r_k=r_k[l], gn_w=gn_w[l], gn_b=gn_b[l], ffn_conv=ffn_conv[l])
        xs, cs, w_in_b = _layer(xs, cs, mods[l, 0], mods[l, 1], p, bias_tabs[l], l < L - 1, big, l, l < L - 1)
    return xs[None]
```
